```python
import jax, jax.numpy as jnp
from jax import lax
import numpy as np

D_MODEL = 1024
BATCH = 16
SEQ = 2048
DEPTH = 1
DEC_BATCH = 128
DEC_SEQ = 8
PAST_LEN = 8192
PAGE_SIZE = 128

HEAD_DIM = 64
N_ATT_HEADS = 8
N_KV_HEADS = 4
Q_PER_KV = N_ATT_HEADS // N_KV_HEADS
D_ATT = N_ATT_HEADS * HEAD_DIM
D_KV = N_KV_HEADS * HEAD_DIM
ROT_DIM = HEAD_DIM // 4
ROPE_THETA = 500000.0
DILATED_GROUPS = ((128, 1), (512, 4), (2048, 16))
MAX_WINDOW = max(w for w, _ in DILATED_GROUPS)
ATT_BLOCK = 128
ATT_SCALE = HEAD_DIM ** -0.5
NEG_BIG = -1e30
SSM_HEAD_DIM = 64
N_SSM_HEADS = 8
D_SSM = N_SSM_HEADS * SSM_HEAD_DIM
SSM_GROUPS = 2
HEADS_PER_SSM_GROUP = N_SSM_HEADS // SSM_GROUPS
D_STATE = 128
CONV_WIDTH = 4
SSD_CHUNK = 128
D_XBC = D_SSM + 2 * SSM_GROUPS * D_STATE
D_MIX = D_ATT + D_SSM
IN_SPLITS = (D_ATT, D_ATT + D_KV, D_ATT + 2 * D_KV, D_ATT + 2 * D_KV + D_SSM,
             D_ATT + 2 * D_KV + D_SSM + D_XBC)
D_IN_PROJ = D_ATT + 2 * D_KV + D_SSM + D_XBC + N_SSM_HEADS
D_FF = 4 * D_MODEL
RMS_EPS = 1e-5

kernel_name = "hymba_dilated_ssd_decoder_step"


def rmsnorm(x, g):
    xf = x.astype(jnp.float32)
    xf = xf * lax.rsqrt(jnp.mean(xf * xf, axis=-1, keepdims=True) + RMS_EPS)
    return xf.astype(x.dtype) * g


def rope(x, pos):
    half = ROT_DIM // 2
    inv = ROPE_THETA ** (-jnp.arange(0, ROT_DIM, 2, dtype=jnp.float32) / ROT_DIM)
    ang = pos.astype(jnp.float32)[:, None] * inv[None, :]
    shp = (ang.shape[0],) + (1,) * (x.ndim - 3) + (half,)
    cos, sin = jnp.cos(ang).reshape(shp), jnp.sin(ang).reshape(shp)
    x1 = x[..., :half].astype(jnp.float32)
    x2 = x[..., half:ROT_DIM].astype(jnp.float32)
    rot = jnp.concatenate([x1 * cos - x2 * sin, x2 * cos + x1 * sin], axis=-1)
    return jnp.concatenate([rot.astype(x.dtype), x[..., ROT_DIM:]], axis=-1)


def in_proj(h, pos, norm_mix, w_in):
    u = rmsnorm(h, norm_mix) @ w_in
    q, k, v, z, xbc, dt_raw = jnp.split(u, IN_SPLITS, axis=-1)
    b, s = h.shape[:2]
    q = rope(q.reshape(b, s, N_KV_HEADS, Q_PER_KV, HEAD_DIM), pos) * ATT_SCALE
    k = rope(k.reshape(b, s, N_KV_HEADS, HEAD_DIM), pos)
    v = v.reshape(b, s, N_KV_HEADS, HEAD_DIM)
    return q, k, v, z, xbc, dt_raw


def dilated_group_prompt(q, k, v, dilation, n_keys):
    b, s = q.shape[:2]
    L = s // dilation
    nb = -(-L // ATT_BLOCK)
    Lp = nb * ATT_BLOCK

    def to_sub(t):
        rest = t.shape[2:]
        t = t.reshape((b, L, dilation) + rest)
        t = jnp.swapaxes(t, 1, 2).reshape((b * dilation, L) + rest)
        t = jnp.pad(t, ((0, 0), (0, Lp - L)) + ((0, 0),) * len(rest))
        return t.reshape((b * dilation, nb, ATT_BLOCK) + rest)

    def band(t):
        prev = jnp.pad(t[:, :-1], ((0, 0), (1, 0)) + ((0, 0),) * (t.ndim - 2))
        return jnp.concatenate([prev, t], axis=2)

    qs = to_sub(q)
    kb, vb = band(to_sub(k)), band(to_sub(v))
    scores = jnp.einsum('znikgd,znjkd->znkgij', qs, kb, preferred_element_type=jnp.float32)
    i = jnp.arange(ATT_BLOCK)[:, None]
    j = jnp.arange(2 * ATT_BLOCK)[None, :]
    dist = i + ATT_BLOCK - j
    key_sub = jnp.arange(nb)[:, None, None] * ATT_BLOCK - ATT_BLOCK + j[None]
    mask = (dist >= 0)[None] & (dist <= n_keys)[None] & (key_sub >= 0)
    scores = jnp.where(mask[None, :, None, None], scores, NEG_BIG)
    lse = jax.nn.logsumexp(scores, axis=-1)
    p = jnp.exp(scores - lse[..., None])
    o = jnp.einsum('znkgij,znjkd->znikgd', p.astype(vb.dtype), vb)
    lse = jnp.moveaxis(lse, -1, 2)

    def from_sub(t):
        rest = t.shape[3:]
        t = t.reshape((b, dilation, Lp) + rest)[:, :, :L]
        return jnp.swapaxes(t, 1, 2).reshape((b, s) + rest)

    return from_sub(o), from_sub(lse)


def dilated_group_sample(q, k_ext, v_ext, dilation, n_keys, n_past):
    t = q.shape[1]
    idx = n_past + jnp.arange(t)[:, None] - dilation * jnp.arange(n_keys + 1)[None, :]
    valid = idx >= 0
    idx = jnp.maximum(idx, 0)
    kg, vg = k_ext[:, idx], v_ext[:, idx]
    scores = jnp.einsum('btkgd,btjkd->btkgj', q, kg, preferred_element_type=jnp.float32)
    scores = jnp.where(valid[None, :, None, None, :], scores, NEG_BIG)
    lse = jax.nn.logsumexp(scores, axis=-1)
    p = jnp.exp(scores - lse[..., None])
    o = jnp.einsum('btkgj,btjkd->btkgd', p.astype(vg.dtype), vg)
    return o, lse


def combine_groups(groups):
    o = jnp.stack([g[0] for g in groups]).astype(jnp.float32)
    lse = jnp.stack([g[1] for g in groups])
    w = jax.nn.softmax(lse, axis=0)
    return jnp.sum(w[..., None] * o, axis=0)


def causal_conv(xbc, prefix, conv_w, conv_b):
    s = xbc.shape[1]
    ext = jnp.concatenate([prefix.astype(xbc.dtype), xbc], axis=1)
    y = conv_b
    for tap in range(CONV_WIDTH):
        y = y + ext[:, tap:tap + s] * conv_w[tap]
    return jax.nn.silu(y), ext[:, s:]


def ssd_chunked(x, dt, A, Bm, Cm, h0):
    f32 = jnp.float32
    x, Bm, Cm, h0 = x.astype(f32), Bm.astype(f32), Cm.astype(f32), h0.astype(f32)
    b, s, h, p = x.shape
    q = min(SSD_CHUNK, s)
    nc = -(-s // q)
    pad = nc * q - s

    def chunks(t):
        t = jnp.pad(t, ((0, 0), (0, pad)) + ((0, 0),) * (t.ndim - 2))
        return t.reshape((b, nc, q) + t.shape[2:])

    x, dt, Bm, Cm = chunks(x), chunks(dt), chunks(Bm), chunks(Cm)
    a_cum = jnp.cumsum(dt * A, axis=2)
    causal = jnp.tril(jnp.ones((q, q), dtype=bool))
    seg = a_cum[:, :, :, None, :] - a_cum[:, :, None, :, :]
    decay = jnp.exp(jnp.where(causal[:, :, None], seg, -jnp.inf))
    cb = jnp.einsum('bcihn,bcjhn->bcijh', Cm, Bm)
    y_diag = jnp.einsum('bcijh,bcjhp->bcihp', cb * decay * dt[:, :, None], x)
    to_end = jnp.exp(a_cum[:, :, -1:] - a_cum) * dt
    chunk_states = jnp.einsum('bcjhn,bcjh,bcjhp->bchpn', Bm, to_end, x)
    chunk_decay = jnp.exp(a_cum[:, :, -1])

    def step(hc, inp):
        dec, st = inp
        return dec[:, :, None, None] * hc + st, hc

    h_final, h_prev = lax.scan(step, h0, (jnp.moveaxis(chunk_decay, 1, 0),
                                          jnp.moveaxis(chunk_states, 1, 0)))
    h_prev = jnp.moveaxis(h_prev, 0, 1)
    y_off = jnp.einsum('bcihn,bchpn,bcih->bcihp', Cm, h_prev, jnp.exp(a_cum))
    y = (y_diag + y_off).reshape(b, nc * q, h, p)[:, :s]
    return y, h_final


def ssm_branch(z, xbc, dt_raw, conv_prefix, h0, conv_w, conv_b, dt_bias, a_log, d_skip, ssm_norm):
    b, s = z.shape[:2]
    xbc, conv_state = causal_conv(xbc, conv_prefix, conv_w, conv_b)
    xs = xbc[..., :D_SSM].reshape(b, s, N_SSM_HEADS, SSM_HEAD_DIM)
    Bm = xbc[..., D_SSM:D_SSM + SSM_GROUPS * D_STATE].reshape(b, s, SSM_GROUPS, D_STATE)
    Cm = xbc[..., D_SSM + SSM_GROUPS * D_STATE:].reshape(b, s, SSM_GROUPS, D_STATE)
    Bm = jnp.repeat(Bm, HEADS_PER_SSM_GROUP, axis=2)
    Cm = jnp.repeat(Cm, HEADS_PER_SSM_GROUP, axis=2)
    dt = jax.nn.softplus(dt_raw.astype(jnp.float32) + dt_bias.astype(jnp.float32))
    A = -jnp.exp(a_log.astype(jnp.float32))
    y, h_final = ssd_chunked(xs, dt, A, Bm, Cm, h0)
    y = y + d_skip.astype(jnp.float32)[:, None] * xs.astype(jnp.float32)
    y = y.reshape(b, s, D_SSM) * jax.nn.silu(z.astype(jnp.float32))
    yg = y.reshape(b, s, SSM_GROUPS, D_SSM // SSM_GROUPS)
    yg = yg * lax.rsqrt(jnp.mean(yg * yg, axis=-1, keepdims=True) + RMS_EPS)
    y = yg.reshape(b, s, D_SSM).astype(z.dtype) * ssm_norm
    return y, conv_state, h_final


def out_and_mlp(h, att, ssm, w_out, norm_mlp, w_up, w_down):
    h = h + jnp.concatenate([att, ssm], axis=-1) @ w_out
    u = jax.nn.relu(rmsnorm(h, norm_mlp) @ w_up)
    return h + (u * u) @ w_down


def layer_prompt(h, lw):
    w_in, w_out, conv_w, conv_b, dt_bias, a_log, d_skip, ssm_norm, norm_mix, norm_mlp, w_up, w_down = lw
    b, s = h.shape[:2]
    q, k, v, z, xbc, dt_raw = in_proj(h, jnp.arange(s), norm_mix, w_in)
    groups = [dilated_group_prompt(q, k, v, d, w // d) for (w, d) in DILATED_GROUPS]
    att = combine_groups(groups).reshape(b, s, D_ATT).astype(h.dtype)
    conv_prefix = jnp.zeros((b, CONV_WIDTH - 1, D_XBC), h.dtype)
    h0 = jnp.zeros((b, N_SSM_HEADS, SSM_HEAD_DIM, D_STATE), jnp.float32)
    ssm, conv_state, ssm_state = ssm_branch(z, xbc, dt_raw, conv_prefix, h0, conv_w, conv_b,
                                            dt_bias, a_log, d_skip, ssm_norm)
    h = out_and_mlp(h, att, ssm, w_out, norm_mlp, w_up, w_down)
    keep = min(MAX_WINDOW, s)
    return h, k[:, s - keep:], v[:, s - keep:], conv_state, ssm_state


def layer_sample(h, cache_k, cache_v, conv_prefix, h0, lw):
    w_in, w_out, conv_w, conv_b, dt_bias, a_log, d_skip, ssm_norm, norm_mix, norm_mlp, w_up, w_down = lw
    b, t = h.shape[:2]
    n_past = cache_k.shape[1]
    q, k, v, z, xbc, dt_raw = in_proj(h, PAST_LEN + jnp.arange(t), norm_mix, w_in)
    k_ext = jnp.concatenate([cache_k.astype(k.dtype), k], axis=1)
    v_ext = jnp.concatenate([cache_v.astype(v.dtype), v], axis=1)
    groups = [dilated_group_sample(q, k_ext, v_ext, d, w // d, n_past) for (w, d) in DILATED_GROUPS]
    att = combine_groups(groups).reshape(b, t, D_ATT).astype(h.dtype)
    ssm, conv_state, ssm_state = ssm_branch(z, xbc, dt_raw, conv_prefix, h0, conv_w, conv_b,
                                            dt_bias, a_log, d_skip, ssm_norm)
    h = out_and_mlp(h, att, ssm, w_out, norm_mlp, w_up, w_down)
    return h, k_ext[:, t:], v_ext[:, t:], conv_state, ssm_state


def setup_inputs(seed: int = 0) -> dict:
    key = jax.random.key(seed)
    ks = jax.random.split(key, 20)
    f32 = jnp.float32
    n_past = min(MAX_WINDOW, PAST_LEN)

    def nrm(k, shape, scale):
        return scale * jax.random.normal(k, shape, f32)

    dt0 = jnp.exp(jax.random.uniform(ks[10], (DEPTH, N_SSM_HEADS), f32,
                                     np.log(1e-3), np.log(1e-1)))
    return {
        "x_prompt": nrm(ks[0], (BATCH, SEQ, D_MODEL), 1.0),
        "x_sample": nrm(ks[1], (DEC_BATCH, DEC_SEQ, D_MODEL), 1.0),
        "cache_k": nrm(ks[2], (DEPTH, DEC_BATCH, n_past, N_KV_HEADS, HEAD_DIM), 1.0),
        "cache_v": nrm(ks[3], (DEPTH, DEC_BATCH, n_past, N_KV_HEADS, HEAD_DIM), 1.0),
        "state_conv": nrm(ks[4], (DEPTH, DEC_BATCH, CONV_WIDTH - 1, D_XBC), 1.0),
        "state_ssm": nrm(ks[5], (DEPTH, DEC_BATCH, N_SSM_HEADS, SSM_HEAD_DIM, D_STATE), 0.1),
        "w_in": nrm(ks[6], (DEPTH, D_MODEL, D_IN_PROJ), D_MODEL ** -0.5),
        "w_out": nrm(ks[7], (DEPTH, D_MIX, D_MODEL), D_MIX ** -0.5),
        "conv_w": nrm(ks[8], (DEPTH, CONV_WIDTH, D_XBC), CONV_WIDTH ** -0.5),
        "conv_b": nrm(ks[9], (DEPTH, D_XBC), 0.01),
        "dt_bias": dt0 + jnp.log(-jnp.expm1(-dt0)),
        "a_log": jnp.log(jax.random.uniform(ks[11], (DEPTH, N_SSM_HEADS), f32, 1.0, 16.0)),
        "d_skip": 1.0 + nrm(ks[12], (DEPTH, N_SSM_HEADS), 0.1),
        "ssm_norm": 1.0 + nrm(ks[13], (DEPTH, D_SSM), 0.01),
        "norm_mix": 1.0 + nrm(ks[14], (DEPTH, D_MODEL), 0.01),
        "norm_mlp": 1.0 + nrm(ks[15], (DEPTH, D_MODEL), 0.01),
        "w_up": nrm(ks[16], (DEPTH, D_MODEL, D_FF), D_MODEL ** -0.5),
        "w_down": nrm(ks[17], (DEPTH, D_FF, D_MODEL), D_FF ** -0.5),
        "norm_final": 1.0 + nrm(ks[18], (D_MODEL,), 0.01),
    }


def reference(x_prompt, x_sample, cache_k, cache_v, state_conv, state_ssm,
              w_in, w_out, conv_w, conv_b, dt_bias, a_log, d_skip, ssm_norm,
              norm_mix, norm_mlp, w_up, w_down, norm_final):
    hp, hs = x_prompt, x_sample
    kp, vp, cp, sp = [], [], [], []
    ksm, vsm, csm, ssm_s = [], [], [], []
    for l in range(DEPTH):
        lw = (w_in[l], w_out[l], conv_w[l], conv_b[l], dt_bias[l], a_log[l], d_skip[l],
              ssm_norm[l], norm_mix[l], norm_mlp[l], w_up[l], w_down[l])
        hp, k1, v1, c1, s1 = layer_prompt(hp, lw)
        hs, k2, v2, c2, s2 = layer_sample(hs, cache_k[l], cache_v[l], state_conv[l], state_ssm[l], lw)
        kp.append(k1); vp.append(v1); cp.append(c1); sp.append(s1)
        ksm.append(k2); vsm.append(v2); csm.append(c2); ssm_s.append(s2)
    y_prompt = rmsnorm(hp, norm_final)
    y_sample = rmsnorm(hs, norm_final)
    return (y_prompt, y_sample,
            jnp.stack(kp), jnp.stack(vp), jnp.stack(cp), jnp.stack(sp),
            jnp.stack(ksm), jnp.stack(vsm), jnp.stack(csm), jnp.stack(ssm_s))
```

```python
import functools

import numpy as np
import jax
import jax.numpy as jnp
from jax import lax
from jax.experimental import pallas as pl
from jax.experimental.pallas import tpu as pltpu

F32 = jnp.float32
BF16 = jnp.bfloat16

LANES = 128
SUBLANES = 8

D_MODEL = 1024
HEAD_DIM = 64
N_KV_HEADS = 4
Q_PER_KV = 2
D_ATT = 512
D_KV = 256
ROT_HALF = 8
ROPE_THETA = 500000.0
DILATIONS = (1, 4, 16)
N_KEYS = 128
MAX_WINDOW = 2048
ATT_BLK = 128
PAST_LEN = 8192
ATT_SCALE = HEAD_DIM ** -0.5
NEG_BIG = -1e30
N_SSM_HEADS = 8
SSM_HEAD_DIM = 64
D_SSM = 512
SSM_GROUPS = 2
D_STATE = 128
CONV_WIDTH = 4
CHUNK = 128
D_XBC = 1024
D_FF = 4096
RMS_EPS = 1e-5
D_IN_MAIN = D_ATT + 2 * D_KV + D_SSM + D_XBC
D_IN_PAD = D_IN_MAIN + LANES

VMEM_LIMIT = 56 * 1024 * 1024
ROW_TILE = 512
FF_TILE = 512
SAMPLE_SEQS = CHUNK // 8

_NT = (((1,), (1,)), ((), ()))


def _params(n_axes):
    return pltpu.CompilerParams(dimension_semantics=("arbitrary",) * n_axes,
                                vmem_limit_bytes=VMEM_LIMIT)


def _resident(shape):
    return pl.BlockSpec(shape, lambda *_: (0,) * len(shape), pipeline_mode=pl.Buffered(1))


def _rms(x, g):
    return x * lax.rsqrt(jnp.mean(x * x, axis=-1, keepdims=True) + RMS_EPS) * g


def _silu(x):
    return x * (1.0 / (1.0 + jnp.exp(-x)))


def _softplus(x):
    return jnp.maximum(x, 0.0) + jnp.log1p(jnp.exp(-jnp.abs(x)))


def _in_proj_body(x_ref, g_ref, w_ref, rope_ref, q_ref, k_ref, v_ref, z_ref, xbc_ref, dt_ref):
    xb = _rms(x_ref[...], g_ref[...]).astype(BF16)
    tm = xb.shape[0]
    first_half = (lax.broadcasted_iota(jnp.int32, (tm, LANES), 1) % HEAD_DIM) < ROT_HALF

    def proj(lo, hi):
        return jnp.dot(xb, w_ref[:, lo:hi], preferred_element_type=F32)

    def rope(u, cos, sin):
        partner = jnp.where(first_half, pltpu.roll(u, LANES - ROT_HALF, 1), pltpu.roll(u, ROT_HALF, 1))
        return u * cos + partner * sin

    cq, sq = rope_ref[:, 0:128], rope_ref[:, 128:256]
    ck, sk = rope_ref[:, 256:384], rope_ref[:, 384:512]
    q = proj(0, D_ATT)
    for j in range(D_ATT // LANES):
        q_ref[j] = rope(q[:, j * LANES:(j + 1) * LANES], cq, sq)
    k = proj(D_ATT, D_ATT + D_KV)
    for j in range(D_KV // LANES):
        k_ref[:, j * LANES:(j + 1) * LANES] = rope(k[:, j * LANES:(j + 1) * LANES], ck, sk)
    v_ref[...] = proj(D_ATT + D_KV, D_ATT + 2 * D_KV)
    z_ref[...] = proj(D_ATT + 2 * D_KV, D_ATT + 2 * D_KV + D_SSM)
    xbc_ref[...] = proj(D_ATT + 2 * D_KV + D_SSM, D_IN_MAIN)
    dt_ref[...] = proj(D_IN_MAIN, D_IN_PAD)


def _in_proj(x2d, norm_mix, w_pad, rope_tab):
    rows = x2d.shape[0]
    tm = min(ROW_TILE, rows)
    n_tab = rope_tab.shape[0] // tm
    row_blk = lambda w: pl.BlockSpec((tm, w), lambda i: (i, 0))
    return pl.pallas_call(
        _in_proj_body,
        grid=(rows // tm,),
        in_specs=[row_blk(D_MODEL), _resident((1, D_MODEL)), _resident((D_MODEL, D_IN_PAD)),
                  pl.BlockSpec((tm, 4 * LANES), lambda i: (i % n_tab, 0))],
        out_specs=[pl.BlockSpec((4, tm, LANES), lambda i: (0, i, 0)),
                   row_blk(D_KV), row_blk(D_KV), row_blk(D_SSM), row_blk(D_XBC), row_blk(LANES)],
        out_shape=[jax.ShapeDtypeStruct((4, rows, LANES), F32),
                   jax.ShapeDtypeStruct((rows, D_KV), F32), jax.ShapeDtypeStruct((rows, D_KV), F32),
                   jax.ShapeDtypeStruct((rows, D_SSM), F32), jax.ShapeDtypeStruct((rows, D_XBC), F32),
                   jax.ShapeDtypeStruct((rows, LANES), F32)],
        compiler_params=_params(1),
        name="in_proj",
    )(x2d, norm_mix, w_pad, rope_tab)


def _rope_table(pos):
    inv = ROPE_THETA ** (-jnp.arange(0, 2 * ROT_HALF, 2, dtype=F32) / (2 * ROT_HALF))
    ang = pos.astype(F32)[:, None] * inv[None, :]
    cos, sin = jnp.cos(ang), jnp.sin(ang)
    n = pos.shape[0]
    ones = jnp.ones((n, HEAD_DIM - 2 * ROT_HALF), F32)
    cos_h = jnp.concatenate([cos, cos, ones], axis=1)
    sin_h = jnp.concatenate([-sin, sin, 0.0 * ones], axis=1)
    cos_l, sin_l = jnp.tile(cos_h, (1, 2)), jnp.tile(sin_h, (1, 2))
    return jnp.concatenate([cos_l * ATT_SCALE, sin_l * ATT_SCALE, cos_l, sin_l], axis=1)


def _attn_prompt_body(q_ref, k_ref, v_ref, band_ref, caus_ref, o_ref, m_s, l_s, acc_s):
    seq = k_ref.shape[0]
    e0 = lax.broadcasted_iota(jnp.int32, (ATT_BLK, LANES), 1) < HEAD_DIM

    def unit(q_rows, k_rows, bias_ref, first_group):
        zero = jnp.zeros((ATT_BLK, LANES), F32)
        parts = []
        for g in range(Q_PER_KV):
            qg = q_ref.at[g][q_rows, :]
            parts += [jnp.where(e0, qg, zero), jnp.where(e0, zero, qg)]
        lhs = jnp.concatenate(parts, axis=0).astype(BF16)
        kb = k_ref[k_rows, :].astype(BF16)
        vb = v_ref[k_rows, :].astype(BF16)
        nk = kb.shape[0]
        s = lax.dot_general(lhs, kb, _NT, preferred_element_type=F32)
        s = (s.reshape(4, ATT_BLK, nk) + bias_ref[...][None]).reshape(4 * ATT_BLK, nk)
        m = jnp.max(s, axis=1, keepdims=True)
        p = jnp.exp(s - m)
        l = jnp.sum(p, axis=1, keepdims=True)
        pv = jnp.dot(p.astype(BF16), vb, preferred_element_type=F32)
        mb = jnp.broadcast_to(m, pv.shape)
        lb = jnp.broadcast_to(l, pv.shape)
        for g in range(Q_PER_KV):
            r0, r1, r2 = 2 * g * ATT_BLK, (2 * g + 1) * ATT_BLK, (2 * g + 2) * ATT_BLK
            acc = jnp.where(e0, pv[r0:r1], pv[r1:r2])
            mg = jnp.where(e0, mb[r0:r1], mb[r1:r2])
            lg = jnp.where(e0, lb[r0:r1], lb[r1:r2])
            m_g, l_g, acc_g = m_s.at[g], l_s.at[g], acc_s.at[g]
            if first_group:
                m_g[q_rows, :] = mg
                l_g[q_rows, :] = lg
                acc_g[q_rows, :] = acc
            else:
                m_old = m_g[q_rows, :]
                m_new = jnp.maximum(m_old, mg)
                a_old = jnp.exp(m_old - m_new)
                a_new = jnp.exp(mg - m_new)
                m_g[q_rows, :] = m_new
                l_g[q_rows, :] = a_old * l_g[q_rows, :] + a_new * lg
                acc_g[q_rows, :] = a_old * acc_g[q_rows, :] + a_new * acc

    for gi, d in enumerate(DILATIONS):
        sub_len = seq // d
        first_group = gi == 0

        def residue(r, carry, d=d, sub_len=sub_len, first_group=first_group):
            def rows(start, n):
                if d == 1:
                    return pl.ds(start if isinstance(start, int) else pl.multiple_of(start, ATT_BLK), n)
                return pl.ds(r + start * d, n, stride=d)

            unit(rows(0, ATT_BLK), rows(0, ATT_BLK), caus_ref, first_group)

            def block(n, c):
                unit(rows(n * ATT_BLK, ATT_BLK), rows((n - 1) * ATT_BLK, 2 * ATT_BLK), band_ref, first_group)
                return c

            lax.fori_loop(1, sub_len // ATT_BLK, block, 0)
            return carry

        lax.fori_loop(0, d, residue, 0)

    def finish(i, carry):
        rows = pl.ds(pl.multiple_of(i * 2 * ATT_BLK, 2 * ATT_BLK), 2 * ATT_BLK)
        for g in range(Q_PER_KV):
            o_ref.at[g][rows, :] = (acc_s.at[g][rows, :] / l_s.at[g][rows, :]).astype(o_ref.dtype)
        return carry

    lax.fori_loop(0, seq // (2 * ATT_BLK), finish, 0)


def _window_bias():
    i = np.arange(ATT_BLK)[:, None]
    j = np.arange(2 * ATT_BLK)[None, :]
    dist = i + ATT_BLK - j
    band = np.where((dist >= 0) & (dist <= N_KEYS), 0.0, NEG_BIG).astype(np.float32)
    caus = band[:, ATT_BLK:]
    return jnp.asarray(band), jnp.asarray(caus)


def _attn_prompt(q4, k, v, n_seq, seq):
    band, caus = _window_bias()
    return pl.pallas_call(
        _attn_prompt_body,
        grid=(n_seq, 2),
        in_specs=[pl.BlockSpec((2, seq, LANES), lambda b, pp: (pp, b, 0)),
                  pl.BlockSpec((seq, LANES), lambda b, pp: (b, pp)),
                  pl.BlockSpec((seq, LANES), lambda b, pp: (b, pp)),
                  _resident(band.shape), _resident(caus.shape)],
        out_specs=pl.BlockSpec((2, seq, LANES), lambda b, pp: (pp, b, 0)),
        out_shape=jax.ShapeDtypeStruct(q4.shape, BF16),
        scratch_shapes=[pltpu.VMEM((2, seq, LANES), F32)] * 3,
        compiler_params=_params(2),
        name="attn_prompt",
    )(q4, k, v, band, caus)


def _attn_sample_body(q_ref, kn_ref, vn_ref, ck_ref, cv_ref, mult_ref, att_ref, ko_ref, vo_ref):
    t = kn_ref.shape[0]
    n_past = ck_ref.shape[1]
    ko_ref[0, 0:n_past - t, :] = ck_ref[0, t:n_past, :]
    ko_ref[0, n_past - t:n_past, :] = kn_ref[...]
    vo_ref[0, 0:n_past - t, :] = cv_ref[0, t:n_past, :]
    vo_ref[0, n_past - t:n_past, :] = vn_ref[...]

    mult = mult_ref[...]
    valid = mult > 0.0
    e0 = lax.broadcasted_iota(jnp.int32, (t, LANES), 1) < HEAD_DIM
    zero = jnp.zeros((t, LANES), F32)
    pad = jnp.zeros((LANES - t, LANES), F32)
    for pp in range(N_KV_HEADS // 2):
        lanes = slice(pp * LANES, (pp + 1) * LANES)
        parts = []
        for g in range(Q_PER_KV):
            qg = q_ref[pp * Q_PER_KV + g]
            parts += [jnp.where(e0, qg, zero), jnp.where(e0, zero, qg)]
        lhs = jnp.concatenate(parts, axis=0).astype(BF16)
        kc = ck_ref[0, :, lanes].astype(BF16)
        vc = cv_ref[0, :, lanes].astype(BF16)
        kn = jnp.concatenate([kn_ref[:, lanes], pad], axis=0).astype(BF16)
        vn = jnp.concatenate([vn_ref[:, lanes], pad], axis=0).astype(BF16)
        s = jnp.concatenate([lax.dot_general(lhs, kc, _NT, preferred_element_type=F32),
                             lax.dot_general(lhs, kn, _NT, preferred_element_type=F32)], axis=1)
        s = jnp.where(valid, s, NEG_BIG)
        m = jnp.max(s, axis=1, keepdims=True)
        p = jnp.exp(s - m) * mult
        l = jnp.sum(p, axis=1, keepdims=True)
        pb = p.astype(BF16)
        pv = (jnp.dot(pb[:, :n_past], vc, preferred_element_type=F32)
              + jnp.dot(pb[:, n_past:], vn, preferred_element_type=F32))
        o = pv / l
        for g in range(Q_PER_KV):
            r0 = 2 * g * t
            att_ref[pp * Q_PER_KV + g] = jnp.where(e0, o[r0:r0 + t], o[r0 + t:r0 + 2 * t])


def _key_multiplicity(t, n_past):
    idx = np.concatenate([np.arange(n_past + t), np.full(LANES - t, 10 ** 9)])[None, :]
    dist = n_past + np.arange(t)[:, None] - idx
    mult = np.zeros(dist.shape, np.float32)
    for d in DILATIONS:
        mult += (dist >= 0) & (dist % d == 0) & (dist <= N_KEYS * d)
    return jnp.asarray(np.tile(mult, (4, 1)))


def _attn_sample(q4, k_new, v_new, cache_k, cache_v):
    n_seq, n_past, _ = cache_k.shape
    t = k_new.shape[0] // n_seq
    mult = _key_multiplicity(t, n_past)
    cache_blk = pl.BlockSpec((1, n_past, D_KV), lambda b: (b, 0, 0))
    new_blk = pl.BlockSpec((t, D_KV), lambda b: (b, 0))
    q_blk = pl.BlockSpec((4, t, LANES), lambda b: (0, b, 0))
    return pl.pallas_call(
        _attn_sample_body,
        grid=(n_seq,),
        in_specs=[q_blk, new_blk, new_blk, cache_blk, cache_blk, _resident(mult.shape)],
        out_specs=[q_blk, cache_blk, cache_blk],
        out_shape=[jax.ShapeDtypeStruct(q4.shape, F32),
                   jax.ShapeDtypeStruct(cache_k.shape, F32), jax.ShapeDtypeStruct(cache_v.shape, F32)],
        compiler_params=_params(1),
        name="attn_sample",
    )(q4, k_new, v_new, cache_k, cache_v, mult)


def _conv_silu(cur_ref, rows, prev_tap, cw_ref, cb_ref, xc_s):
    for j in range(D_XBC // LANES):
        lanes = slice(j * LANES, (j + 1) * LANES)
        cur = cur_ref[rows, lanes]
        y = cb_ref[:, lanes]
        for tap in range(CONV_WIDTH):
            k = CONV_WIDTH - 1 - tap
            shifted = cur if k == 0 else prev_tap(k, lanes, pltpu.roll(cur, k, 0))
            y = y + shifted * cw_ref[tap:tap + 1, lanes]
        xc_s[:, lanes] = _silu(y)


def _ssd_scalars(dt_raw, dtb_ref, alog_ref, seg_mask, seg_end):
    dt = _softplus(dt_raw + dtb_ref[...])
    a = dt * (-jnp.exp(alog_ref[...]))
    tril = jnp.where(seg_mask, 1.0, 0.0).astype(F32)
    a_cum = jnp.dot(tril, a, precision=lax.Precision.HIGHEST, preferred_element_type=F32)
    a_end = jnp.dot(seg_end, a_cum, precision=lax.Precision.HIGHEST, preferred_element_type=F32)
    to_end = jnp.exp(a_end - a_cum) * dt
    head = lax.broadcasted_iota(jnp.int32, (LANES, D_SSM), 0)
    lane = lax.broadcasted_iota(jnp.int32, (LANES, D_SSM), 1)
    expand = jnp.where(lane // SSM_HEAD_DIM == head, 1.0, 0.0).astype(F32)
    e_cum = jnp.dot(jnp.exp(a_cum), expand, precision=lax.Precision.HIGHEST, preferred_element_type=F32)
    return dt.T, a_cum, a_cum.T, a_end, to_end.T, e_cum


def _ssd_diag(xc_s, blk, cb, a_cum, a_cum_t, dt_t, seg_mask):
    e0 = lax.broadcasted_iota(jnp.int32, (CHUNK, LANES), 1) < SSM_HEAD_DIM
    x_pair = xc_s[:, blk * LANES:(blk + 1) * LANES].astype(BF16)
    out = []
    for e in range(2):
        h = 2 * blk + e
        seg = jnp.broadcast_to(a_cum[:, h:h + 1], (CHUNK, CHUNK)) - jnp.broadcast_to(a_cum_t[h:h + 1, :], (CHUNK, CHUNK))
        w = cb * jnp.where(seg_mask, jnp.exp(seg), 0.0) * dt_t[h:h + 1, :]
        out.append(jnp.dot(w.astype(BF16), x_pair, preferred_element_type=F32))
    return jnp.where(e0, out[0], out[1])


def _ssd_finish(xc_s, z_ref, rows, y_parts, dsk_ref, nrm_ref, y_ref):
    blocks_per_group = D_SSM // SSM_GROUPS // LANES
    for g in range(SSM_GROUPS):
        gated = []
        for p in range(blocks_per_group):
            blk = g * blocks_per_group + p
            lanes = slice(blk * LANES, (blk + 1) * LANES)
            y = y_parts[blk] + dsk_ref[:, lanes] * xc_s[:, lanes]
            gated.append(y * _silu(z_ref[rows, lanes]))
        ss = sum(jnp.sum(y * y, axis=-1, keepdims=True) for y in gated)
        inv = lax.rsqrt(ss / (D_SSM // SSM_GROUPS) + RMS_EPS)
        for p in range(blocks_per_group):
            blk = g * blocks_per_group + p
            lanes = slice(blk * LANES, (blk + 1) * LANES)
            y_ref[rows, lanes] = (gated[p] * inv * nrm_ref[:, lanes]).astype(y_ref.dtype)


def _ssd_prompt_body(xbc_ref, z_ref, dt_ref, cw_ref, cb_ref, dtb_ref, alog_ref, dsk_ref, nrm_ref,
                     y_ref, st_ref, st_s, xc_s):
    seq = xbc_ref.shape[0]
    st_s[...] = jnp.zeros_like(st_s)
    row = lax.broadcasted_iota(jnp.int32, (CHUNK, CHUNK), 0)
    col = lax.broadcasted_iota(jnp.int32, (CHUNK, CHUNK), 1)
    causal = row >= col
    seg_end = jnp.where(col == CHUNK - 1, 1.0, 0.0).astype(F32)
    row8 = lax.broadcasted_iota(jnp.int32, (SUBLANES, LANES), 0)
    heads_per_group = N_SSM_HEADS // SSM_GROUPS
    group_rows = heads_per_group * SSM_HEAD_DIM

    def chunk(c, carry):
        rows = pl.ds(pl.multiple_of(c * CHUNK, CHUNK), CHUNK)
        prev_rows = pl.ds(pl.multiple_of(jnp.maximum(c * CHUNK - SUBLANES, 0), SUBLANES), SUBLANES)
        have_prev = c > 0

        def prev_tap(k, lanes, rolled):
            prev = jnp.where(have_prev, xbc_ref[prev_rows, lanes], 0.0)
            head = jnp.where(row8 < k, pltpu.roll(prev, k, 0), rolled[0:SUBLANES])
            return jnp.concatenate([head, rolled[SUBLANES:]], axis=0)

        _conv_silu(xbc_ref, rows, prev_tap, cw_ref, cb_ref, xc_s)
        dt_t, a_cum, a_cum_t, a_end, to_end_t, e_cum = _ssd_scalars(
            dt_ref[rows, :], dtb_ref, alog_ref, causal, seg_end)
        x_t = xc_s[:, 0:D_SSM].T
        y_parts = []
        for g in range(SSM_GROUPS):
            bg = xc_s[:, D_SSM + g * D_STATE:D_SSM + (g + 1) * D_STATE].astype(BF16)
            cg = xc_s[:, D_SSM + (SSM_GROUPS + g) * D_STATE:D_SSM + (SSM_GROUPS + g + 1) * D_STATE].astype(BF16)
            cb = lax.dot_general(cg, bg, _NT, preferred_element_type=F32)
            grp = slice(g * group_rows, (g + 1) * group_rows)
            st_prev = st_s[grp, :]
            y_off = lax.dot_general(cg, st_prev.astype(BF16), _NT, preferred_element_type=F32)
            for p in range(group_rows // LANES):
                blk = g * (group_rows // LANES) + p
                y_diag = _ssd_diag(xc_s, blk, cb, a_cum, a_cum_t, dt_t, causal)
                y_parts.append(y_diag + y_off[:, p * LANES:(p + 1) * LANES] * e_cum[:, blk * LANES:(blk + 1) * LANES])
            xw = []
            for h in range(g * heads_per_group, (g + 1) * heads_per_group):
                hr = slice(h * SSM_HEAD_DIM, (h + 1) * SSM_HEAD_DIM)
                xw.append((x_t[hr, :] * to_end_t[h:h + 1, :]).astype(BF16))
            new = jnp.dot(jnp.concatenate(xw, axis=0), bg, preferred_element_type=F32)
            for i, h in enumerate(range(g * heads_per_group, (g + 1) * heads_per_group)):
                hr = slice(h * SSM_HEAD_DIM, (h + 1) * SSM_HEAD_DIM)
                lr = slice(i * SSM_HEAD_DIM, (i + 1) * SSM_HEAD_DIM)
                decay = jnp.exp(jnp.broadcast_to(a_end[0:1, h:h + 1], (SSM_HEAD_DIM, D_STATE)))
                st_s[hr, :] = decay * st_prev[lr, :] + new[lr, :]
        _ssd_finish(xc_s, z_ref, rows, y_parts, dsk_ref, nrm_ref, y_ref)
        return carry

    lax.fori_loop(0, seq // CHUNK, chunk, 0)
    st_ref[...] = st_s[...]


def _ssd_prompt(xbc, z, dt_raw, n_seq, seq, consts):
    rows = xbc.shape[0]
    blk = lambda w: pl.BlockSpec((seq, w), lambda b: (b, 0))
    state_blk = pl.BlockSpec((None, D_SSM, D_STATE), lambda b: (b, 0, 0))
    return pl.pallas_call(
        _ssd_prompt_body,
        grid=(n_seq,),
        in_specs=[blk(D_XBC), blk(D_SSM), blk(LANES)] + [_resident(c.shape) for c in consts],
        out_specs=[blk(D_SSM), state_blk],
        out_shape=[jax.ShapeDtypeStruct((rows, D_SSM), BF16),
                   jax.ShapeDtypeStruct((n_seq, D_SSM, D_STATE), F32)],
        scratch_shapes=[pltpu.VMEM((D_SSM, D_STATE), F32), pltpu.VMEM((CHUNK, D_XBC), F32)],
        compiler_params=_params(1),
        name="ssd_prompt",
    )(xbc, z, dt_raw, *consts)


def _ssd_sample_body(xbc_ref, pre_ref, z_ref, dt_ref, h0_ref, cw_ref, cb_ref, dtb_ref, alog_ref, dsk_ref,
                     nrm_ref, y_ref, st_ref, xc_s, yoff_s, aend_s):
    t = CHUNK // SAMPLE_SEQS
    row = lax.broadcasted_iota(jnp.int32, (CHUNK, CHUNK), 0)
    col = lax.broadcasted_iota(jnp.int32, (CHUNK, CHUNK), 1)
    same_seq = (row // t) == (col // t)
    seg_mask = same_seq & (row >= col)
    seg_end = jnp.where(col == (row // t) * t + (t - 1), 1.0, 0.0).astype(F32)
    step = lax.broadcasted_iota(jnp.int32, (CHUNK, LANES), 0) % t
    rows = pl.ds(0, CHUNK)
    heads_per_group = N_SSM_HEADS // SSM_GROUPS
    group_rows = heads_per_group * SSM_HEAD_DIM

    def prev_tap(k, lanes, rolled):
        return jnp.where(step < k, pltpu.roll(pre_ref[:, lanes], (k - t) % CHUNK, 0), rolled)

    _conv_silu(xbc_ref, rows, prev_tap, cw_ref, cb_ref, xc_s)
    dt_t, a_cum, a_cum_t, a_end, to_end_t, e_cum = _ssd_scalars(
        dt_ref[...], dtb_ref, alog_ref, seg_mask, seg_end)
    aend_s[...] = a_end
    x_t = xc_s[:, 0:D_SSM].T
    col_seq = lax.broadcasted_iota(jnp.int32, (SSM_HEAD_DIM, CHUNK), 1) // t
    y_parts = []
    for g in range(SSM_GROUPS):
        bg = xc_s[:, D_SSM + g * D_STATE:D_SSM + (g + 1) * D_STATE].astype(BF16)
        cg_lanes = slice(D_SSM + (SSM_GROUPS + g) * D_STATE, D_SSM + (SSM_GROUPS + g + 1) * D_STATE)
        cg = xc_s[:, cg_lanes].astype(BF16)
        cb = lax.dot_general(cg, bg, _NT, preferred_element_type=F32)
        grp = slice(g * group_rows, (g + 1) * group_rows)
        xw = []
        for h in range(g * heads_per_group, (g + 1) * heads_per_group):
            hr = slice(h * SSM_HEAD_DIM, (h + 1) * SSM_HEAD_DIM)
            xw.append(x_t[hr, :] * to_end_t[h:h + 1, :])

        def per_seq(b, carry, g=g, bg=bg, cg_lanes=cg_lanes, grp=grp, xw=xw):
            seq_rows = pl.ds(pl.multiple_of(b * t, t), t)
            h_prev = h0_ref[b, grp, :]
            cg_b = xc_s[seq_rows, cg_lanes].astype(BF16)
            yoff_s[seq_rows, g * group_rows:(g + 1) * group_rows] = lax.dot_general(
                cg_b, h_prev.astype(BF16), _NT, preferred_element_type=F32)
            mine = col_seq == b
            xw_b = jnp.concatenate([jnp.where(mine, w, 0.0).astype(BF16) for w in xw], axis=0)
            new = jnp.dot(xw_b, bg, preferred_element_type=F32)
            a_last = aend_s[pl.ds(b * t, 1), :]
            for i, h in enumerate(range(g * heads_per_group, (g + 1) * heads_per_group)):
                lr = slice(i * SSM_HEAD_DIM, (i + 1) * SSM_HEAD_DIM)
                decay = jnp.exp(jnp.broadcast_to(a_last[:, h:h + 1], (SSM_HEAD_DIM, D_STATE)))
                st_ref[b, h * SSM_HEAD_DIM:(h + 1) * SSM_HEAD_DIM, :] = decay * h_prev[lr, :] + new[lr, :]
            return carry

        lax.fori_loop(0, SAMPLE_SEQS, per_seq, 0)
        for p in range(group_rows // LANES):
            blk = g * (group_rows // LANES) + p
            lanes = slice(blk * LANES, (blk + 1) * LANES)
            y_diag = _ssd_diag(xc_s, blk, cb, a_cum, a_cum_t, dt_t, seg_mask)
            y_parts.append(y_diag + yoff_s[:, lanes] * e_cum[:, lanes])
    _ssd_finish(xc_s, z_ref, rows, y_parts, dsk_ref, nrm_ref, y_ref)


def _ssd_sample(xbc, prefix_tiles, z, dt_raw, h0, consts):
    rows = xbc.shape[0]
    n_seq = h0.shape[0]
    blk = lambda w: pl.BlockSpec((CHUNK, w), lambda i: (i, 0))
    state_blk = pl.BlockSpec((SAMPLE_SEQS, D_SSM, D_STATE), lambda i: (i, 0, 0))
    return pl.pallas_call(
        _ssd_sample_body,
        grid=(rows // CHUNK,),
        in_specs=[blk(D_XBC), blk(D_XBC), blk(D_SSM), blk(LANES), state_blk]
                 + [_resident(c.shape) for c in consts],
        out_specs=[blk(D_SSM), state_blk],
        out_shape=[jax.ShapeDtypeStruct((rows, D_SSM), BF16),
                   jax.ShapeDtypeStruct((n_seq, D_SSM, D_STATE), F32)],
        scratch_shapes=[pltpu.VMEM((CHUNK, D_XBC), F32), pltpu.VMEM((CHUNK, D_SSM), F32),
                        pltpu.VMEM((CHUNK, LANES), F32)],
        compiler_params=_params(1),
        name="ssd_sample",
    )(xbc, prefix_tiles, z, dt_raw, h0, *consts)


def _out_mlp_body(x_ref, att_ref, ssm_ref, woa_ref, wos_ref, g_ref, wu_ref, wd_ref, gf_ref, o_ref):
    att = jnp.concatenate([att_ref[j] for j in range(att_ref.shape[0])], axis=1).astype(BF16)
    h = (x_ref[...] + jnp.dot(att, woa_ref[...], preferred_element_type=F32)
         + jnp.dot(ssm_ref[...], wos_ref[...], preferred_element_type=F32))
    hn = _rms(h, g_ref[...]).astype(BF16)
    mlp = None
    for c in range(D_FF // FF_TILE):
        cols = slice(c * FF_TILE, (c + 1) * FF_TILE)
        u = jnp.maximum(jnp.dot(hn, wu_ref[:, cols], preferred_element_type=F32), 0.0)
        down = jnp.dot((u * u).astype(BF16), wd_ref[cols, :], preferred_element_type=F32)
        mlp = down if mlp is None else mlp + down
    o_ref[...] = _rms(h + mlp, gf_ref[...])


def _out_mlp(x2d, att4, ssm, w_out_att, w_out_ssm, norm_mlp, w_up, w_down, norm_final):
    rows = x2d.shape[0]
    tm = min(ROW_TILE, rows)
    row_blk = lambda w: pl.BlockSpec((tm, w), lambda i: (i, 0))
    return pl.pallas_call(
        _out_mlp_body,
        grid=(rows // tm,),
        in_specs=[row_blk(D_MODEL), pl.BlockSpec((4, tm, LANES), lambda i: (0, i, 0)), row_blk(D_SSM),
                  _resident((D_ATT, D_MODEL)), _resident((D_SSM, D_MODEL)), _resident((1, D_MODEL)),
                  _resident((D_MODEL, D_FF)), _resident((D_FF, D_MODEL)), _resident((1, D_MODEL))],
        out_specs=row_blk(D_MODEL),
        out_shape=jax.ShapeDtypeStruct((rows, D_MODEL), F32),
        compiler_params=_params(1),
        name="out_mlp",
    )(x2d, att4, ssm, w_out_att, w_out_ssm, norm_mlp, w_up, w_down, norm_final)


def _q_permutation():
    perm = []
    for pp in range(N_KV_HEADS // 2):
        for g in range(Q_PER_KV):
            for e in range(2):
                base = (2 * pp + e) * Q_PER_KV * HEAD_DIM + g * HEAD_DIM
                perm += list(range(base, base + HEAD_DIM))
    return np.asarray(perm, np.int32)


def _pad_lanes(v):
    v = v.reshape(1, -1).astype(F32)
    return jnp.pad(v, ((0, 0), (0, LANES - v.shape[1])))


def kernel(x_prompt, x_sample, cache_k, cache_v, state_conv, state_ssm, w_in, w_out, conv_w, conv_b, dt_bias, a_log, d_skip, ssm_norm, norm_mix, norm_mlp, w_up, w_down, norm_final):
    depth = w_in.shape[0]
    assert depth == 1, "single-layer step"
    n_p, seq, _ = x_prompt.shape
    n_s, t_new, _ = x_sample.shape
    n_past = cache_k.shape[2]
    assert seq == MAX_WINDOW and n_past == MAX_WINDOW and t_new == SUBLANES and n_s % SAMPLE_SEQS == 0

    perm = _q_permutation()
    w = w_in[0]
    w_pad = jnp.concatenate(
        [w[:, perm], w[:, D_ATT:], jnp.zeros((D_MODEL, D_IN_PAD - w.shape[1]), F32)], axis=1).astype(BF16)
    w_out_att = w_out[0][perm, :].astype(BF16)
    w_out_ssm = w_out[0][D_ATT:, :].astype(BF16)
    w_up_b, w_down_b = w_up[0].astype(BF16), w_down[0].astype(BF16)
    g_mix, g_mlp, g_fin = (v.reshape(1, D_MODEL) for v in (norm_mix[0], norm_mlp[0], norm_final))
    ssd_consts = (conv_w[0], conv_b[0].reshape(1, D_XBC), _pad_lanes(dt_bias[0]), _pad_lanes(a_log[0]),
                  jnp.repeat(d_skip[0], SSM_HEAD_DIM).reshape(1, D_SSM), ssm_norm[0].reshape(1, D_SSM))

    xp = x_prompt.reshape(n_p * seq, D_MODEL)
    q4, k, v, z, xbc, dt_raw = _in_proj(xp, g_mix, w_pad, _rope_table(jnp.arange(seq)))
    att4 = _attn_prompt(q4, k, v, n_p, seq)
    y_ssm, st_p = _ssd_prompt(xbc, z, dt_raw, n_p, seq, ssd_consts)
    y_prompt = _out_mlp(xp, att4, y_ssm, w_out_att, w_out_ssm, g_mlp, w_up_b, w_down_b, g_fin)
    k_prompt = k.reshape(1, n_p, seq, N_KV_HEADS, HEAD_DIM)
    v_prompt = v.reshape(1, n_p, seq, N_KV_HEADS, HEAD_DIM)
    conv_prompt = xbc.reshape(1, n_p, seq, D_XBC)[:, :, seq - (CONV_WIDTH - 1):]

    xs = x_sample.reshape(n_s * t_new, D_MODEL)
    rows_tile = min(ROW_TILE, n_s * t_new)
    pos = PAST_LEN + (jnp.arange(rows_tile) % t_new)
    q4s, ks, vs, zs, xbcs, dts = _in_proj(xs, g_mix, w_pad, _rope_table(pos))
    att4s, k_sample, v_sample = _attn_sample(
        q4s, ks, vs, cache_k[0].reshape(n_s, n_past, D_KV), cache_v[0].reshape(n_s, n_past, D_KV))
    prefix_tiles = jnp.pad(state_conv[0], ((0, 0), (t_new - (CONV_WIDTH - 1), 0), (0, 0))).reshape(n_s * t_new, D_XBC)
    y_ssm_s, st_s = _ssd_sample(xbcs, prefix_tiles, zs, dts, state_ssm[0].reshape(n_s, D_SSM, D_STATE), ssd_consts)
    y_sample = _out_mlp(xs, att4s, y_ssm_s, w_out_att, w_out_ssm, g_mlp, w_up_b, w_down_b, g_fin)
    conv_sample = xbcs.reshape(1, n_s, t_new, D_XBC)[:, :, t_new - (CONV_WIDTH - 1):]

    return (y_prompt.reshape(n_p, seq, D_MODEL), y_sample.reshape(n_s, t_new, D_MODEL),
            k_prompt, v_prompt, conv_prompt,
            st_p.reshape(1, n_p, N_SSM_HEADS, SSM_HEAD_DIM, D_STATE),
            k_sample.reshape(1, n_s, n_past, N_KV_HEADS, HEAD_DIM),
            v_sample.reshape(1, n_s, n_past, N_KV_HEADS, HEAD_DIM),
            conv_sample,
            st_s.reshape(1, n_s, N_SSM_HEADS, SSM_HEAD_DIM, D_STATE))
```

```python
import functools

import numpy as np
import jax
import jax.numpy as jnp
from jax import lax
from jax.experimental import pallas as pl
from jax.experimental.pallas import tpu as pltpu

F32 = jnp.float32
BF16 = jnp.bfloat16

LANES = 128
SUBLANES = 8

D_MODEL = 1024
HEAD_DIM = 64
N_KV_HEADS = 4
Q_PER_KV = 2
D_ATT = 512
D_KV = 256
ROT_HALF = 8
ROPE_THETA = 500000.0
DILATIONS = (1, 4, 16)
N_KEYS = 128
MAX_WINDOW = 2048
ATT_BLK = 128
ATT_UNITS = 2
PAST_LEN = 8192
ATT_SCALE = HEAD_DIM ** -0.5
NEG_BIG = -1e30
N_SSM_HEADS = 8
SSM_HEAD_DIM = 64
D_SSM = 512
SSM_GROUPS = 2
D_STATE = 128
CONV_WIDTH = 4
CHUNK = 128
D_XBC = 1024
D_FF = 4096
RMS_EPS = 1e-5
D_IN_MAIN = D_ATT + 2 * D_KV + D_SSM + D_XBC
D_IN_PAD = D_IN_MAIN + LANES

VMEM_LIMIT = 56 * 1024 * 1024
ROW_TILE = 512
FF_TILE = 512
SAMPLE_SEQS = CHUNK // 8

_NT = (((1,), (1,)), ((), ()))


def _params(n_axes):
    return pltpu.CompilerParams(dimension_semantics=("arbitrary",) * n_axes,
                                vmem_limit_bytes=VMEM_LIMIT)


def _resident(shape):
    return pl.BlockSpec(shape, lambda *_: (0,) * len(shape), pipeline_mode=pl.Buffered(1))


def _rms(x, g):
    return x * lax.rsqrt(jnp.mean(x * x, axis=-1, keepdims=True) + RMS_EPS) * g


def _silu(x):
    return x * (1.0 / (1.0 + jnp.exp(-x)))


def _softplus(x):
    return jnp.maximum(x, 0.0) + jnp.log1p(jnp.exp(-jnp.abs(x)))


def _in_proj_body(x_ref, g_ref, w_ref, rope_ref, q_ref, k_ref, v_ref, z_ref, xbc_ref, dt_ref, *kv_t_refs):
    xb = _rms(x_ref[...], g_ref[...]).astype(BF16)
    tm = xb.shape[0]
    first_half = (lax.broadcasted_iota(jnp.int32, (tm, LANES), 1) % HEAD_DIM) < ROT_HALF

    def proj(lo, hi):
        return jnp.dot(xb, w_ref[:, lo:hi], preferred_element_type=F32)

    def rope(u, cos, sin):
        partner = jnp.where(first_half, pltpu.roll(u, LANES - ROT_HALF, 1), pltpu.roll(u, ROT_HALF, 1))
        return u * cos + partner * sin

    cq, sq = rope_ref[:, 0:128], rope_ref[:, 128:256]
    ck, sk = rope_ref[:, 256:384], rope_ref[:, 384:512]
    q = proj(0, D_ATT)
    for j in range(D_ATT // LANES):
        q_ref[j] = rope(q[:, j * LANES:(j + 1) * LANES], cq, sq)
    k = proj(D_ATT, D_ATT + D_KV)
    v = proj(D_ATT + D_KV, D_ATT + 2 * D_KV)
    v_ref[...] = v
    for j in range(D_KV // LANES):
        lanes = slice(j * LANES, (j + 1) * LANES)
        kj = rope(k[:, lanes], ck, sk)
        k_ref[:, lanes] = kj
        if kv_t_refs:
            kv_t_refs[0][lanes, :] = kj.T
            kv_t_refs[1][lanes, :] = v[:, lanes].T
    z_ref[...] = proj(D_ATT + 2 * D_KV, D_ATT + 2 * D_KV + D_SSM)
    xbc_ref[...] = proj(D_ATT + 2 * D_KV + D_SSM, D_IN_MAIN)
    dt_ref[...] = proj(D_IN_MAIN, D_IN_PAD)


def _in_proj(x2d, norm_mix, w_pad, rope_tab, seq=None):
    rows = x2d.shape[0]
    tm = min(ROW_TILE, rows)
    n_tab = rope_tab.shape[0] // tm
    row_blk = lambda w: pl.BlockSpec((tm, w), lambda i: (i, 0))
    out_specs = [pl.BlockSpec((4, tm, LANES), lambda i: (0, i, 0)),
                 row_blk(D_KV), row_blk(D_KV), row_blk(D_SSM), row_blk(D_XBC), row_blk(LANES)]
    out_shape = [jax.ShapeDtypeStruct((4, rows, LANES), F32),
                 jax.ShapeDtypeStruct((rows, D_KV), F32), jax.ShapeDtypeStruct((rows, D_KV), F32),
                 jax.ShapeDtypeStruct((rows, D_SSM), F32), jax.ShapeDtypeStruct((rows, D_XBC), F32),
                 jax.ShapeDtypeStruct((rows, LANES), F32)]
    if seq is not None:
        per_seq = seq // tm
        t_blk = pl.BlockSpec((None, D_KV, tm), lambda i: (i // per_seq, 0, i % per_seq))
        out_specs += [t_blk, t_blk]
        out_shape += [jax.ShapeDtypeStruct((rows // seq, D_KV, seq), F32)] * 2
    return pl.pallas_call(
        _in_proj_body,
        grid=(rows // tm,),
        in_specs=[row_blk(D_MODEL), _resident((1, D_MODEL)), _resident((D_MODEL, D_IN_PAD)),
                  pl.BlockSpec((tm, 4 * LANES), lambda i: (i % n_tab, 0))],
        out_specs=out_specs,
        out_shape=out_shape,
        compiler_params=_params(1),
        name="in_proj",
    )(x2d, norm_mix, w_pad, rope_tab)


def _rope_table(pos):
    inv = ROPE_THETA ** (-jnp.arange(0, 2 * ROT_HALF, 2, dtype=F32) / (2 * ROT_HALF))
    ang = pos.astype(F32)[:, None] * inv[None, :]
    cos, sin = jnp.cos(ang), jnp.sin(ang)
    n = pos.shape[0]
    ones = jnp.ones((n, HEAD_DIM - 2 * ROT_HALF), F32)
    cos_h = jnp.concatenate([cos, cos, ones], axis=1)
    sin_h = jnp.concatenate([-sin, sin, 0.0 * ones], axis=1)
    cos_l, sin_l = jnp.tile(cos_h, (1, 2)), jnp.tile(sin_h, (1, 2))
    return jnp.concatenate([cos_l * ATT_SCALE, sin_l * ATT_SCALE, cos_l, sin_l], axis=1)


def _attn_prompt_body(q_ref, k_ref, v_ref, band_ref, caus_ref, o_ref, m_s, l_s, acc_s):
    seq = k_ref.shape[0]
    e0 = lax.broadcasted_iota(jnp.int32, (ATT_BLK, LANES), 1) < HEAD_DIM

    zero = jnp.zeros((ATT_BLK, LANES), F32)

    def scores(q_rows, k_rows, bias):
        parts = []
        for g in range(Q_PER_KV):
            qg = q_ref.at[g][q_rows, :]
            parts += [jnp.where(e0, qg, zero), jnp.where(e0, zero, qg)]
        lhs = jnp.concatenate(parts, axis=0).astype(BF16)
        kb = k_ref[k_rows, :].astype(BF16)
        vb = v_ref[k_rows, :].astype(BF16)
        nk = kb.shape[0]
        s = lax.dot_general(lhs, kb, _NT, preferred_element_type=F32)
        s = (s.reshape(4, ATT_BLK, nk) + bias[None]).reshape(4 * ATT_BLK, nk)
        m = jnp.max(s, axis=1, keepdims=True)
        p = jnp.exp(s - m)
        l = jnp.sum(p, axis=1, keepdims=True)
        pv = jnp.dot(p.astype(BF16), vb, preferred_element_type=F32)
        mb = jnp.broadcast_to(m, pv.shape)
        lb = jnp.broadcast_to(l, pv.shape)
        out = []
        for g in range(Q_PER_KV):
            r0, r1, r2 = 2 * g * ATT_BLK, (2 * g + 1) * ATT_BLK, (2 * g + 2) * ATT_BLK
            out.append((jnp.where(e0, mb[r0:r1], mb[r1:r2]), jnp.where(e0, lb[r0:r1], lb[r1:r2]),
                        jnp.where(e0, pv[r0:r1], pv[r1:r2])))
        return out

    def process(units, first_group):
        state = [(m_s.at[g], l_s.at[g], acc_s.at[g]) for g in range(Q_PER_KV)]
        old = None
        if not first_group:
            old = [[tuple(ref[q_rows, :] for ref in state[g]) for g in range(Q_PER_KV)] for q_rows, _, _ in units]
        new = [scores(*unit) for unit in units]
        for i, (q_rows, _, _) in enumerate(units):
            for g in range(Q_PER_KV):
                mg, lg, acc = new[i][g]
                if not first_group:
                    m_old, l_old, acc_old = old[i][g]
                    m_new = jnp.maximum(m_old, mg)
                    a_old = jnp.exp(m_old - m_new)
                    a_new = jnp.exp(mg - m_new)
                    mg, lg, acc = m_new, a_old * l_old + a_new * lg, a_old * acc_old + a_new * acc
                m_g, l_g, acc_g = state[g]
                m_g[q_rows, :] = mg
                l_g[q_rows, :] = lg
                acc_g[q_rows, :] = acc

    for gi, d in enumerate(DILATIONS):
        blocks = seq // d // ATT_BLK

        def unit(u, d=d, blocks=blocks):
            r, n = u // blocks, u % blocks

            def rows(start, size):
                if d == 1:
                    return pl.ds(pl.multiple_of(start, ATT_BLK), size)
                return pl.ds(r + start * d, size, stride=d)

            q_rows = rows(n * ATT_BLK, ATT_BLK)
            if blocks == 1:
                return q_rows, q_rows, caus_ref[...]
            return q_rows, rows(jnp.maximum(n - 1, 0) * ATT_BLK, 2 * ATT_BLK), band_ref[jnp.minimum(n, 1)]

        def step(i, carry, unit=unit, first_group=(gi == 0)):
            process([unit(ATT_UNITS * i + j) for j in range(ATT_UNITS)], first_group)
            return carry

        lax.fori_loop(0, d * blocks // ATT_UNITS, step, 0)

    def finish(i, carry):
        rows = pl.ds(pl.multiple_of(i * 2 * ATT_BLK, 2 * ATT_BLK), 2 * ATT_BLK)
        for g in range(Q_PER_KV):
            o_ref.at[g][rows, :] = (acc_s.at[g][rows, :] / l_s.at[g][rows, :]).astype(o_ref.dtype)
        return carry

    lax.fori_loop(0, seq // (2 * ATT_BLK), finish, 0)


def _window_bias():
    i = np.arange(ATT_BLK)[:, None]
    j = np.arange(2 * ATT_BLK)[None, :]
    dist = i + ATT_BLK - j
    inner = np.where((dist >= 0) & (dist <= N_KEYS), 0.0, NEG_BIG).astype(np.float32)
    caus = inner[:, ATT_BLK:]
    first = np.concatenate([caus, np.full_like(caus, NEG_BIG)], axis=1)
    return jnp.asarray(np.stack([first, inner])), jnp.asarray(caus)


def _attn_prompt(q4, k, v, n_seq, seq):
    band, caus = _window_bias()
    return pl.pallas_call(
        _attn_prompt_body,
        grid=(n_seq, 2),
        in_specs=[pl.BlockSpec((2, seq, LANES), lambda b, pp: (pp, b, 0)),
                  pl.BlockSpec((seq, LANES), lambda b, pp: (b, pp)),
                  pl.BlockSpec((seq, LANES), lambda b, pp: (b, pp)),
                  _resident(band.shape), _resident(caus.shape)],
        out_specs=pl.BlockSpec((2, seq, LANES), lambda b, pp: (pp, b, 0)),
        out_shape=jax.ShapeDtypeStruct(q4.shape, BF16),
        scratch_shapes=[pltpu.VMEM((2, seq, LANES), F32)] * 3,
        compiler_params=_params(2),
        name="attn_prompt",
    )(q4, k, v, band, caus)


def _attn_sample_body(q_ref, kn_ref, vn_ref, ck_ref, cv_ref, mult_ref, att_ref, ko_ref, vo_ref):
    t = kn_ref.shape[0]
    n_past = ck_ref.shape[2]
    tail = slice(n_past - LANES, n_past)
    is_new = lax.broadcasted_iota(jnp.int32, (D_KV, LANES), 1) >= LANES - t
    row_pad = jnp.zeros((LANES - t, D_KV), F32)

    def shift(c_ref, n_ref, o_ref):
        moved = pltpu.roll(c_ref[0], n_past - t, 1)
        o_ref[0] = moved
        new_t = jnp.concatenate([n_ref[...], row_pad], axis=0).T
        o_ref[0, :, tail] = jnp.where(is_new, pltpu.roll(new_t, LANES - t, 1), moved[:, tail])

    shift(ck_ref, kn_ref, ko_ref)
    shift(cv_ref, vn_ref, vo_ref)

    mult = mult_ref[...]
    valid = mult > 0.0
    e0 = lax.broadcasted_iota(jnp.int32, (t, LANES), 1) < HEAD_DIM
    zero = jnp.zeros((t, LANES), F32)
    pad = jnp.zeros((LANES - t, LANES), F32)
    for pp in range(N_KV_HEADS // 2):
        lanes = slice(pp * LANES, (pp + 1) * LANES)
        parts = []
        for g in range(Q_PER_KV):
            qg = q_ref[pp * Q_PER_KV + g]
            parts += [jnp.where(e0, qg, zero), jnp.where(e0, zero, qg)]
        lhs = jnp.concatenate(parts, axis=0).astype(BF16)
        kc = ck_ref[0, lanes, :].astype(BF16)
        vc = cv_ref[0, lanes, :].astype(BF16)
        kn = jnp.concatenate([kn_ref[:, lanes], pad], axis=0).astype(BF16)
        vn = jnp.concatenate([vn_ref[:, lanes], pad], axis=0).astype(BF16)
        s = jnp.concatenate([jnp.dot(lhs, kc, preferred_element_type=F32),
                             lax.dot_general(lhs, kn, _NT, preferred_element_type=F32)], axis=1)
        s = jnp.where(valid, s, NEG_BIG)
        m = jnp.max(s, axis=1, keepdims=True)
        p = jnp.exp(s - m) * mult
        l = jnp.sum(p, axis=1, keepdims=True)
        pb = p.astype(BF16)
        pv = (lax.dot_general(pb[:, :n_past], vc, _NT, preferred_element_type=F32)
              + jnp.dot(pb[:, n_past:], vn, preferred_element_type=F32))
        o = pv / l
        for g in range(Q_PER_KV):
            r0 = 2 * g * t
            att_ref[pp * Q_PER_KV + g] = jnp.where(e0, o[r0:r0 + t], o[r0 + t:r0 + 2 * t])


def _key_multiplicity(t, n_past):
    idx = np.concatenate([np.arange(n_past + t), np.full(LANES - t, 10 ** 9)])[None, :]
    dist = n_past + np.arange(t)[:, None] - idx
    mult = np.zeros(dist.shape, np.float32)
    for d in DILATIONS:
        mult += (dist >= 0) & (dist % d == 0) & (dist <= N_KEYS * d)
    return jnp.asarray(np.tile(mult, (4, 1)))


def _attn_sample(q4, k_new, v_new, cache_k, cache_v):
    n_seq, _, n_past = cache_k.shape
    t = k_new.shape[0] // n_seq
    mult = _key_multiplicity(t, n_past)
    cache_blk = pl.BlockSpec((1, D_KV, n_past), lambda b: (b, 0, 0))
    new_blk = pl.BlockSpec((t, D_KV), lambda b: (b, 0))
    q_blk = pl.BlockSpec((4, t, LANES), lambda b: (0, b, 0))
    return pl.pallas_call(
        _attn_sample_body,
        grid=(n_seq,),
        in_specs=[q_blk, new_blk, new_blk, cache_blk, cache_blk, _resident(mult.shape)],
        out_specs=[q_blk, cache_blk, cache_blk],
        out_shape=[jax.ShapeDtypeStruct(q4.shape, F32),
                   jax.ShapeDtypeStruct(cache_k.shape, F32), jax.ShapeDtypeStruct(cache_v.shape, F32)],
        compiler_params=_params(1),
        name="attn_sample",
    )(q4, k_new, v_new, cache_k, cache_v, mult)


def _conv_silu(cur_ref, rows, prev_tap, cw_ref, cb_ref, xc_s):
    for j in range(D_XBC // LANES):
        lanes = slice(j * LANES, (j + 1) * LANES)
        cur = cur_ref[rows, lanes]
        y = cb_ref[:, lanes]
        for tap in range(CONV_WIDTH):
            k = CONV_WIDTH - 1 - tap
            shifted = cur if k == 0 else prev_tap(k, lanes, pltpu.roll(cur, k, 0))
            y = y + shifted * cw_ref[tap:tap + 1, lanes]
        xc_s[:, lanes] = _silu(y)


def _ssd_scalars(dt_raw, dtb_ref, alog_ref, seg_mask, seg_end):
    dt = _softplus(dt_raw + dtb_ref[...])
    a = dt * (-jnp.exp(alog_ref[...]))
    tril = jnp.where(seg_mask, 1.0, 0.0).astype(F32)
    a_cum = jnp.dot(tril, a, precision=lax.Precision.HIGHEST, preferred_element_type=F32)
    a_end = jnp.dot(seg_end, a_cum, precision=lax.Precision.HIGHEST, preferred_element_type=F32)
    to_end = jnp.exp(a_end - a_cum) * dt
    head = lax.broadcasted_iota(jnp.int32, (LANES, D_SSM), 0)
    lane = lax.broadcasted_iota(jnp.int32, (LANES, D_SSM), 1)
    expand = jnp.where(lane // SSM_HEAD_DIM == head, 1.0, 0.0).astype(F32)
    e_cum = jnp.dot(jnp.exp(a_cum), expand, precision=lax.Precision.HIGHEST, preferred_element_type=F32)
    return dt.T, a_cum, a_cum.T, a_end, to_end.T, e_cum


def _ssd_diag(xc_s, blk, cb, a_cum, a_cum_t, dt_t, seg_mask):
    e0 = lax.broadcasted_iota(jnp.int32, (CHUNK, LANES), 1) < SSM_HEAD_DIM
    x_pair = xc_s[:, blk * LANES:(blk + 1) * LANES].astype(BF16)
    out = []
    for e in range(2):
        h = 2 * blk + e
        seg = jnp.broadcast_to(a_cum[:, h:h + 1], (CHUNK, CHUNK)) - jnp.broadcast_to(a_cum_t[h:h + 1, :], (CHUNK, CHUNK))
        w = cb * jnp.where(seg_mask, jnp.exp(seg), 0.0) * dt_t[h:h + 1, :]
        out.append(jnp.dot(w.astype(BF16), x_pair, preferred_element_type=F32))
    return jnp.where(e0, out[0], out[1])


def _ssd_finish(xc_s, z_ref, rows, y_parts, dsk_ref, nrm_ref, y_ref):
    blocks_per_group = D_SSM // SSM_GROUPS // LANES
    for g in range(SSM_GROUPS):
        gated = []
        for p in range(blocks_per_group):
            blk = g * blocks_per_group + p
            lanes = slice(blk * LANES, (blk + 1) * LANES)
            y = y_parts[blk] + dsk_ref[:, lanes] * xc_s[:, lanes]
            gated.append(y * _silu(z_ref[rows, lanes]))
        ss = sum(jnp.sum(y * y, axis=-1, keepdims=True) for y in gated)
        inv = lax.rsqrt(ss / (D_SSM // SSM_GROUPS) + RMS_EPS)
        for p in range(blocks_per_group):
            blk = g * blocks_per_group + p
            lanes = slice(blk * LANES, (blk + 1) * LANES)
            y_ref[rows, lanes] = (gated[p] * inv * nrm_ref[:, lanes]).astype(y_ref.dtype)


def _ssd_prompt_body(xbc_ref, z_ref, dt_ref, cw_ref, cb_ref, dtb_ref, alog_ref, dsk_ref, nrm_ref,
                     y_ref, st_ref, st_s, xc_s):
    seq = xbc_ref.shape[0]
    st_s[...] = jnp.zeros_like(st_s)
    row = lax.broadcasted_iota(jnp.int32, (CHUNK, CHUNK), 0)
    col = lax.broadcasted_iota(jnp.int32, (CHUNK, CHUNK), 1)
    causal = row >= col
    seg_end = jnp.where(col == CHUNK - 1, 1.0, 0.0).astype(F32)
    row8 = lax.broadcasted_iota(jnp.int32, (SUBLANES, LANES), 0)
    heads_per_group = N_SSM_HEADS // SSM_GROUPS
    group_rows = heads_per_group * SSM_HEAD_DIM

    def chunk(c, carry):
        rows = pl.ds(pl.multiple_of(c * CHUNK, CHUNK), CHUNK)
        prev_rows = pl.ds(pl.multiple_of(jnp.maximum(c * CHUNK - SUBLANES, 0), SUBLANES), SUBLANES)
        have_prev = c > 0

        def prev_tap(k, lanes, rolled):
            prev = jnp.where(have_prev, xbc_ref[prev_rows, lanes], 0.0)
            head = jnp.where(row8 < k, pltpu.roll(prev, k, 0), rolled[0:SUBLANES])
            return jnp.concatenate([head, rolled[SUBLANES:]], axis=0)

        _conv_silu(xbc_ref, rows, prev_tap, cw_ref, cb_ref, xc_s)
        dt_t, a_cum, a_cum_t, a_end, to_end_t, e_cum = _ssd_scalars(
            dt_ref[rows, :], dtb_ref, alog_ref, causal, seg_end)
        x_t = xc_s[:, 0:D_SSM].T
        y_parts = []
        for g in range(SSM_GROUPS):
            bg = xc_s[:, D_SSM + g * D_STATE:D_SSM + (g + 1) * D_STATE].astype(BF16)
            cg = xc_s[:, D_SSM + (SSM_GROUPS + g) * D_STATE:D_SSM + (SSM_GROUPS + g + 1) * D_STATE].astype(BF16)
            cb = lax.dot_general(cg, bg, _NT, preferred_element_type=F32)
            grp = slice(g * group_rows, (g + 1) * group_rows)
            st_prev = st_s[grp, :]
            y_off = lax.dot_general(cg, st_prev.astype(BF16), _NT, preferred_element_type=F32)
            for p in range(group_rows // LANES):
                blk = g * (group_rows // LANES) + p
                y_diag = _ssd_diag(xc_s, blk, cb, a_cum, a_cum_t, dt_t, causal)
                y_parts.append(y_diag + y_off[:, p * LANES:(p + 1) * LANES] * e_cum[:, blk * LANES:(blk + 1) * LANES])
            xw = []
            for h in range(g * heads_per_group, (g + 1) * heads_per_group):
                hr = slice(h * SSM_HEAD_DIM, (h + 1) * SSM_HEAD_DIM)
                xw.append((x_t[hr, :] * to_end_t[h:h + 1, :]).astype(BF16))
            new = jnp.dot(jnp.concatenate(xw, axis=0), bg, preferred_element_type=F32)
            for i, h in enumerate(range(g * heads_per_group, (g + 1) * heads_per_group)):
                hr = slice(h * SSM_HEAD_DIM, (h + 1) * SSM_HEAD_DIM)
                lr = slice(i * SSM_HEAD_DIM, (i + 1) * SSM_HEAD_DIM)
                decay = jnp.exp(jnp.broadcast_to(a_end[0:1, h:h + 1], (SSM_HEAD_DIM, D_STATE)))
                st_s[hr, :] = decay * st_prev[lr, :] + new[lr, :]
        _ssd_finish(xc_s, z_ref, rows, y_parts, dsk_ref, nrm_ref, y_ref)
        return carry

    lax.fori_loop(0, seq // CHUNK, chunk, 0)
    st_ref[...] = st_s[...]


def _ssd_prompt(xbc, z, dt_raw, n_seq, seq, consts):
    rows = xbc.shape[0]
    blk = lambda w: pl.BlockSpec((seq, w), lambda b: (b, 0))
    state_blk = pl.BlockSpec((None, D_SSM, D_STATE), lambda b: (b, 0, 0))
    return pl.pallas_call(
        _ssd_prompt_body,
        grid=(n_seq,),
        in_specs=[blk(D_XBC), blk(D_SSM), blk(LANES)] + [_resident(c.shape) for c in consts],
        out_specs=[blk(D_SSM), state_blk],
        out_shape=[jax.ShapeDtypeStruct((rows, D_SSM), BF16),
                   jax.ShapeDtypeStruct((n_seq, D_SSM, D_STATE), F32)],
        scratch_shapes=[pltpu.VMEM((D_SSM, D_STATE), F32), pltpu.VMEM((CHUNK, D_XBC), F32)],
        compiler_params=_params(1),
        name="ssd_prompt",
    )(xbc, z, dt_raw, *consts)


def _ssd_sample_body(xbc_ref, pre_ref, z_ref, dt_ref, h0_ref, cw_ref, cb_ref, dtb_ref, alog_ref, dsk_ref,
                     nrm_ref, y_ref, st_ref, xc_s, yoff_s, aend_s):
    t = CHUNK // SAMPLE_SEQS
    row = lax.broadcasted_iota(jnp.int32, (CHUNK, CHUNK), 0)
    col = lax.broadcasted_iota(jnp.int32, (CHUNK, CHUNK), 1)
    same_seq = (row // t) == (col // t)
    seg_mask = same_seq & (row >= col)
    seg_end = jnp.where(col == (row // t) * t + (t - 1), 1.0, 0.0).astype(F32)
    step = lax.broadcasted_iota(jnp.int32, (CHUNK, LANES), 0) % t
    rows = pl.ds(0, CHUNK)
    heads_per_group = N_SSM_HEADS // SSM_GROUPS
    group_rows = heads_per_group * SSM_HEAD_DIM

    def prev_tap(k, lanes, rolled):
        return jnp.where(step < k, pltpu.roll(pre_ref[:, lanes], (k - t) % CHUNK, 0), rolled)

    _conv_silu(xbc_ref, rows, prev_tap, cw_ref, cb_ref, xc_s)
    dt_t, a_cum, a_cum_t, a_end, to_end_t, e_cum = _ssd_scalars(
        dt_ref[...], dtb_ref, alog_ref, seg_mask, seg_end)
    aend_s[...] = a_end
    x_t = xc_s[:, 0:D_SSM].T
    col_seq = lax.broadcasted_iota(jnp.int32, (SSM_HEAD_DIM, CHUNK), 1) // t
    y_parts = []
    for g in range(SSM_GROUPS):
        bg = xc_s[:, D_SSM + g * D_STATE:D_SSM + (g + 1) * D_STATE].astype(BF16)
        cg_lanes = slice(D_SSM + (SSM_GROUPS + g) * D_STATE, D_SSM + (SSM_GROUPS + g + 1) * D_STATE)
        cg = xc_s[:, cg_lanes].astype(BF16)
        cb = lax.dot_general(cg, bg, _NT, preferred_element_type=F32)
        grp = slice(g * group_rows, (g + 1) * group_rows)
        xw = []
        for h in range(g * heads_per_group, (g + 1) * heads_per_group):
            hr = slice(h * SSM_HEAD_DIM, (h + 1) * SSM_HEAD_DIM)
            xw.append(x_t[hr, :] * to_end_t[h:h + 1, :])

        def per_seq(b, carry, g=g, bg=bg, cg_lanes=cg_lanes, grp=grp, xw=xw):
            seq_rows = pl.ds(pl.multiple_of(b * t, t), t)
            h_prev = h0_ref[b, grp, :]
            cg_b = xc_s[seq_rows, cg_lanes].astype(BF16)
            yoff_s[seq_rows, g * group_rows:(g + 1) * group_rows] = lax.dot_general(
                cg_b, h_prev.astype(BF16), _NT, preferred_element_type=F32)
            mine = col_seq == b
            xw_b = jnp.concatenate([jnp.where(mine, w, 0.0).astype(BF16) for w in xw], axis=0)
            new = jnp.dot(xw_b, bg, preferred_element_type=F32)
            a_last = aend_s[pl.ds(b * t, 1), :]
            for i, h in enumerate(range(g * heads_per_group, (g + 1) * heads_per_group)):
                lr = slice(i * SSM_HEAD_DIM, (i + 1) * SSM_HEAD_DIM)
                decay = jnp.exp(jnp.broadcast_to(a_last[:, h:h + 1], (SSM_HEAD_DIM, D_STATE)))
                st_ref[b, h * SSM_HEAD_DIM:(h + 1) * SSM_HEAD_DIM, :] = decay * h_prev[lr, :] + new[lr, :]
            return carry

        lax.fori_loop(0, SAMPLE_SEQS, per_seq, 0)
        for p in range(group_rows // LANES):
            blk = g * (group_rows // LANES) + p
            lanes = slice(blk * LANES, (blk + 1) * LANES)
            y_diag = _ssd_diag(xc_s, blk, cb, a_cum, a_cum_t, dt_t, seg_mask)
            y_parts.append(y_diag + yoff_s[:, lanes] * e_cum[:, lanes])
    _ssd_finish(xc_s, z_ref, rows, y_parts, dsk_ref, nrm_ref, y_ref)


def _ssd_sample(xbc, prefix_tiles, z, dt_raw, h0, consts):
    rows = xbc.shape[0]
    n_seq = h0.shape[0]
    blk = lambda w: pl.BlockSpec((CHUNK, w), lambda i: (i, 0))
    state_blk = pl.BlockSpec((SAMPLE_SEQS, D_SSM, D_STATE), lambda i: (i, 0, 0))
    return pl.pallas_call(
        _ssd_sample_body,
        grid=(rows // CHUNK,),
        in_specs=[blk(D_XBC), blk(D_XBC), blk(D_SSM), blk(LANES), state_blk]
                 + [_resident(c.shape) for c in consts],
        out_specs=[blk(D_SSM), state_blk],
        out_shape=[jax.ShapeDtypeStruct((rows, D_SSM), BF16),
                   jax.ShapeDtypeStruct((n_seq, D_SSM, D_STATE), F32)],
        scratch_shapes=[pltpu.VMEM((CHUNK, D_XBC), F32), pltpu.VMEM((CHUNK, D_SSM), F32),
                        pltpu.VMEM((CHUNK, LANES), F32)],
        compiler_params=_params(1),
        name="ssd_sample",
    )(xbc, prefix_tiles, z, dt_raw, h0, *consts)


def _out_mlp_body(x_ref, att_ref, ssm_ref, woa_ref, wos_ref, g_ref, wu_ref, wd_ref, gf_ref, o_ref):
    att = jnp.concatenate([att_ref[j] for j in range(att_ref.shape[0])], axis=1).astype(BF16)
    h = (x_ref[...] + jnp.dot(att, woa_ref[...], preferred_element_type=F32)
         + jnp.dot(ssm_ref[...], wos_ref[...], preferred_element_type=F32))
    hn = _rms(h, g_ref[...]).astype(BF16)
    mlp = None
    for c in range(D_FF // FF_TILE):
        cols = slice(c * FF_TILE, (c + 1) * FF_TILE)
        u = jnp.maximum(jnp.dot(hn, wu_ref[:, cols], preferred_element_type=F32), 0.0)
        down = jnp.dot((u * u).astype(BF16), wd_ref[cols, :], preferred_element_type=F32)
        mlp = down if mlp is None else mlp + down
    o_ref[...] = _rms(h + mlp, gf_ref[...])


def _out_mlp(x2d, att4, ssm, w_out_att, w_out_ssm, norm_mlp, w_up, w_down, norm_final):
    rows = x2d.shape[0]
    tm = min(ROW_TILE, rows)
    row_blk = lambda w: pl.BlockSpec((tm, w), lambda i: (i, 0))
    return pl.pallas_call(
        _out_mlp_body,
        grid=(rows // tm,),
        in_specs=[row_blk(D_MODEL), pl.BlockSpec((4, tm, LANES), lambda i: (0, i, 0)), row_blk(D_SSM),
                  _resident((D_ATT, D_MODEL)), _resident((D_SSM, D_MODEL)), _resident((1, D_MODEL)),
                  _resident((D_MODEL, D_FF)), _resident((D_FF, D_MODEL)), _resident((1, D_MODEL))],
        out_specs=row_blk(D_MODEL),
        out_shape=jax.ShapeDtypeStruct((rows, D_MODEL), F32),
        compiler_params=_params(1),
        name="out_mlp",
    )(x2d, att4, ssm, w_out_att, w_out_ssm, norm_mlp, w_up, w_down, norm_final)


def _q_permutation():
    perm = []
    for pp in range(N_KV_HEADS // 2):
        for g in range(Q_PER_KV):
            for e in range(2):
                base = (2 * pp + e) * Q_PER_KV * HEAD_DIM + g * HEAD_DIM
                perm += list(range(base, base + HEAD_DIM))
    return np.asarray(perm, np.int32)


def _to_lanes(buf):
    n, pos = buf.shape[:2]
    return jnp.transpose(buf, (0, 2, 3, 1)).reshape(n, D_KV, pos)


def _from_lanes(buf_t):
    n, _, pos = buf_t.shape
    return jnp.transpose(buf_t.reshape(n, N_KV_HEADS, HEAD_DIM, pos), (0, 3, 1, 2))[None]


def _pad_lanes(v):
    v = v.reshape(1, -1).astype(F32)
    return jnp.pad(v, ((0, 0), (0, LANES - v.shape[1])))


def kernel(x_prompt, x_sample, cache_k, cache_v, state_conv, state_ssm, w_in, w_out, conv_w, conv_b, dt_bias, a_log, d_skip, ssm_norm, norm_mix, norm_mlp, w_up, w_down, norm_final):
    depth = w_in.shape[0]
    assert depth == 1, "single-layer step"
    n_p, seq, _ = x_prompt.shape
    n_s, t_new, _ = x_sample.shape
    n_past = cache_k.shape[2]
    assert seq == MAX_WINDOW and n_past == MAX_WINDOW and t_new == SUBLANES and n_s % SAMPLE_SEQS == 0

    perm = _q_permutation()
    w = w_in[0]
    w_pad = jnp.concatenate(
        [w[:, perm], w[:, D_ATT:], jnp.zeros((D_MODEL, D_IN_PAD - w.shape[1]), F32)], axis=1).astype(BF16)
    w_out_att = w_out[0][perm, :].astype(BF16)
    w_out_ssm = w_out[0][D_ATT:, :].astype(BF16)
    w_up_b, w_down_b = w_up[0].astype(BF16), w_down[0].astype(BF16)
    g_mix, g_mlp, g_fin = (v.reshape(1, D_MODEL) for v in (norm_mix[0], norm_mlp[0], norm_final))
    ssd_consts = (conv_w[0], conv_b[0].reshape(1, D_XBC), _pad_lanes(dt_bias[0]), _pad_lanes(a_log[0]),
                  jnp.repeat(d_skip[0], SSM_HEAD_DIM).reshape(1, D_SSM), ssm_norm[0].reshape(1, D_SSM))

    xp = x_prompt.reshape(n_p * seq, D_MODEL)
    q4, k, v, z, xbc, dt_raw, k_t, v_t = _in_proj(xp, g_mix, w_pad, _rope_table(jnp.arange(seq)), seq=seq)
    att4 = _attn_prompt(q4, k, v, n_p, seq)
    y_ssm, st_p = _ssd_prompt(xbc, z, dt_raw, n_p, seq, ssd_consts)
    y_prompt = _out_mlp(xp, att4, y_ssm, w_out_att, w_out_ssm, g_mlp, w_up_b, w_down_b, g_fin)
    k_prompt, v_prompt = _from_lanes(k_t), _from_lanes(v_t)
    conv_prompt = xbc.reshape(1, n_p, seq, D_XBC)[:, :, seq - (CONV_WIDTH - 1):]

    xs = x_sample.reshape(n_s * t_new, D_MODEL)
    rows_tile = min(ROW_TILE, n_s * t_new)
    pos = PAST_LEN + (jnp.arange(rows_tile) % t_new)
    q4s, ks, vs, zs, xbcs, dts = _in_proj(xs, g_mix, w_pad, _rope_table(pos))
    att4s, k_sample, v_sample = _attn_sample(q4s, ks, vs, _to_lanes(cache_k[0]), _to_lanes(cache_v[0]))
    prefix_tiles = jnp.pad(state_conv[0], ((0, 0), (t_new - (CONV_WIDTH - 1), 0), (0, 0))).reshape(n_s * t_new, D_XBC)
    y_ssm_s, st_s = _ssd_sample(xbcs, prefix_tiles, zs, dts, state_ssm[0].reshape(n_s, D_SSM, D_STATE), ssd_consts)
    y_sample = _out_mlp(xs, att4s, y_ssm_s, w_out_att, w_out_ssm, g_mlp, w_up_b, w_down_b, g_fin)
    conv_sample = xbcs.reshape(1, n_s, t_new, D_XBC)[:, :, t_new - (CONV_WIDTH - 1):]

    return (y_prompt.reshape(n_p, seq, D_MODEL), y_sample.reshape(n_s, t_new, D_MODEL),
            k_prompt, v_prompt, conv_prompt,
            st_p.reshape(1, n_p, N_SSM_HEADS, SSM_HEAD_DIM, D_STATE),
            _from_lanes(k_sample), _from_lanes(v_sample), conv_sample,
            st_s.reshape(1, n_s, N_SSM_HEADS, SSM_HEAD_DIM, D_STATE))
```

```python
import functools

import numpy as np
import jax
import jax.numpy as jnp
from jax import lax
from jax.experimental import pallas as pl
from jax.experimental.pallas import tpu as pltpu

F32 = jnp.float32
BF16 = jnp.bfloat16

LANES = 128
SUBLANES = 8

D_MODEL = 1024
HEAD_DIM = 64
N_KV_HEADS = 4
Q_PER_KV = 2
D_ATT = 512
D_KV = 256
ROT_HALF = 8
ROPE_THETA = 500000.0
DILATIONS = (1, 4, 16)
N_KEYS = 128
MAX_WINDOW = 2048
ATT_BLK = 128
ATT_UNITS = 4
RESIDUES = 16
PAST_LEN = 8192
ATT_SCALE = HEAD_DIM ** -0.5
NEG_BIG = -1e30
N_SSM_HEADS = 8
SSM_HEAD_DIM = 64
D_SSM = 512
SSM_GROUPS = 2
D_STATE = 128
CONV_WIDTH = 4
CHUNK = 128
D_XBC = 1024
D_FF = 4096
RMS_EPS = 1e-5
D_IN_MAIN = D_ATT + 2 * D_KV + D_SSM + D_XBC
D_IN_PAD = D_IN_MAIN + LANES

VMEM_LIMIT = 56 * 1024 * 1024
ROW_TILE = 512
FF_TILE = 512
SSD_CHUNKS_PER_STEP = 2
SAMPLE_SEQS = CHUNK // 8

_NT = (((1,), (1,)), ((), ()))


def _params(n_axes):
    return pltpu.CompilerParams(dimension_semantics=("arbitrary",) * n_axes,
                                vmem_limit_bytes=VMEM_LIMIT)


def _resident(shape):
    return pl.BlockSpec(shape, lambda *_: (0,) * len(shape), pipeline_mode=pl.Buffered(1))


def _rms(x, g):
    return x * lax.rsqrt(jnp.mean(x * x, axis=-1, keepdims=True) + RMS_EPS) * g


def _silu(x):
    half = 0.5 * x
    return half + half * jnp.tanh(half)


def _softplus(x):
    return jnp.maximum(x, 0.0) + jnp.log1p(jnp.exp(-jnp.abs(x)))


def _in_proj_body(x_ref, g_ref, w_ref, rope_ref, q_ref, k_ref, v_ref, z_ref, xbc_ref, dt_ref, *kv_t_refs):
    xb = _rms(x_ref[...], g_ref[...]).astype(BF16)
    tm = xb.shape[0]
    first_half = (lax.broadcasted_iota(jnp.int32, (tm, LANES), 1) % HEAD_DIM) < ROT_HALF

    def proj(lo, hi):
        return jnp.dot(xb, w_ref[:, lo:hi], preferred_element_type=F32)

    def rope(u, cos, sin):
        partner = jnp.where(first_half, pltpu.roll(u, LANES - ROT_HALF, 1), pltpu.roll(u, ROT_HALF, 1))
        return u * cos + partner * sin

    cq, sq = rope_ref[:, 0:128], rope_ref[:, 128:256]
    ck, sk = rope_ref[:, 256:384], rope_ref[:, 384:512]
    q = proj(0, D_ATT)
    for j in range(D_ATT // LANES):
        q_ref[j] = rope(q[:, j * LANES:(j + 1) * LANES], cq, sq)
    k = proj(D_ATT, D_ATT + D_KV)
    v = proj(D_ATT + D_KV, D_ATT + 2 * D_KV)
    v_ref[...] = v
    for j in range(D_KV // LANES):
        lanes = slice(j * LANES, (j + 1) * LANES)
        kj = rope(k[:, lanes], ck, sk)
        k_ref[:, lanes] = kj
        if kv_t_refs:
            kv_t_refs[0][lanes, :] = kj.T
            kv_t_refs[1][lanes, :] = v[:, lanes].T
    z_ref[...] = proj(D_ATT + 2 * D_KV, D_ATT + 2 * D_KV + D_SSM)
    xbc_ref[...] = proj(D_ATT + 2 * D_KV + D_SSM, D_IN_MAIN)
    dt_ref[...] = proj(D_IN_MAIN, D_IN_PAD)


def _in_proj(x2d, norm_mix, w_pad, rope_tab, seq=None):
    rows = x2d.shape[0]
    tm = min(ROW_TILE, rows)
    n_tab = rope_tab.shape[0] // tm
    row_blk = lambda w: pl.BlockSpec((tm, w), lambda i: (i, 0))
    out_specs = [pl.BlockSpec((4, tm, LANES), lambda i: (0, i, 0)),
                 row_blk(D_KV), row_blk(D_KV), row_blk(D_SSM), row_blk(D_XBC), row_blk(LANES)]
    out_shape = [jax.ShapeDtypeStruct((4, rows, LANES), F32),
                 jax.ShapeDtypeStruct((rows, D_KV), F32), jax.ShapeDtypeStruct((rows, D_KV), F32),
                 jax.ShapeDtypeStruct((rows, D_SSM), F32), jax.ShapeDtypeStruct((rows, D_XBC), F32),
                 jax.ShapeDtypeStruct((rows, LANES), F32)]
    if seq is not None:
        per_seq = seq // tm
        t_blk = pl.BlockSpec((None, D_KV, tm), lambda i: (i // per_seq, 0, i % per_seq))
        out_specs += [t_blk, t_blk]
        out_shape += [jax.ShapeDtypeStruct((rows // seq, D_KV, seq), F32)] * 2
    return pl.pallas_call(
        _in_proj_body,
        grid=(rows // tm,),
        in_specs=[row_blk(D_MODEL), _resident((1, D_MODEL)), _resident((D_MODEL, D_IN_PAD)),
                  pl.BlockSpec((tm, 4 * LANES), lambda i: (i % n_tab, 0))],
        out_specs=out_specs,
        out_shape=out_shape,
        compiler_params=_params(1),
        name="in_proj",
    )(x2d, norm_mix, w_pad, rope_tab)


def _rope_table(pos):
    inv = ROPE_THETA ** (-jnp.arange(0, 2 * ROT_HALF, 2, dtype=F32) / (2 * ROT_HALF))
    ang = pos.astype(F32)[:, None] * inv[None, :]
    cos, sin = jnp.cos(ang), jnp.sin(ang)
    n = pos.shape[0]
    ones = jnp.ones((n, HEAD_DIM - 2 * ROT_HALF), F32)
    cos_h = jnp.concatenate([cos, cos, ones], axis=1)
    sin_h = jnp.concatenate([-sin, sin, 0.0 * ones], axis=1)
    cos_l, sin_l = jnp.tile(cos_h, (1, 2)), jnp.tile(sin_h, (1, 2))
    return jnp.concatenate([cos_l * ATT_SCALE, sin_l * ATT_SCALE, cos_l, sin_l], axis=1)


def _attn_prompt_body(q_ref, k_ref, v_ref, bias1_ref, bias4_ref, caus_ref, o_ref,
                      q3, k3, v3, m3, l3, acc3, nat_s):
    seq = k_ref.shape[0]
    per_res = seq // RESIDUES
    e0 = lax.broadcasted_iota(jnp.int32, (ATT_BLK, LANES), 1) < HEAD_DIM
    zero = jnp.zeros((ATT_BLK, LANES), F32)

    def gather(r, carry):
        rows = pl.ds(r, per_res, stride=RESIDUES)
        for g in range(Q_PER_KV):
            q3[g, r] = q_ref.at[g][rows, :]
        k3[r] = k_ref[rows, :]
        v3[r] = v_ref[rows, :]
        return carry

    lax.fori_loop(0, RESIDUES, gather, 0)

    def scores(q_idx, k_idx, bias):
        parts = []
        for g in range(Q_PER_KV):
            qg = q3.at[g][q_idx].reshape(ATT_BLK, LANES)
            parts += [jnp.where(e0, qg, zero), jnp.where(e0, zero, qg)]
        lhs = jnp.concatenate(parts, axis=0).astype(BF16)
        kb = k3[k_idx]
        nk = kb.shape[0] * kb.shape[1]
        kb = kb.reshape(nk, LANES).astype(BF16)
        vb = v3[k_idx].reshape(nk, LANES).astype(BF16)
        s = lax.dot_general(lhs, kb, _NT, preferred_element_type=F32)
        s = (s.reshape(4, ATT_BLK, nk) + bias[None]).reshape(4 * ATT_BLK, nk)
        m = jnp.max(s, axis=1, keepdims=True)
        p = jnp.exp(s - m)
        l = jnp.sum(p, axis=1, keepdims=True)
        pv = jnp.dot(p.astype(BF16), vb, preferred_element_type=F32)
        mb = jnp.broadcast_to(m, pv.shape)
        lb = jnp.broadcast_to(l, pv.shape)
        out = []
        for g in range(Q_PER_KV):
            r0, r1, r2 = 2 * g * ATT_BLK, (2 * g + 1) * ATT_BLK, (2 * g + 2) * ATT_BLK
            out.append((jnp.where(e0, mb[r0:r1], mb[r1:r2]), jnp.where(e0, lb[r0:r1], lb[r1:r2]),
                        jnp.where(e0, pv[r0:r1], pv[r1:r2])))
        return out

    def process(units, first_group):
        state = [(m3.at[g], l3.at[g], acc3.at[g]) for g in range(Q_PER_KV)]
        old = None
        if not first_group:
            old = [[tuple(ref[q_idx].reshape(ATT_BLK, LANES) for ref in state[g]) for g in range(Q_PER_KV)]
                   for q_idx, _, _ in units]
        new = [scores(*unit) for unit in units]
        for i, (q_idx, _, _) in enumerate(units):
            for g in range(Q_PER_KV):
                mg, lg, acc = new[i][g]
                if not first_group:
                    m_old, l_old, acc_old = old[i][g]
                    m_new = jnp.maximum(m_old, mg)
                    a_old = jnp.exp(m_old - m_new)
                    a_new = jnp.exp(mg - m_new)
                    mg, lg, acc = m_new, a_old * l_old + a_new * lg, a_old * acc_old + a_new * acc
                for ref, val in zip(state[g], (mg, lg, acc)):
                    ref[q_idx] = val.reshape(ref[q_idx].shape)

    for gi, (d, bias_ref) in enumerate(zip(DILATIONS, (bias1_ref, bias4_ref, caus_ref))):
        blocks = seq // d // ATT_BLK
        lanes16 = RESIDUES // d
        depth = ATT_BLK // lanes16

        def unit(u, d=d, blocks=blocks, lanes16=lanes16, depth=depth, bias_ref=bias_ref):
            r, n = u // blocks, u % blocks
            lead = pl.ds(r, lanes16, stride=d)
            q_idx = (lead, pl.ds(pl.multiple_of(n * depth, depth), depth))
            if blocks == 1:
                return q_idx, q_idx, bias_ref[...]
            k_idx = (lead, pl.ds(pl.multiple_of(jnp.maximum(n - 1, 0) * depth, depth), 2 * depth))
            return q_idx, k_idx, bias_ref[jnp.minimum(n, 1)]

        def step(i, carry, unit=unit, first_group=(gi == 0)):
            process([unit(ATT_UNITS * i + j) for j in range(ATT_UNITS)], first_group)
            return carry

        lax.fori_loop(0, d * blocks // ATT_UNITS, step, 0)

    def scatter(r, carry):
        rows = pl.ds(r, per_res, stride=RESIDUES)
        for g in range(Q_PER_KV):
            nat_s.at[g][rows, :] = acc3[g, r] / l3[g, r]
        return carry

    lax.fori_loop(0, RESIDUES, scatter, 0)
    for g in range(Q_PER_KV):
        o_ref[g] = nat_s[g].astype(o_ref.dtype)


def _window_bias(d):
    lanes16 = RESIDUES // d

    def order(n):
        a, ll = np.divmod(np.arange(n), n // lanes16)
        return lanes16 * ll + a

    i = order(ATT_BLK)[:, None]
    if d == RESIDUES:
        return jnp.asarray(np.where(order(ATT_BLK)[None, :] <= i, 0.0, NEG_BIG).astype(np.float32))
    j = order(2 * ATT_BLK)[None, :]
    dist = i + ATT_BLK - j
    inner = np.where((dist >= 0) & (dist <= N_KEYS), 0.0, NEG_BIG)
    first = np.where(j <= i, 0.0, NEG_BIG)
    return jnp.asarray(np.stack([first, inner]).astype(np.float32))


def _attn_prompt(q4, k, v, n_seq, seq):
    biases = [_window_bias(d) for d in DILATIONS]
    slab = (RESIDUES, seq // RESIDUES, LANES)
    return pl.pallas_call(
        _attn_prompt_body,
        grid=(n_seq, 2),
        in_specs=[pl.BlockSpec((2, seq, LANES), lambda b, pp: (pp, b, 0)),
                  pl.BlockSpec((seq, LANES), lambda b, pp: (b, pp)),
                  pl.BlockSpec((seq, LANES), lambda b, pp: (b, pp))] + [_resident(x.shape) for x in biases],
        out_specs=pl.BlockSpec((2, seq, LANES), lambda b, pp: (pp, b, 0)),
        out_shape=jax.ShapeDtypeStruct(q4.shape, BF16),
        scratch_shapes=[pltpu.VMEM((2,) + slab, F32), pltpu.VMEM(slab, F32), pltpu.VMEM(slab, F32)]
                       + [pltpu.VMEM((2,) + slab, F32)] * 3 + [pltpu.VMEM((2, seq, LANES), F32)],
        compiler_params=_params(2),
        name="attn_prompt",
    )(q4, k, v, *biases)


def _attn_sample_body(q_ref, kn_ref, vn_ref, ck_ref, cv_ref, mult_ref, att_ref, ko_ref, vo_ref):
    t = kn_ref.shape[0]
    n_past = ck_ref.shape[2]
    tail = slice(n_past - LANES, n_past)
    is_new = lax.broadcasted_iota(jnp.int32, (D_KV, LANES), 1) >= LANES - t
    row_pad = jnp.zeros((LANES - t, D_KV), F32)

    def shift(c_ref, n_ref, o_ref):
        moved = pltpu.roll(c_ref[0], n_past - t, 1)
        o_ref[0] = moved
        new_t = jnp.concatenate([n_ref[...], row_pad], axis=0).T
        o_ref[0, :, tail] = jnp.where(is_new, pltpu.roll(new_t, LANES - t, 1), moved[:, tail])

    shift(ck_ref, kn_ref, ko_ref)
    shift(cv_ref, vn_ref, vo_ref)

    mult = mult_ref[...]
    valid = mult > 0.0
    e0 = lax.broadcasted_iota(jnp.int32, (t, LANES), 1) < HEAD_DIM
    zero = jnp.zeros((t, LANES), F32)
    pad = jnp.zeros((LANES - t, LANES), F32)
    for pp in range(N_KV_HEADS // 2):
        lanes = slice(pp * LANES, (pp + 1) * LANES)
        parts = []
        for g in range(Q_PER_KV):
            qg = q_ref[pp * Q_PER_KV + g]
            parts += [jnp.where(e0, qg, zero), jnp.where(e0, zero, qg)]
        lhs = jnp.concatenate(parts, axis=0).astype(BF16)
        kc = ck_ref[0, lanes, :].astype(BF16)
        vc = cv_ref[0, lanes, :].astype(BF16)
        kn = jnp.concatenate([kn_ref[:, lanes], pad], axis=0).astype(BF16)
        vn = jnp.concatenate([vn_ref[:, lanes], pad], axis=0).astype(BF16)
        s = jnp.concatenate([jnp.dot(lhs, kc, preferred_element_type=F32),
                             lax.dot_general(lhs, kn, _NT, preferred_element_type=F32)], axis=1)
        s = jnp.where(valid, s, NEG_BIG)
        m = jnp.max(s, axis=1, keepdims=True)
        p = jnp.exp(s - m) * mult
        l = jnp.sum(p, axis=1, keepdims=True)
        pb = p.astype(BF16)
        pv = (lax.dot_general(pb[:, :n_past], vc, _NT, preferred_element_type=F32)
              + jnp.dot(pb[:, n_past:], vn, preferred_element_type=F32))
        o = pv / l
        for g in range(Q_PER_KV):
            r0 = 2 * g * t
            att_ref[pp * Q_PER_KV + g] = jnp.where(e0, o[r0:r0 + t], o[r0 + t:r0 + 2 * t])


def _key_multiplicity(t, n_past):
    idx = np.concatenate([np.arange(n_past + t), np.full(LANES - t, 10 ** 9)])[None, :]
    dist = n_past + np.arange(t)[:, None] - idx
    mult = np.zeros(dist.shape, np.float32)
    for d in DILATIONS:
        mult += (dist >= 0) & (dist % d == 0) & (dist <= N_KEYS * d)
    return jnp.asarray(np.tile(mult, (4, 1)))


def _attn_sample(q4, k_new, v_new, cache_k, cache_v):
    n_seq, _, n_past = cache_k.shape
    t = k_new.shape[0] // n_seq
    mult = _key_multiplicity(t, n_past)
    cache_blk = pl.BlockSpec((1, D_KV, n_past), lambda b: (b, 0, 0))
    new_blk = pl.BlockSpec((t, D_KV), lambda b: (b, 0))
    q_blk = pl.BlockSpec((4, t, LANES), lambda b: (0, b, 0))
    return pl.pallas_call(
        _attn_sample_body,
        grid=(n_seq,),
        in_specs=[q_blk, new_blk, new_blk, cache_blk, cache_blk, _resident(mult.shape)],
        out_specs=[q_blk, cache_blk, cache_blk],
        out_shape=[jax.ShapeDtypeStruct(q4.shape, F32),
                   jax.ShapeDtypeStruct(cache_k.shape, F32), jax.ShapeDtypeStruct(cache_v.shape, F32)],
        compiler_params=_params(1),
        name="attn_sample",
    )(q4, k_new, v_new, cache_k, cache_v, mult)


def _conv_silu(cur_ref, rows, prev_tap, cw_ref, cb_ref, xc_s):
    for j in range(D_XBC // LANES):
        lanes = slice(j * LANES, (j + 1) * LANES)
        cur = cur_ref[rows, lanes]
        y = cb_ref[:, lanes]
        for tap in range(CONV_WIDTH):
            k = CONV_WIDTH - 1 - tap
            shifted = cur if k == 0 else prev_tap(k, lanes, pltpu.roll(cur, k, 0))
            y = y + shifted * cw_ref[tap:tap + 1, lanes]
        xc_s[:, lanes] = _silu(y)


def _select_sum(sel, x):
    total = None
    for _ in range(3):
        piece = x.astype(BF16)
        x = x - piece.astype(F32)
        part = jnp.dot(sel, piece, preferred_element_type=F32)
        total = part if total is None else total + part
    return total


def _ssd_scalars(dt_raw, dtb_ref, alog_ref, seg_mask, seg_end=None):
    dt = _softplus(dt_raw + dtb_ref[...])
    a = dt * (-jnp.exp(alog_ref[...]))
    a_cum = _select_sum(jnp.where(seg_mask, 1.0, 0.0).astype(BF16), a)
    if seg_end is None:
        a_end = jnp.broadcast_to(a_cum[CHUNK - 1:CHUNK, :], a_cum.shape)
    else:
        a_end = _select_sum(seg_end, a_cum)
    to_end = jnp.exp(a_end - a_cum) * dt
    return dt.T, a_cum, a_cum.T, a_end, to_end.T


def _ssd_diag(xc_s, blk, cb, a_cum, a_cum_t, dt_t, seg_mask):
    e0 = lax.broadcasted_iota(jnp.int32, (CHUNK, LANES), 1) < SSM_HEAD_DIM
    x_pair = xc_s[:, blk * LANES:(blk + 1) * LANES].astype(BF16)
    out, grow = [], []
    for e in range(2):
        h = 2 * blk + e
        a_col = jnp.broadcast_to(a_cum[:, h:h + 1], (CHUNK, CHUNK))
        seg = a_col - jnp.broadcast_to(a_cum_t[h:h + 1, :], (CHUNK, CHUNK))
        w = cb * jnp.where(seg_mask, jnp.exp(seg), 0.0) * dt_t[h:h + 1, :]
        out.append(jnp.dot(w.astype(BF16), x_pair, preferred_element_type=F32))
        grow.append(jnp.exp(a_col))
    return jnp.where(e0, out[0], out[1]), jnp.where(e0, grow[0], grow[1])


def _ssd_finish(xc_s, z_ref, rows, y_parts, dsk_ref, nrm_ref, y_ref):
    blocks_per_group = D_SSM // SSM_GROUPS // LANES
    for g in range(SSM_GROUPS):
        gated = []
        for p in range(blocks_per_group):
            blk = g * blocks_per_group + p
            lanes = slice(blk * LANES, (blk + 1) * LANES)
            y = y_parts[blk] + dsk_ref[:, lanes] * xc_s[:, lanes]
            gated.append(y * _silu(z_ref[rows, lanes]))
        ss = sum(jnp.sum(y * y, axis=-1, keepdims=True) for y in gated)
        inv = lax.rsqrt(ss / (D_SSM // SSM_GROUPS) + RMS_EPS)
        for p in range(blocks_per_group):
            blk = g * blocks_per_group + p
            lanes = slice(blk * LANES, (blk + 1) * LANES)
            y_ref[rows, lanes] = (gated[p] * inv * nrm_ref[:, lanes]).astype(y_ref.dtype)


def _ssd_prompt_body(xbc_ref, z_ref, dt_ref, cw_ref, cb_ref, dtb_ref, alog_ref, dsk_ref, nrm_ref,
                     y_ref, st_ref, st_s, xc_bufs):
    seq = xbc_ref.shape[0]
    st_s[...] = jnp.zeros_like(st_s)
    row = lax.broadcasted_iota(jnp.int32, (CHUNK, CHUNK), 0)
    col = lax.broadcasted_iota(jnp.int32, (CHUNK, CHUNK), 1)
    causal = row >= col
    row8 = lax.broadcasted_iota(jnp.int32, (SUBLANES, LANES), 0)
    heads_per_group = N_SSM_HEADS // SSM_GROUPS
    group_rows = heads_per_group * SSM_HEAD_DIM

    def chunk(c, xc_s):
        rows = pl.ds(pl.multiple_of(c * CHUNK, CHUNK), CHUNK)
        prev_rows = pl.ds(pl.multiple_of(jnp.maximum(c * CHUNK - SUBLANES, 0), SUBLANES), SUBLANES)
        have_prev = c > 0

        def prev_tap(k, lanes, rolled):
            prev = jnp.where(have_prev, xbc_ref[prev_rows, lanes], 0.0)
            head = jnp.where(row8 < k, pltpu.roll(prev, k, 0), rolled[0:SUBLANES])
            return jnp.concatenate([head, rolled[SUBLANES:]], axis=0)

        _conv_silu(xbc_ref, rows, prev_tap, cw_ref, cb_ref, xc_s)
        dt_t, a_cum, a_cum_t, a_end, to_end_t = _ssd_scalars(dt_ref[rows, :], dtb_ref, alog_ref, causal)
        x_t = xc_s[:, 0:D_SSM].T
        y_parts = []
        for g in range(SSM_GROUPS):
            bg = xc_s[:, D_SSM + g * D_STATE:D_SSM + (g + 1) * D_STATE].astype(BF16)
            cg = xc_s[:, D_SSM + (SSM_GROUPS + g) * D_STATE:D_SSM + (SSM_GROUPS + g + 1) * D_STATE].astype(BF16)
            cb = lax.dot_general(cg, bg, _NT, preferred_element_type=F32)
            grp = slice(g * group_rows, (g + 1) * group_rows)
            st_prev = st_s[grp, :]
            y_off = lax.dot_general(cg, st_prev.astype(BF16), _NT, preferred_element_type=F32)
            for p in range(group_rows // LANES):
                blk = g * (group_rows // LANES) + p
                y_diag, grow = _ssd_diag(xc_s, blk, cb, a_cum, a_cum_t, dt_t, causal)
                y_parts.append(y_diag + y_off[:, p * LANES:(p + 1) * LANES] * grow)
            xw = []
            for h in range(g * heads_per_group, (g + 1) * heads_per_group):
                hr = slice(h * SSM_HEAD_DIM, (h + 1) * SSM_HEAD_DIM)
                xw.append((x_t[hr, :] * to_end_t[h:h + 1, :]).astype(BF16))
            new = jnp.dot(jnp.concatenate(xw, axis=0), bg, preferred_element_type=F32)
            for i, h in enumerate(range(g * heads_per_group, (g + 1) * heads_per_group)):
                hr = slice(h * SSM_HEAD_DIM, (h + 1) * SSM_HEAD_DIM)
                lr = slice(i * SSM_HEAD_DIM, (i + 1) * SSM_HEAD_DIM)
                decay = jnp.exp(jnp.broadcast_to(a_end[0:1, h:h + 1], (SSM_HEAD_DIM, D_STATE)))
                st_s[hr, :] = decay * st_prev[lr, :] + new[lr, :]
        _ssd_finish(xc_s, z_ref, rows, y_parts, dsk_ref, nrm_ref, y_ref)

    def step(i, carry):
        for j in range(SSD_CHUNKS_PER_STEP):
            chunk(SSD_CHUNKS_PER_STEP * i + j, xc_bufs.at[j])
        return carry

    lax.fori_loop(0, seq // CHUNK // SSD_CHUNKS_PER_STEP, step, 0)
    st_ref[...] = st_s[...]


def _ssd_prompt(xbc, z, dt_raw, n_seq, seq, consts):
    rows = xbc.shape[0]
    blk = lambda w: pl.BlockSpec((seq, w), lambda b: (b, 0))
    state_blk = pl.BlockSpec((None, D_SSM, D_STATE), lambda b: (b, 0, 0))
    return pl.pallas_call(
        _ssd_prompt_body,
        grid=(n_seq,),
        in_specs=[blk(D_XBC), blk(D_SSM), blk(LANES)] + [_resident(c.shape) for c in consts],
        out_specs=[blk(D_SSM), state_blk],
        out_shape=[jax.ShapeDtypeStruct((rows, D_SSM), BF16),
                   jax.ShapeDtypeStruct((n_seq, D_SSM, D_STATE), F32)],
        scratch_shapes=[pltpu.VMEM((D_SSM, D_STATE), F32),
                        pltpu.VMEM((SSD_CHUNKS_PER_STEP, CHUNK, D_XBC), F32)],
        compiler_params=_params(1),
        name="ssd_prompt",
    )(xbc, z, dt_raw, *consts)


def _ssd_sample_body(xbc_ref, pre_ref, z_ref, dt_ref, h0_ref, cw_ref, cb_ref, dtb_ref, alog_ref, dsk_ref,
                     nrm_ref, y_ref, st_ref, xc_s, yoff_s, aend_s):
    t = CHUNK // SAMPLE_SEQS
    row = lax.broadcasted_iota(jnp.int32, (CHUNK, CHUNK), 0)
    col = lax.broadcasted_iota(jnp.int32, (CHUNK, CHUNK), 1)
    same_seq = (row // t) == (col // t)
    seg_mask = same_seq & (row >= col)
    seg_end = jnp.where(col == (row // t) * t + (t - 1), 1.0, 0.0).astype(BF16)
    step = lax.broadcasted_iota(jnp.int32, (CHUNK, LANES), 0) % t
    rows = pl.ds(0, CHUNK)
    heads_per_group = N_SSM_HEADS // SSM_GROUPS
    group_rows = heads_per_group * SSM_HEAD_DIM

    def prev_tap(k, lanes, rolled):
        return jnp.where(step < k, pltpu.roll(pre_ref[:, lanes], (k - t) % CHUNK, 0), rolled)

    _conv_silu(xbc_ref, rows, prev_tap, cw_ref, cb_ref, xc_s)
    dt_t, a_cum, a_cum_t, a_end, to_end_t = _ssd_scalars(
        dt_ref[...], dtb_ref, alog_ref, seg_mask, seg_end)
    aend_s[...] = a_end
    x_t = xc_s[:, 0:D_SSM].T
    col_seq = lax.broadcasted_iota(jnp.int32, (SSM_HEAD_DIM, CHUNK), 1) // t
    y_parts = []
    for g in range(SSM_GROUPS):
        bg = xc_s[:, D_SSM + g * D_STATE:D_SSM + (g + 1) * D_STATE].astype(BF16)
        cg_lanes = slice(D_SSM + (SSM_GROUPS + g) * D_STATE, D_SSM + (SSM_GROUPS + g + 1) * D_STATE)
        cg = xc_s[:, cg_lanes].astype(BF16)
        cb = lax.dot_general(cg, bg, _NT, preferred_element_type=F32)
        grp = slice(g * group_rows, (g + 1) * group_rows)
        xw = []
        for h in range(g * heads_per_group, (g + 1) * heads_per_group):
            hr = slice(h * SSM_HEAD_DIM, (h + 1) * SSM_HEAD_DIM)
            xw.append(x_t[hr, :] * to_end_t[h:h + 1, :])

        def per_seq(b, carry, g=g, bg=bg, cg_lanes=cg_lanes, grp=grp, xw=xw):
            seq_rows = pl.ds(pl.multiple_of(b * t, t), t)
            h_prev = h0_ref[b, grp, :]
            cg_b = xc_s[seq_rows, cg_lanes].astype(BF16)
            yoff_s[seq_rows, g * group_rows:(g + 1) * group_rows] = lax.dot_general(
                cg_b, h_prev.astype(BF16), _NT, preferred_element_type=F32)
            mine = col_seq == b
            xw_b = jnp.concatenate([jnp.where(mine, w, 0.0).astype(BF16) for w in xw], axis=0)
            new = jnp.dot(xw_b, bg, preferred_element_type=F32)
            a_last = aend_s[pl.ds(b * t, 1), :]
            for i, h in enumerate(range(g * heads_per_group, (g + 1) * heads_per_group)):
                lr = slice(i * SSM_HEAD_DIM, (i + 1) * SSM_HEAD_DIM)
                decay = jnp.exp(jnp.broadcast_to(a_last[:, h:h + 1], (SSM_HEAD_DIM, D_STATE)))
                st_ref[b, h * SSM_HEAD_DIM:(h + 1) * SSM_HEAD_DIM, :] = decay * h_prev[lr, :] + new[lr, :]
            return carry

        lax.fori_loop(0, SAMPLE_SEQS, per_seq, 0)
        for p in range(group_rows // LANES):
            blk = g * (group_rows // LANES) + p
            lanes = slice(blk * LANES, (blk + 1) * LANES)
            y_diag, grow = _ssd_diag(xc_s, blk, cb, a_cum, a_cum_t, dt_t, seg_mask)
            y_parts.append(y_diag + yoff_s[:, lanes] * grow)
    _ssd_finish(xc_s, z_ref, rows, y_parts, dsk_ref, nrm_ref, y_ref)


def _ssd_sample(xbc, prefix_tiles, z, dt_raw, h0, consts):
    rows = xbc.shape[0]
    n_seq = h0.shape[0]
    blk = lambda w: pl.BlockSpec((CHUNK, w), lambda i: (i, 0))
    state_blk = pl.BlockSpec((SAMPLE_SEQS, D_SSM, D_STATE), lambda i: (i, 0, 0))
    return pl.pallas_call(
        _ssd_sample_body,
        grid=(rows // CHUNK,),
        in_specs=[blk(D_XBC), blk(D_XBC), blk(D_SSM), blk(LANES), state_blk]
                 + [_resident(c.shape) for c in consts],
        out_specs=[blk(D_SSM), state_blk],
        out_shape=[jax.ShapeDtypeStruct((rows, D_SSM), BF16),
                   jax.ShapeDtypeStruct((n_seq, D_SSM, D_STATE), F32)],
        scratch_shapes=[pltpu.VMEM((CHUNK, D_XBC), F32), pltpu.VMEM((CHUNK, D_SSM), F32),
                        pltpu.VMEM((CHUNK, LANES), F32)],
        compiler_params=_params(1),
        name="ssd_sample",
    )(xbc, prefix_tiles, z, dt_raw, h0, *consts)


def _out_mlp_body(x_ref, att_ref, ssm_ref, woa_ref, wos_ref, g_ref, wu_ref, wd_ref, gf_ref, o_ref):
    att = jnp.concatenate([att_ref[j] for j in range(att_ref.shape[0])], axis=1).astype(BF16)
    h = (x_ref[...] + jnp.dot(att, woa_ref[...], preferred_element_type=F32)
         + jnp.dot(ssm_ref[...], wos_ref[...], preferred_element_type=F32))
    hn = _rms(h, g_ref[...]).astype(BF16)
    mlp = None
    for c in range(D_FF // FF_TILE):
        cols = slice(c * FF_TILE, (c + 1) * FF_TILE)
        u = jnp.maximum(jnp.dot(hn, wu_ref[:, cols], preferred_element_type=F32), 0.0)
        down = jnp.dot((u * u).astype(BF16), wd_ref[cols, :], preferred_element_type=F32)
        mlp = down if mlp is None else mlp + down
    o_ref[...] = _rms(h + mlp, gf_ref[...])


def _out_mlp(x2d, att4, ssm, w_out_att, w_out_ssm, norm_mlp, w_up, w_down, norm_final):
    rows = x2d.shape[0]
    tm = min(ROW_TILE, rows)
    row_blk = lambda w: pl.BlockSpec((tm, w), lambda i: (i, 0))
    return pl.pallas_call(
        _out_mlp_body,
        grid=(rows // tm,),
        in_specs=[row_blk(D_MODEL), pl.BlockSpec((4, tm, LANES), lambda i: (0, i, 0)), row_blk(D_SSM),
                  _resident((D_ATT, D_MODEL)), _resident((D_SSM, D_MODEL)), _resident((1, D_MODEL)),
                  _resident((D_MODEL, D_FF)), _resident((D_FF, D_MODEL)), _resident((1, D_MODEL))],
        out_specs=row_blk(D_MODEL),
        out_shape=jax.ShapeDtypeStruct((rows, D_MODEL), F32),
        compiler_params=_params(1),
        name="out_mlp",
    )(x2d, att4, ssm, w_out_att, w_out_ssm, norm_mlp, w_up, w_down, norm_final)


def _q_permutation():
    perm = []
    for pp in range(N_KV_HEADS // 2):
        for g in range(Q_PER_KV):
            for e in range(2):
                base = (2 * pp + e) * Q_PER_KV * HEAD_DIM + g * HEAD_DIM
                perm += list(range(base, base + HEAD_DIM))
    return np.asarray(perm, np.int32)


def _to_lanes(buf):
    n, pos = buf.shape[:2]
    return jnp.transpose(buf, (0, 2, 3, 1)).reshape(n, D_KV, pos)


def _from_lanes(buf_t):
    n, _, pos = buf_t.shape
    return jnp.transpose(buf_t.reshape(n, N_KV_HEADS, HEAD_DIM, pos), (0, 3, 1, 2))[None]


def _pad_lanes(v):
    v = v.reshape(1, -1).astype(F32)
    return jnp.pad(v, ((0, 0), (0, LANES - v.shape[1])))


def kernel(x_prompt, x_sample, cache_k, cache_v, state_conv, state_ssm, w_in, w_out, conv_w, conv_b, dt_bias, a_log, d_skip, ssm_norm, norm_mix, norm_mlp, w_up, w_down, norm_final):
    depth = w_in.shape[0]
    assert depth == 1, "single-layer step"
    n_p, seq, _ = x_prompt.shape
    n_s, t_new, _ = x_sample.shape
    n_past = cache_k.shape[2]
    assert seq == MAX_WINDOW and n_past == MAX_WINDOW and t_new == SUBLANES and n_s % SAMPLE_SEQS == 0

    perm = _q_permutation()
    w = w_in[0]
    w_pad = jnp.concatenate(
        [w[:, perm], w[:, D_ATT:], jnp.zeros((D_MODEL, D_IN_PAD - w.shape[1]), F32)], axis=1).astype(BF16)
    w_out_att = w_out[0][perm, :].astype(BF16)
    w_out_ssm = w_out[0][D_ATT:, :].astype(BF16)
    w_up_b, w_down_b = w_up[0].astype(BF16), w_down[0].astype(BF16)
    g_mix, g_mlp, g_fin = (v.reshape(1, D_MODEL) for v in (norm_mix[0], norm_mlp[0], norm_final))
    ssd_consts = (conv_w[0], conv_b[0].reshape(1, D_XBC), _pad_lanes(dt_bias[0]), _pad_lanes(a_log[0]),
                  jnp.repeat(d_skip[0], SSM_HEAD_DIM).reshape(1, D_SSM), ssm_norm[0].reshape(1, D_SSM))

    xp = x_prompt.reshape(n_p * seq, D_MODEL)
    q4, k, v, z, xbc, dt_raw, k_t, v_t = _in_proj(xp, g_mix, w_pad, _rope_table(jnp.arange(seq)), seq=seq)
    att4 = _attn_prompt(q4, k, v, n_p, seq)
    y_ssm, st_p = _ssd_prompt(xbc, z, dt_raw, n_p, seq, ssd_consts)
    y_prompt = _out_mlp(xp, att4, y_ssm, w_out_att, w_out_ssm, g_mlp, w_up_b, w_down_b, g_fin)
    k_prompt, v_prompt = _from_lanes(k_t), _from_lanes(v_t)
    conv_prompt = xbc.reshape(1, n_p, seq, D_XBC)[:, :, seq - (CONV_WIDTH - 1):]

    xs = x_sample.reshape(n_s * t_new, D_MODEL)
    rows_tile = min(ROW_TILE, n_s * t_new)
    pos = PAST_LEN + (jnp.arange(rows_tile) % t_new)
    q4s, ks, vs, zs, xbcs, dts = _in_proj(xs, g_mix, w_pad, _rope_table(pos))
    att4s, k_sample, v_sample = _attn_sample(q4s, ks, vs, _to_lanes(cache_k[0]), _to_lanes(cache_v[0]))
    prefix_tiles = jnp.pad(state_conv[0], ((0, 0), (t_new - (CONV_WIDTH - 1), 0), (0, 0))).reshape(n_s * t_new, D_XBC)
    y_ssm_s, st_s = _ssd_sample(xbcs, prefix_tiles, zs, dts, state_ssm[0].reshape(n_s, D_SSM, D_STATE), ssd_consts)
    y_sample = _out_mlp(xs, att4s, y_ssm_s, w_out_att, w_out_ssm, g_mlp, w_up_b, w_down_b, g_fin)
    conv_sample = xbcs.reshape(1, n_s, t_new, D_XBC)[:, :, t_new - (CONV_WIDTH - 1):]

    return (y_prompt.reshape(n_p, seq, D_MODEL), y_sample.reshape(n_s, t_new, D_MODEL),
            k_prompt, v_prompt, conv_prompt,
            st_p.reshape(1, n_p, N_SSM_HEADS, SSM_HEAD_DIM, D_STATE),
            _from_lanes(k_sample), _from_lanes(v_sample), conv_sample,
            st_s.reshape(1, n_s, N_SSM_HEADS, SSM_HEAD_DIM, D_STATE))
```

```python
import functools

import numpy as np
import jax
import jax.numpy as jnp
from jax import lax
from jax.experimental import pallas as pl
from jax.experimental.pallas import tpu as pltpu

F32 = jnp.float32
BF16 = jnp.bfloat16

LANES = 128
SUBLANES = 8

D_MODEL = 1024
HEAD_DIM = 64
N_KV_HEADS = 4
Q_PER_KV = 2
D_ATT = 512
D_KV = 256
ROT_HALF = 8
ROPE_THETA = 500000.0
DILATIONS = (1, 4, 16)
N_KEYS = 128
MAX_WINDOW = 2048
ATT_BLK = 128
ATT_UNITS = 4
RESIDUES = 16
PAST_LEN = 8192
ATT_SCALE = HEAD_DIM ** -0.5
NEG_BIG = -1e30
N_SSM_HEADS = 8
SSM_HEAD_DIM = 64
D_SSM = 512
SSM_GROUPS = 2
D_STATE = 128
CONV_WIDTH = 4
CHUNK = 128
D_XBC = 1024
D_FF = 4096
RMS_EPS = 1e-5
D_IN_MAIN = D_ATT + 2 * D_KV + D_SSM + D_XBC
D_IN_PAD = D_IN_MAIN + LANES

VMEM_LIMIT = 56 * 1024 * 1024
ROW_TILE = 512
FF_TILE = 512
SSD_CHUNKS_PER_STEP = 2
SAMPLE_SEQS = CHUNK // 8

_NT = (((1,), (1,)), ((), ()))


def _params(n_axes):
    return pltpu.CompilerParams(dimension_semantics=("arbitrary",) * n_axes,
                                vmem_limit_bytes=VMEM_LIMIT)


def _resident(shape):
    return pl.BlockSpec(shape, lambda *_: (0,) * len(shape), pipeline_mode=pl.Buffered(1))


def _rms(x, g):
    return x * lax.rsqrt(jnp.mean(x * x, axis=-1, keepdims=True) + RMS_EPS) * g


def _silu(x):
    half = 0.5 * x
    return half + half * jnp.tanh(half)


def _softplus(x):
    return jnp.maximum(x, 0.0) + jnp.log1p(jnp.exp(-jnp.abs(x)))


def _in_proj_body(x_ref, g_ref, w_ref, rope_ref, q_ref, k_ref, v_ref, z_ref, xbc_ref, dt_ref, *kv_t_refs):
    xb = _rms(x_ref[...], g_ref[...]).astype(BF16)
    tm = xb.shape[0]
    first_half = (lax.broadcasted_iota(jnp.int32, (tm, LANES), 1) % HEAD_DIM) < ROT_HALF

    def proj(lo, hi):
        return jnp.dot(xb, w_ref[:, lo:hi], preferred_element_type=F32)

    def rope(u, cos, sin):
        partner = jnp.where(first_half, pltpu.roll(u, LANES - ROT_HALF, 1), pltpu.roll(u, ROT_HALF, 1))
        return u * cos + partner * sin

    cq, sq = rope_ref[:, 0:128], rope_ref[:, 128:256]
    ck, sk = rope_ref[:, 256:384], rope_ref[:, 384:512]
    q = proj(0, D_ATT)
    for j in range(D_ATT // LANES):
        q_ref[j] = rope(q[:, j * LANES:(j + 1) * LANES], cq, sq)
    k = proj(D_ATT, D_ATT + D_KV)
    v = proj(D_ATT + D_KV, D_ATT + 2 * D_KV)
    v_ref[...] = v
    for j in range(D_KV // LANES):
        lanes = slice(j * LANES, (j + 1) * LANES)
        kj = rope(k[:, lanes], ck, sk)
        k_ref[:, lanes] = kj
        if kv_t_refs:
            kv_t_refs[0][lanes, :] = kj.T
            kv_t_refs[1][lanes, :] = v[:, lanes].T
    z_ref[...] = proj(D_ATT + 2 * D_KV, D_ATT + 2 * D_KV + D_SSM)
    xbc_ref[...] = proj(D_ATT + 2 * D_KV + D_SSM, D_IN_MAIN)
    dt_ref[...] = proj(D_IN_MAIN, D_IN_PAD)


def _in_proj(x2d, norm_mix, w_pad, rope_tab, seq=None):
    rows = x2d.shape[0]
    tm = min(ROW_TILE, rows)
    n_tab = rope_tab.shape[0] // tm
    row_blk = lambda w: pl.BlockSpec((tm, w), lambda i: (i, 0))
    out_specs = [pl.BlockSpec((4, tm, LANES), lambda i: (0, i, 0)),
                 row_blk(D_KV), row_blk(D_KV), row_blk(D_SSM), row_blk(D_XBC), row_blk(LANES)]
    out_shape = [jax.ShapeDtypeStruct((4, rows, LANES), F32),
                 jax.ShapeDtypeStruct((rows, D_KV), F32), jax.ShapeDtypeStruct((rows, D_KV), F32),
                 jax.ShapeDtypeStruct((rows, D_SSM), F32), jax.ShapeDtypeStruct((rows, D_XBC), F32),
                 jax.ShapeDtypeStruct((rows, LANES), F32)]
    if seq is not None:
        per_seq = seq // tm
        t_blk = pl.BlockSpec((None, D_KV, tm), lambda i: (i // per_seq, 0, i % per_seq))
        out_specs += [t_blk, t_blk]
        out_shape += [jax.ShapeDtypeStruct((rows // seq, D_KV, seq), F32)] * 2
    return pl.pallas_call(
        _in_proj_body,
        grid=(rows // tm,),
        in_specs=[row_blk(D_MODEL), _resident((1, D_MODEL)), _resident((D_MODEL, D_IN_PAD)),
                  pl.BlockSpec((tm, 4 * LANES), lambda i: (i % n_tab, 0))],
        out_specs=out_specs,
        out_shape=out_shape,
        compiler_params=_params(1),
        name="in_proj",
    )(x2d, norm_mix, w_pad, rope_tab)


def _rope_table(pos):
    inv = ROPE_THETA ** (-jnp.arange(0, 2 * ROT_HALF, 2, dtype=F32) / (2 * ROT_HALF))
    ang = pos.astype(F32)[:, None] * inv[None, :]
    cos, sin = jnp.cos(ang), jnp.sin(ang)
    n = pos.shape[0]
    ones = jnp.ones((n, HEAD_DIM - 2 * ROT_HALF), F32)
    cos_h = jnp.concatenate([cos, cos, ones], axis=1)
    sin_h = jnp.concatenate([-sin, sin, 0.0 * ones], axis=1)
    cos_l, sin_l = jnp.tile(cos_h, (1, 2)), jnp.tile(sin_h, (1, 2))
    return jnp.concatenate([cos_l * ATT_SCALE, sin_l * ATT_SCALE, cos_l, sin_l], axis=1)


def _attn_prompt_body(q_ref, k_ref, v_ref, bias1_ref, bias4_ref, caus_ref, o_ref,
                      q3, k3, v3, m3, l3, acc3, nat_s):
    seq = k_ref.shape[0]
    per_res = seq // RESIDUES
    e0 = lax.broadcasted_iota(jnp.int32, (ATT_BLK, LANES), 1) < HEAD_DIM
    zero = jnp.zeros((ATT_BLK, LANES), F32)

    def gather(r, carry):
        rows = pl.ds(r, per_res, stride=RESIDUES)
        for g in range(Q_PER_KV):
            q3[g, r] = q_ref.at[g][rows, :]
        k3[r] = k_ref[rows, :]
        v3[r] = v_ref[rows, :]
        return carry

    lax.fori_loop(0, RESIDUES, gather, 0)

    def scores(q_idx, k_idx, bias):
        parts = []
        for g in range(Q_PER_KV):
            qg = q3.at[g][q_idx].reshape(ATT_BLK, LANES)
            parts += [jnp.where(e0, qg, zero), jnp.where(e0, zero, qg)]
        lhs = jnp.concatenate(parts, axis=0).astype(BF16)
        kb = k3[k_idx]
        nk = kb.shape[0] * kb.shape[1]
        kb = kb.reshape(nk, LANES).astype(BF16)
        vb = v3[k_idx].reshape(nk, LANES).astype(BF16)
        s = lax.dot_general(lhs, kb, _NT, preferred_element_type=F32)
        s = (s.reshape(4, ATT_BLK, nk) + bias[None]).reshape(4 * ATT_BLK, nk)
        m = jnp.max(s, axis=1, keepdims=True)
        p = jnp.exp(s - m)
        l = jnp.sum(p, axis=1, keepdims=True)
        pv = jnp.dot(p.astype(BF16), vb, preferred_element_type=F32)
        mb = jnp.broadcast_to(m, pv.shape)
        lb = jnp.broadcast_to(l, pv.shape)
        out = []
        for g in range(Q_PER_KV):
            r0, r1, r2 = 2 * g * ATT_BLK, (2 * g + 1) * ATT_BLK, (2 * g + 2) * ATT_BLK
            out.append((jnp.where(e0, mb[r0:r1], mb[r1:r2]), jnp.where(e0, lb[r0:r1], lb[r1:r2]),
                        jnp.where(e0, pv[r0:r1], pv[r1:r2])))
        return out

    def process(units, first_group):
        state = [(m3.at[g], l3.at[g], acc3.at[g]) for g in range(Q_PER_KV)]
        old = None
        if not first_group:
            old = [[tuple(ref[q_idx].reshape(ATT_BLK, LANES) for ref in state[g]) for g in range(Q_PER_KV)]
                   for q_idx, _, _ in units]
        new = [scores(*unit) for unit in units]
        for i, (q_idx, _, _) in enumerate(units):
            for g in range(Q_PER_KV):
                mg, lg, acc = new[i][g]
                if not first_group:
                    m_old, l_old, acc_old = old[i][g]
                    m_new = jnp.maximum(m_old, mg)
                    a_old = jnp.exp(m_old - m_new)
                    a_new = jnp.exp(mg - m_new)
                    mg, lg, acc = m_new, a_old * l_old + a_new * lg, a_old * acc_old + a_new * acc
                for ref, val in zip(state[g], (mg, lg, acc)):
                    ref[q_idx] = val.reshape(ref[q_idx].shape)

    for gi, (d, bias_ref) in enumerate(zip(DILATIONS, (bias1_ref, bias4_ref, caus_ref))):
        blocks = seq // d // ATT_BLK
        lanes16 = RESIDUES // d
        depth = ATT_BLK // lanes16

        def unit(u, d=d, blocks=blocks, lanes16=lanes16, depth=depth, bias_ref=bias_ref):
            r, n = u // blocks, u % blocks
            lead = pl.ds(r, lanes16, stride=d)
            q_idx = (lead, pl.ds(pl.multiple_of(n * depth, depth), depth))
            if blocks == 1:
                return q_idx, q_idx, bias_ref[...]
            k_idx = (lead, pl.ds(pl.multiple_of(jnp.maximum(n - 1, 0) * depth, depth), 2 * depth))
            return q_idx, k_idx, bias_ref[jnp.minimum(n, 1)]

        def step(i, carry, unit=unit, first_group=(gi == 0)):
            process([unit(ATT_UNITS * i + j) for j in range(ATT_UNITS)], first_group)
            return carry

        lax.fori_loop(0, d * blocks // ATT_UNITS, step, 0)

    def scatter(r, carry):
        rows = pl.ds(r, per_res, stride=RESIDUES)
        for g in range(Q_PER_KV):
            nat_s.at[g][rows, :] = acc3[g, r] / l3[g, r]
        return carry

    lax.fori_loop(0, RESIDUES, scatter, 0)
    for g in range(Q_PER_KV):
        o_ref[g] = nat_s[g].astype(o_ref.dtype)


def _window_bias(d):
    lanes16 = RESIDUES // d

    def order(n):
        a, ll = np.divmod(np.arange(n), n // lanes16)
        return lanes16 * ll + a

    i = order(ATT_BLK)[:, None]
    if d == RESIDUES:
        return jnp.asarray(np.where(order(ATT_BLK)[None, :] <= i, 0.0, NEG_BIG).astype(np.float32))
    j = order(2 * ATT_BLK)[None, :]
    dist = i + ATT_BLK - j
    inner = np.where((dist >= 0) & (dist <= N_KEYS), 0.0, NEG_BIG)
    first = np.where(j <= i, 0.0, NEG_BIG)
    return jnp.asarray(np.stack([first, inner]).astype(np.float32))


def _attn_prompt(q4, k, v, n_seq, seq):
    biases = [_window_bias(d) for d in DILATIONS]
    slab = (RESIDUES, seq // RESIDUES, LANES)
    return pl.pallas_call(
        _attn_prompt_body,
        grid=(n_seq, 2),
        in_specs=[pl.BlockSpec((2, seq, LANES), lambda b, pp: (pp, b, 0)),
                  pl.BlockSpec((seq, LANES), lambda b, pp: (b, pp)),
                  pl.BlockSpec((seq, LANES), lambda b, pp: (b, pp))] + [_resident(x.shape) for x in biases],
        out_specs=pl.BlockSpec((2, seq, LANES), lambda b, pp: (pp, b, 0)),
        out_shape=jax.ShapeDtypeStruct(q4.shape, BF16),
        scratch_shapes=[pltpu.VMEM((2,) + slab, F32), pltpu.VMEM(slab, F32), pltpu.VMEM(slab, F32)]
                       + [pltpu.VMEM((2,) + slab, F32)] * 3 + [pltpu.VMEM((2, seq, LANES), F32)],
        compiler_params=_params(2),
        name="attn_prompt",
    )(q4, k, v, *biases)


def _attn_sample_pieces(q_ref, kn_ref, vn_ref, ck_ref, cv_ref, mult_ref, att_ref, ko_ref, vo_ref):
    t = kn_ref.shape[0]
    n_past = ck_ref.shape[2]
    tail = slice(n_past - LANES, n_past)

    def shift(c_ref, n_ref, o_ref, rows):
        is_new = lax.broadcasted_iota(jnp.int32, (LANES, LANES), 1) >= LANES - t
        moved = pltpu.roll(c_ref[0, rows, :], n_past - t, 1)
        o_ref[0, rows, :] = moved
        new_rows = jnp.concatenate([n_ref[:, rows], jnp.zeros((LANES - t, LANES), F32)], axis=0)
        o_ref[0, rows, tail] = jnp.where(is_new, pltpu.roll(new_rows.T, LANES - t, 1), moved[:, tail])

    def attend(pp):
        mult = mult_ref[...]
        e0 = lax.broadcasted_iota(jnp.int32, (t, LANES), 1) < HEAD_DIM
        zero = jnp.zeros((t, LANES), F32)
        pad = jnp.zeros((LANES - t, LANES), F32)
        lanes = slice(pp * LANES, (pp + 1) * LANES)
        parts = []
        for g in range(Q_PER_KV):
            qg = q_ref[pp * Q_PER_KV + g]
            parts += [jnp.where(e0, qg, zero), jnp.where(e0, zero, qg)]
        lhs = jnp.concatenate(parts, axis=0).astype(BF16)
        kc = ck_ref[0, lanes, :].astype(BF16)
        vc = cv_ref[0, lanes, :].astype(BF16)
        kn = jnp.concatenate([kn_ref[:, lanes], pad], axis=0).astype(BF16)
        vn = jnp.concatenate([vn_ref[:, lanes], pad], axis=0).astype(BF16)
        s = jnp.concatenate([jnp.dot(lhs, kc, preferred_element_type=F32),
                             lax.dot_general(lhs, kn, _NT, preferred_element_type=F32)], axis=1)
        s = jnp.where(mult > 0.0, s, NEG_BIG)
        m = jnp.max(s, axis=1, keepdims=True)
        p = jnp.exp(s - m) * mult
        l = jnp.sum(p, axis=1, keepdims=True)
        pb = p.astype(BF16)
        pv = (lax.dot_general(pb[:, :n_past], vc, _NT, preferred_element_type=F32)
              + jnp.dot(pb[:, n_past:], vn, preferred_element_type=F32))
        o = pv / l
        for g in range(Q_PER_KV):
            r0 = 2 * g * t
            att_ref[pp * Q_PER_KV + g] = jnp.where(e0, o[r0:r0 + t], o[r0 + t:r0 + 2 * t])

    pieces = []
    for pp in range(N_KV_HEADS // 2):
        rows = slice(pp * LANES, (pp + 1) * LANES)
        pieces += [functools.partial(shift, ck_ref, kn_ref, ko_ref, rows),
                   functools.partial(shift, cv_ref, vn_ref, vo_ref, rows),
                   functools.partial(attend, pp)]
    return pieces


def _key_multiplicity(t, n_past):
    idx = np.concatenate([np.arange(n_past + t), np.full(LANES - t, 10 ** 9)])[None, :]
    dist = n_past + np.arange(t)[:, None] - idx
    mult = np.zeros(dist.shape, np.float32)
    for d in DILATIONS:
        mult += (dist >= 0) & (dist % d == 0) & (dist <= N_KEYS * d)
    return jnp.asarray(np.tile(mult, (4, 1)))


def _attn_sample_specs(q4, k_new, v_new, cache_k, cache_v):
    n_seq, _, n_past = cache_k.shape
    t = k_new.shape[0] // n_seq
    mult = _key_multiplicity(t, n_past)
    cache_blk = pl.BlockSpec((1, D_KV, n_past), lambda b: (b, 0, 0))
    new_blk = pl.BlockSpec((t, D_KV), lambda b: (b, 0))
    q_blk = pl.BlockSpec((4, t, LANES), lambda b: (0, b, 0))
    return ([q4, k_new, v_new, cache_k, cache_v, mult],
            [q_blk, new_blk, new_blk, cache_blk, cache_blk, _resident(mult.shape)],
            [q_blk, cache_blk, cache_blk],
            [jax.ShapeDtypeStruct(q4.shape, F32),
             jax.ShapeDtypeStruct(cache_k.shape, F32), jax.ShapeDtypeStruct(cache_v.shape, F32)])


def _conv_silu(cur_ref, rows, prev_tap, cw_ref, cb_ref, xc_s):
    for j in range(D_XBC // LANES):
        lanes = slice(j * LANES, (j + 1) * LANES)
        cur = cur_ref[rows, lanes]
        y = cb_ref[:, lanes]
        for tap in range(CONV_WIDTH):
            k = CONV_WIDTH - 1 - tap
            shifted = cur if k == 0 else prev_tap(k, lanes, pltpu.roll(cur, k, 0))
            y = y + shifted * cw_ref[tap:tap + 1, lanes]
        xc_s[:, lanes] = _silu(y)


def _select_sum(sel, x):
    total = None
    for _ in range(3):
        piece = x.astype(BF16)
        x = x - piece.astype(F32)
        part = jnp.dot(sel, piece, preferred_element_type=F32)
        total = part if total is None else total + part
    return total


def _ssd_scalars(dt_raw, dtb_ref, alog_ref, seg_mask, seg_end=None):
    dt = _softplus(dt_raw + dtb_ref[...])
    a = dt * (-jnp.exp(alog_ref[...]))
    a_cum = _select_sum(jnp.where(seg_mask, 1.0, 0.0).astype(BF16), a)
    if seg_end is None:
        a_end = jnp.broadcast_to(a_cum[CHUNK - 1:CHUNK, :], a_cum.shape)
    else:
        a_end = _select_sum(seg_end, a_cum)
    to_end = jnp.exp(a_end - a_cum) * dt
    return dt.T, a_cum, a_cum.T, a_end, to_end.T


def _ssd_diag(xc_s, blk, cb, a_cum, a_cum_t, dt_t, seg_mask):
    e0 = lax.broadcasted_iota(jnp.int32, (CHUNK, LANES), 1) < SSM_HEAD_DIM
    x_pair = xc_s[:, blk * LANES:(blk + 1) * LANES].astype(BF16)
    out, grow = [], []
    for e in range(2):
        h = 2 * blk + e
        a_col = jnp.broadcast_to(a_cum[:, h:h + 1], (CHUNK, CHUNK))
        seg = a_col - jnp.broadcast_to(a_cum_t[h:h + 1, :], (CHUNK, CHUNK))
        w = cb * jnp.where(seg_mask, jnp.exp(seg), 0.0) * dt_t[h:h + 1, :]
        out.append(jnp.dot(w.astype(BF16), x_pair, preferred_element_type=F32))
        grow.append(jnp.exp(a_col))
    return jnp.where(e0, out[0], out[1]), jnp.where(e0, grow[0], grow[1])


def _ssd_finish(xc_s, z_ref, rows, y_parts, dsk_ref, nrm_ref, y_ref):
    blocks_per_group = D_SSM // SSM_GROUPS // LANES
    for g in range(SSM_GROUPS):
        gated = []
        for p in range(blocks_per_group):
            blk = g * blocks_per_group + p
            lanes = slice(blk * LANES, (blk + 1) * LANES)
            y = y_parts[blk] + dsk_ref[:, lanes] * xc_s[:, lanes]
            gated.append(y * _silu(z_ref[rows, lanes]))
        ss = sum(jnp.sum(y * y, axis=-1, keepdims=True) for y in gated)
        inv = lax.rsqrt(ss / (D_SSM // SSM_GROUPS) + RMS_EPS)
        for p in range(blocks_per_group):
            blk = g * blocks_per_group + p
            lanes = slice(blk * LANES, (blk + 1) * LANES)
            y_ref[rows, lanes] = (gated[p] * inv * nrm_ref[:, lanes]).astype(y_ref.dtype)


def _ssd_prompt_body(xbc_ref, z_ref, dt_ref, cw_ref, cb_ref, dtb_ref, alog_ref, dsk_ref, nrm_ref,
                     y_ref, st_ref, st_s, xc_bufs):
    seq = xbc_ref.shape[0]
    st_s[...] = jnp.zeros_like(st_s)
    row = lax.broadcasted_iota(jnp.int32, (CHUNK, CHUNK), 0)
    col = lax.broadcasted_iota(jnp.int32, (CHUNK, CHUNK), 1)
    causal = row >= col
    row8 = lax.broadcasted_iota(jnp.int32, (SUBLANES, LANES), 0)
    heads_per_group = N_SSM_HEADS // SSM_GROUPS
    group_rows = heads_per_group * SSM_HEAD_DIM

    def chunk(c, xc_s):
        rows = pl.ds(pl.multiple_of(c * CHUNK, CHUNK), CHUNK)
        prev_rows = pl.ds(pl.multiple_of(jnp.maximum(c * CHUNK - SUBLANES, 0), SUBLANES), SUBLANES)
        have_prev = c > 0

        def prev_tap(k, lanes, rolled):
            prev = jnp.where(have_prev, xbc_ref[prev_rows, lanes], 0.0)
            head = jnp.where(row8 < k, pltpu.roll(prev, k, 0), rolled[0:SUBLANES])
            return jnp.concatenate([head, rolled[SUBLANES:]], axis=0)

        _conv_silu(xbc_ref, rows, prev_tap, cw_ref, cb_ref, xc_s)
        dt_t, a_cum, a_cum_t, a_end, to_end_t = _ssd_scalars(dt_ref[rows, :], dtb_ref, alog_ref, causal)
        x_t = xc_s[:, 0:D_SSM].T
        y_parts = []
        for g in range(SSM_GROUPS):
            bg = xc_s[:, D_SSM + g * D_STATE:D_SSM + (g + 1) * D_STATE].astype(BF16)
            cg = xc_s[:, D_SSM + (SSM_GROUPS + g) * D_STATE:D_SSM + (SSM_GROUPS + g + 1) * D_STATE].astype(BF16)
            cb = lax.dot_general(cg, bg, _NT, preferred_element_type=F32)
            grp = slice(g * group_rows, (g + 1) * group_rows)
            st_prev = st_s[grp, :]
            y_off = lax.dot_general(cg, st_prev.astype(BF16), _NT, preferred_element_type=F32)
            for p in range(group_rows // LANES):
                blk = g * (group_rows // LANES) + p
                y_diag, grow = _ssd_diag(xc_s, blk, cb, a_cum, a_cum_t, dt_t, causal)
                y_parts.append(y_diag + y_off[:, p * LANES:(p + 1) * LANES] * grow)
            xw = []
            for h in range(g * heads_per_group, (g + 1) * heads_per_group):
                hr = slice(h * SSM_HEAD_DIM, (h + 1) * SSM_HEAD_DIM)
                xw.append((x_t[hr, :] * to_end_t[h:h + 1, :]).astype(BF16))
            new = jnp.dot(jnp.concatenate(xw, axis=0), bg, preferred_element_type=F32)
            for i, h in enumerate(range(g * heads_per_group, (g + 1) * heads_per_group)):
                hr = slice(h * SSM_HEAD_DIM, (h + 1) * SSM_HEAD_DIM)
                lr = slice(i * SSM_HEAD_DIM, (i + 1) * SSM_HEAD_DIM)
                decay = jnp.exp(jnp.broadcast_to(a_end[0:1, h:h + 1], (SSM_HEAD_DIM, D_STATE)))
                st_s[hr, :] = decay * st_prev[lr, :] + new[lr, :]
        _ssd_finish(xc_s, z_ref, rows, y_parts, dsk_ref, nrm_ref, y_ref)

    def step(i, carry):
        for j in range(SSD_CHUNKS_PER_STEP):
            chunk(SSD_CHUNKS_PER_STEP * i + j, xc_bufs.at[j])
        return carry

    lax.fori_loop(0, seq // CHUNK // SSD_CHUNKS_PER_STEP, step, 0)
    st_ref[...] = st_s[...]


def _ssd_prompt(xbc, z, dt_raw, n_seq, seq, consts):
    rows = xbc.shape[0]
    blk = lambda w: pl.BlockSpec((seq, w), lambda b: (b, 0))
    state_blk = pl.BlockSpec((None, D_SSM, D_STATE), lambda b: (b, 0, 0))
    return pl.pallas_call(
        _ssd_prompt_body,
        grid=(n_seq,),
        in_specs=[blk(D_XBC), blk(D_SSM), blk(LANES)] + [_resident(c.shape) for c in consts],
        out_specs=[blk(D_SSM), state_blk],
        out_shape=[jax.ShapeDtypeStruct((rows, D_SSM), BF16),
                   jax.ShapeDtypeStruct((n_seq, D_SSM, D_STATE), F32)],
        scratch_shapes=[pltpu.VMEM((D_SSM, D_STATE), F32),
                        pltpu.VMEM((SSD_CHUNKS_PER_STEP, CHUNK, D_XBC), F32)],
        compiler_params=_params(1),
        name="ssd_prompt",
    )(xbc, z, dt_raw, *consts)


def _ssd_sample_body(xbc_ref, pre_ref, z_ref, dt_ref, h0_ref, cw_ref, cb_ref, dtb_ref, alog_ref, dsk_ref,
                     nrm_ref, y_ref, st_ref, xc_s, yoff_s, aend_s):
    t = CHUNK // SAMPLE_SEQS
    row = lax.broadcasted_iota(jnp.int32, (CHUNK, CHUNK), 0)
    col = lax.broadcasted_iota(jnp.int32, (CHUNK, CHUNK), 1)
    same_seq = (row // t) == (col // t)
    seg_mask = same_seq & (row >= col)
    seg_end = jnp.where(col == (row // t) * t + (t - 1), 1.0, 0.0).astype(BF16)
    step = lax.broadcasted_iota(jnp.int32, (CHUNK, LANES), 0) % t
    rows = pl.ds(0, CHUNK)
    heads_per_group = N_SSM_HEADS // SSM_GROUPS
    group_rows = heads_per_group * SSM_HEAD_DIM

    def prev_tap(k, lanes, rolled):
        return jnp.where(step < k, pltpu.roll(pre_ref[:, lanes], (k - t) % CHUNK, 0), rolled)

    _conv_silu(xbc_ref, rows, prev_tap, cw_ref, cb_ref, xc_s)
    dt_t, a_cum, a_cum_t, a_end, to_end_t = _ssd_scalars(
        dt_ref[...], dtb_ref, alog_ref, seg_mask, seg_end)
    aend_s[...] = a_end
    x_t = xc_s[:, 0:D_SSM].T
    col_seq = lax.broadcasted_iota(jnp.int32, (SSM_HEAD_DIM, CHUNK), 1) // t
    y_parts = []
    for g in range(SSM_GROUPS):
        bg = xc_s[:, D_SSM + g * D_STATE:D_SSM + (g + 1) * D_STATE].astype(BF16)
        cg_lanes = slice(D_SSM + (SSM_GROUPS + g) * D_STATE, D_SSM + (SSM_GROUPS + g + 1) * D_STATE)
        cg = xc_s[:, cg_lanes].astype(BF16)
        cb = lax.dot_general(cg, bg, _NT, preferred_element_type=F32)
        grp = slice(g * group_rows, (g + 1) * group_rows)
        xw = []
        for h in range(g * heads_per_group, (g + 1) * heads_per_group):
            hr = slice(h * SSM_HEAD_DIM, (h + 1) * SSM_HEAD_DIM)
            xw.append(x_t[hr, :] * to_end_t[h:h + 1, :])

        def per_seq(b, carry, g=g, bg=bg, cg_lanes=cg_lanes, grp=grp, xw=xw):
            seq_rows = pl.ds(pl.multiple_of(b * t, t), t)
            h_prev = h0_ref[b, grp, :]
            cg_b = xc_s[seq_rows, cg_lanes].astype(BF16)
            yoff_s[seq_rows, g * group_rows:(g + 1) * group_rows] = lax.dot_general(
                cg_b, h_prev.astype(BF16), _NT, preferred_element_type=F32)
            mine = col_seq == b
            xw_b = jnp.concatenate([jnp.where(mine, w, 0.0).astype(BF16) for w in xw], axis=0)
            new = jnp.dot(xw_b, bg, preferred_element_type=F32)
            a_last = aend_s[pl.ds(b * t, 1), :]
            for i, h in enumerate(range(g * heads_per_group, (g + 1) * heads_per_group)):
                lr = slice(i * SSM_HEAD_DIM, (i + 1) * SSM_HEAD_DIM)
                decay = jnp.exp(jnp.broadcast_to(a_last[:, h:h + 1], (SSM_HEAD_DIM, D_STATE)))
                st_ref[b, h * SSM_HEAD_DIM:(h + 1) * SSM_HEAD_DIM, :] = decay * h_prev[lr, :] + new[lr, :]
            return carry

        lax.fori_loop(0, SAMPLE_SEQS, per_seq, 0)
        for p in range(group_rows // LANES):
            blk = g * (group_rows // LANES) + p
            lanes = slice(blk * LANES, (blk + 1) * LANES)
            y_diag, grow = _ssd_diag(xc_s, blk, cb, a_cum, a_cum_t, dt_t, seg_mask)
            y_parts.append(y_diag + yoff_s[:, lanes] * grow)
    _ssd_finish(xc_s, z_ref, rows, y_parts, dsk_ref, nrm_ref, y_ref)


def _ssd_sample(xbc, prefix_tiles, z, dt_raw, h0, consts):
    rows = xbc.shape[0]
    n_seq = h0.shape[0]
    blk = lambda w: pl.BlockSpec((CHUNK, w), lambda i: (i, 0))
    state_blk = pl.BlockSpec((SAMPLE_SEQS, D_SSM, D_STATE), lambda i: (i, 0, 0))
    return pl.pallas_call(
        _ssd_sample_body,
        grid=(rows // CHUNK,),
        in_specs=[blk(D_XBC), blk(D_XBC), blk(D_SSM), blk(LANES), state_blk]
                 + [_resident(c.shape) for c in consts],
        out_specs=[blk(D_SSM), state_blk],
        out_shape=[jax.ShapeDtypeStruct((rows, D_SSM), BF16),
                   jax.ShapeDtypeStruct((n_seq, D_SSM, D_STATE), F32)],
        scratch_shapes=[pltpu.VMEM((CHUNK, D_XBC), F32), pltpu.VMEM((CHUNK, D_SSM), F32),
                        pltpu.VMEM((CHUNK, LANES), F32)],
        compiler_params=_params(1),
        name="ssd_sample",
    )(xbc, prefix_tiles, z, dt_raw, h0, *consts)


def _out_mlp_body(x_ref, att_ref, ssm_ref, woa_ref, wos_ref, g_ref, wu_ref, wd_ref, gf_ref, o_ref, side_work=()):
    side_work = list(side_work)
    att = jnp.concatenate([att_ref[j] for j in range(att_ref.shape[0])], axis=1).astype(BF16)
    h = (x_ref[...] + jnp.dot(att, woa_ref[...], preferred_element_type=F32)
         + jnp.dot(ssm_ref[...], wos_ref[...], preferred_element_type=F32))
    hn = _rms(h, g_ref[...]).astype(BF16)
    mlp = None
    for c in range(D_FF // FF_TILE):
        if side_work:
            side_work.pop(0)()
        cols = slice(c * FF_TILE, (c + 1) * FF_TILE)
        u = jnp.maximum(jnp.dot(hn, wu_ref[:, cols], preferred_element_type=F32), 0.0)
        down = jnp.dot((u * u).astype(BF16), wd_ref[cols, :], preferred_element_type=F32)
        mlp = down if mlp is None else mlp + down
    for work in side_work:
        work()
    o_ref[...] = _rms(h + mlp, gf_ref[...])


def _out_mlp_attn_body(*refs):
    n_mlp, n_att = 9, 6
    o_ref = refs[n_mlp + n_att]
    pieces = _attn_sample_pieces(*refs[n_mlp:n_mlp + n_att], *refs[n_mlp + n_att + 1:])
    _out_mlp_body(*refs[:n_mlp], o_ref, side_work=pieces)


def _out_mlp(x2d, att4, ssm, w_out_att, w_out_ssm, norm_mlp, w_up, w_down, norm_final, sample=None):
    rows = x2d.shape[0]
    tm = min(ROW_TILE, rows) if sample is None else rows // sample[3].shape[0]
    row_blk = lambda w: pl.BlockSpec((tm, w), lambda i: (i, 0))
    operands = [x2d, att4, ssm, w_out_att, w_out_ssm, norm_mlp, w_up, w_down, norm_final]
    in_specs = [row_blk(D_MODEL), pl.BlockSpec((4, tm, LANES), lambda i: (0, i, 0)), row_blk(D_SSM),
                _resident((D_ATT, D_MODEL)), _resident((D_SSM, D_MODEL)), _resident((1, D_MODEL)),
                _resident((D_MODEL, D_FF)), _resident((D_FF, D_MODEL)), _resident((1, D_MODEL))]
    out_specs = [row_blk(D_MODEL)]
    out_shape = [jax.ShapeDtypeStruct((rows, D_MODEL), F32)]
    body = _out_mlp_body
    if sample is not None:
        att_operands, att_in, att_out, att_shape = _attn_sample_specs(*sample)
        assert rows == tm * sample[3].shape[0] and tm % SUBLANES == 0
        operands += att_operands
        in_specs += att_in
        out_specs += att_out
        out_shape += att_shape
        body = _out_mlp_attn_body
    out = pl.pallas_call(
        body,
        grid=(rows // tm,),
        in_specs=in_specs,
        out_specs=out_specs,
        out_shape=out_shape,
        compiler_params=_params(1),
        name="out_mlp",
    )(*operands)
    return out[0] if sample is None else out


def _q_permutation():
    perm = []
    for pp in range(N_KV_HEADS // 2):
        for g in range(Q_PER_KV):
            for e in range(2):
                base = (2 * pp + e) * Q_PER_KV * HEAD_DIM + g * HEAD_DIM
                perm += list(range(base, base + HEAD_DIM))
    return np.asarray(perm, np.int32)


def _to_lanes(buf):
    n, pos = buf.shape[:2]
    return jnp.transpose(buf, (0, 2, 3, 1)).reshape(n, D_KV, pos)


def _from_lanes(buf_t):
    n, _, pos = buf_t.shape
    return jnp.transpose(buf_t.reshape(n, N_KV_HEADS, HEAD_DIM, pos), (0, 3, 1, 2))[None]


def _pad_lanes(v):
    v = v.reshape(1, -1).astype(F32)
    return jnp.pad(v, ((0, 0), (0, LANES - v.shape[1])))


def kernel(x_prompt, x_sample, cache_k, cache_v, state_conv, state_ssm, w_in, w_out, conv_w, conv_b, dt_bias, a_log, d_skip, ssm_norm, norm_mix, norm_mlp, w_up, w_down, norm_final):
    depth = w_in.shape[0]
    assert depth == 1, "single-layer step"
    n_p, seq, _ = x_prompt.shape
    n_s, t_new, _ = x_sample.shape
    n_past = cache_k.shape[2]
    assert seq == MAX_WINDOW and n_past == MAX_WINDOW and t_new == SUBLANES and n_s % SAMPLE_SEQS == 0

    perm = _q_permutation()
    w = w_in[0]
    w_pad = jnp.concatenate(
        [w[:, perm], w[:, D_ATT:], jnp.zeros((D_MODEL, D_IN_PAD - w.shape[1]), F32)], axis=1).astype(BF16)
    w_out_att = w_out[0][perm, :].astype(BF16)
    w_out_ssm = w_out[0][D_ATT:, :].astype(BF16)
    w_up_b, w_down_b = w_up[0].astype(BF16), w_down[0].astype(BF16)
    g_mix, g_mlp, g_fin = (v.reshape(1, D_MODEL) for v in (norm_mix[0], norm_mlp[0], norm_final))
    ssd_consts = (conv_w[0], conv_b[0].reshape(1, D_XBC), _pad_lanes(dt_bias[0]), _pad_lanes(a_log[0]),
                  jnp.repeat(d_skip[0], SSM_HEAD_DIM).reshape(1, D_SSM), ssm_norm[0].reshape(1, D_SSM))

    xp = x_prompt.reshape(n_p * seq, D_MODEL)
    q4, k, v, z, xbc, dt_raw, k_t, v_t = _in_proj(xp, g_mix, w_pad, _rope_table(jnp.arange(seq)), seq=seq)
    att4 = _attn_prompt(q4, k, v, n_p, seq)
    y_ssm, st_p = _ssd_prompt(xbc, z, dt_raw, n_p, seq, ssd_consts)
    k_prompt, v_prompt = _from_lanes(k_t), _from_lanes(v_t)
    conv_prompt = xbc.reshape(1, n_p, seq, D_XBC)[:, :, seq - (CONV_WIDTH - 1):]

    xs = x_sample.reshape(n_s * t_new, D_MODEL)
    rows_tile = min(ROW_TILE, n_s * t_new)
    pos = PAST_LEN + (jnp.arange(rows_tile) % t_new)
    q4s, ks, vs, zs, xbcs, dts = _in_proj(xs, g_mix, w_pad, _rope_table(pos))
    y_prompt, att4s, k_sample, v_sample = _out_mlp(
        xp, att4, y_ssm, w_out_att, w_out_ssm, g_mlp, w_up_b, w_down_b, g_fin,
        sample=(q4s, ks, vs, _to_lanes(cache_k[0]), _to_lanes(cache_v[0])))
    prefix_tiles = jnp.pad(state_conv[0], ((0, 0), (t_new - (CONV_WIDTH - 1), 0), (0, 0))).reshape(n_s * t_new, D_XBC)
    y_ssm_s, st_s = _ssd_sample(xbcs, prefix_tiles, zs, dts, state_ssm[0].reshape(n_s, D_SSM, D_STATE), ssd_consts)
    y_sample = _out_mlp(xs, att4s, y_ssm_s, w_out_att, w_out_ssm, g_mlp, w_up_b, w_down_b, g_fin)
    conv_sample = xbcs.reshape(1, n_s, t_new, D_XBC)[:, :, t_new - (CONV_WIDTH - 1):]

    return (y_prompt.reshape(n_p, seq, D_MODEL), y_sample.reshape(n_s, t_new, D_MODEL),
            k_prompt, v_prompt, conv_prompt,
            st_p.reshape(1, n_p, N_SSM_HEADS, SSM_HEAD_DIM, D_STATE),
            _from_lanes(k_sample), _from_lanes(v_sample), conv_sample,
            st_s.reshape(1, n_s, N_SSM_HEADS, SSM_HEAD_DIM, D_STATE))
```

```python
import functools

import numpy as np
import jax
import jax.numpy as jnp
from jax import lax
from jax.experimental import pallas as pl
from jax.experimental.pallas import tpu as pltpu

F32 = jnp.float32
BF16 = jnp.bfloat16

LANES = 128
SUBLANES = 8

D_MODEL = 1024
HEAD_DIM = 64
N_KV_HEADS = 4
Q_PER_KV = 2
D_ATT = 512
D_KV = 256
ROT_HALF = 8
ROPE_THETA = 500000.0
DILATIONS = (1, 4, 16)
N_KEYS = 128
MAX_WINDOW = 2048
ATT_BLK = 128
ATT_UNITS = 8
RESIDUES = 16
PAST_LEN = 8192
ATT_SCALE = HEAD_DIM ** -0.5
NEG_BIG = -1e30
N_SSM_HEADS = 8
SSM_HEAD_DIM = 64
D_SSM = 512
SSM_GROUPS = 2
D_STATE = 128
CONV_WIDTH = 4
CHUNK = 128
D_XBC = 1024
D_FF = 4096
RMS_EPS = 1e-5
D_IN_MAIN = D_ATT + 2 * D_KV + D_SSM + D_XBC
D_IN_PAD = D_IN_MAIN + LANES

VMEM_LIMIT = 56 * 1024 * 1024
ROW_TILE = 512
FF_TILE = 1024
SSD_CHUNKS_PER_STEP = 4
SAMPLE_SEQS = CHUNK // 8

_NT = (((1,), (1,)), ((), ()))


def _params(n_axes):
    return pltpu.CompilerParams(dimension_semantics=("arbitrary",) * n_axes,
                                vmem_limit_bytes=VMEM_LIMIT)


def _resident(shape):
    return pl.BlockSpec(shape, lambda *_: (0,) * len(shape), pipeline_mode=pl.Buffered(1))


def _rms(x, g):
    return x * lax.rsqrt(jnp.mean(x * x, axis=-1, keepdims=True) + RMS_EPS) * g


def _silu(x):
    half = 0.5 * x
    return half + half * jnp.tanh(half)


def _softplus(x):
    return jnp.maximum(x, 0.0) + jnp.log1p(jnp.exp(-jnp.abs(x)))


def _in_proj_body(x_ref, g_ref, w_ref, rope_ref, q_ref, k_ref, v_ref, z_ref, xbc_ref, dt_ref, *kv_t_refs):
    xb = _rms(x_ref[...], g_ref[...]).astype(BF16)
    tm = xb.shape[0]
    first_half = (lax.broadcasted_iota(jnp.int32, (tm, LANES), 1) % HEAD_DIM) < ROT_HALF

    def proj(lo, hi):
        return jnp.dot(xb, w_ref[:, lo:hi], preferred_element_type=F32)

    def rope(u, cos, sin):
        partner = jnp.where(first_half, pltpu.roll(u, LANES - ROT_HALF, 1), pltpu.roll(u, ROT_HALF, 1))
        return u * cos + partner * sin

    cq, sq = rope_ref[:, 0:128], rope_ref[:, 128:256]
    ck, sk = rope_ref[:, 256:384], rope_ref[:, 384:512]
    q = proj(0, D_ATT)
    for j in range(D_ATT // LANES):
        q_ref[j] = rope(q[:, j * LANES:(j + 1) * LANES], cq, sq)
    k = proj(D_ATT, D_ATT + D_KV)
    v = proj(D_ATT + D_KV, D_ATT + 2 * D_KV)
    v_ref[...] = v
    for j in range(D_KV // LANES):
        lanes = slice(j * LANES, (j + 1) * LANES)
        kj = rope(k[:, lanes], ck, sk)
        k_ref[:, lanes] = kj
        if kv_t_refs:
            kv_t_refs[0][lanes, :] = kj.T
            kv_t_refs[1][lanes, :] = v[:, lanes].T
    z_ref[...] = proj(D_ATT + 2 * D_KV, D_ATT + 2 * D_KV + D_SSM)
    xbc_ref[...] = proj(D_ATT + 2 * D_KV + D_SSM, D_IN_MAIN)
    dt_ref[...] = proj(D_IN_MAIN, D_IN_PAD)


def _in_proj(x2d, norm_mix, w_pad, rope_tab, seq=None):
    rows = x2d.shape[0]
    tm = min(ROW_TILE, rows)
    n_tab = rope_tab.shape[0] // tm
    row_blk = lambda w: pl.BlockSpec((tm, w), lambda i: (i, 0))
    out_specs = [pl.BlockSpec((4, tm, LANES), lambda i: (0, i, 0)),
                 row_blk(D_KV), row_blk(D_KV), row_blk(D_SSM), row_blk(D_XBC), row_blk(LANES)]
    out_shape = [jax.ShapeDtypeStruct((4, rows, LANES), F32),
                 jax.ShapeDtypeStruct((rows, D_KV), F32), jax.ShapeDtypeStruct((rows, D_KV), F32),
                 jax.ShapeDtypeStruct((rows, D_SSM), F32), jax.ShapeDtypeStruct((rows, D_XBC), F32),
                 jax.ShapeDtypeStruct((rows, LANES), F32)]
    if seq is not None:
        per_seq = seq // tm
        t_blk = pl.BlockSpec((None, D_KV, tm), lambda i: (i // per_seq, 0, i % per_seq))
        out_specs += [t_blk, t_blk]
        out_shape += [jax.ShapeDtypeStruct((rows // seq, D_KV, seq), F32)] * 2
    return pl.pallas_call(
        _in_proj_body,
        grid=(rows // tm,),
        in_specs=[row_blk(D_MODEL), _resident((1, D_MODEL)), _resident((D_MODEL, D_IN_PAD)),
                  pl.BlockSpec((tm, 4 * LANES), lambda i: (i % n_tab, 0))],
        out_specs=out_specs,
        out_shape=out_shape,
        compiler_params=_params(1),
        name="in_proj",
    )(x2d, norm_mix, w_pad, rope_tab)


def _rope_table(pos):
    inv = ROPE_THETA ** (-jnp.arange(0, 2 * ROT_HALF, 2, dtype=F32) / (2 * ROT_HALF))
    ang = pos.astype(F32)[:, None] * inv[None, :]
    cos, sin = jnp.cos(ang), jnp.sin(ang)
    n = pos.shape[0]
    ones = jnp.ones((n, HEAD_DIM - 2 * ROT_HALF), F32)
    cos_h = jnp.concatenate([cos, cos, ones], axis=1)
    sin_h = jnp.concatenate([-sin, sin, 0.0 * ones], axis=1)
    cos_l, sin_l = jnp.tile(cos_h, (1, 2)), jnp.tile(sin_h, (1, 2))
    return jnp.concatenate([cos_l * ATT_SCALE, sin_l * ATT_SCALE, cos_l, sin_l], axis=1)


def _attn_prompt_body(q_ref, k_ref, v_ref, bias1_ref, bias4_ref, caus_ref, o_ref,
                      q3, k3, v3, m3, l3, acc3, nat_s):
    seq = k_ref.shape[0]
    per_res = seq // RESIDUES
    e0 = lax.broadcasted_iota(jnp.int32, (ATT_BLK, LANES), 1) < HEAD_DIM
    zero = jnp.zeros((ATT_BLK, LANES), F32)

    def gather(r, carry):
        rows = pl.ds(r, per_res, stride=RESIDUES)
        for g in range(Q_PER_KV):
            q3[g, r] = q_ref.at[g][rows, :]
        k3[r] = k_ref[rows, :]
        v3[r] = v_ref[rows, :]
        return carry

    lax.fori_loop(0, RESIDUES, gather, 0)

    def scores(q_idx, k_idx, bias):
        parts = []
        for g in range(Q_PER_KV):
            qg = q3.at[g][q_idx].reshape(ATT_BLK, LANES)
            parts += [jnp.where(e0, qg, zero), jnp.where(e0, zero, qg)]
        lhs = jnp.concatenate(parts, axis=0).astype(BF16)
        kb = k3[k_idx]
        nk = kb.shape[0] * kb.shape[1]
        kb = kb.reshape(nk, LANES).astype(BF16)
        vb = v3[k_idx].reshape(nk, LANES).astype(BF16)
        s = lax.dot_general(lhs, kb, _NT, preferred_element_type=F32)
        s = (s.reshape(4, ATT_BLK, nk) + bias[None]).reshape(4 * ATT_BLK, nk)
        m = jnp.max(s, axis=1, keepdims=True)
        p = jnp.exp(s - m)
        l = jnp.sum(p, axis=1, keepdims=True)
        pv = jnp.dot(p.astype(BF16), vb, preferred_element_type=F32)
        mb = jnp.broadcast_to(m, pv.shape)
        lb = jnp.broadcast_to(l, pv.shape)
        out = []
        for g in range(Q_PER_KV):
            r0, r1, r2 = 2 * g * ATT_BLK, (2 * g + 1) * ATT_BLK, (2 * g + 2) * ATT_BLK
            out.append((jnp.where(e0, mb[r0:r1], mb[r1:r2]), jnp.where(e0, lb[r0:r1], lb[r1:r2]),
                        jnp.where(e0, pv[r0:r1], pv[r1:r2])))
        return out

    def process(units, first_group):
        state = [(m3.at[g], l3.at[g], acc3.at[g]) for g in range(Q_PER_KV)]
        old = None
        if not first_group:
            old = [[tuple(ref[q_idx].reshape(ATT_BLK, LANES) for ref in state[g]) for g in range(Q_PER_KV)]
                   for q_idx, _, _ in units]
        new = [scores(*unit) for unit in units]
        for i, (q_idx, _, _) in enumerate(units):
            for g in range(Q_PER_KV):
                mg, lg, acc = new[i][g]
                if not first_group:
                    m_old, l_old, acc_old = old[i][g]
                    m_new = jnp.maximum(m_old, mg)
                    a_old = jnp.exp(m_old - m_new)
                    a_new = jnp.exp(mg - m_new)
                    mg, lg, acc = m_new, a_old * l_old + a_new * lg, a_old * acc_old + a_new * acc
                for ref, val in zip(state[g], (mg, lg, acc)):
                    ref[q_idx] = val.reshape(ref[q_idx].shape)

    for gi, (d, bias_ref) in enumerate(zip(DILATIONS, (bias1_ref, bias4_ref, caus_ref))):
        blocks = seq // d // ATT_BLK
        lanes16 = RESIDUES // d
        depth = ATT_BLK // lanes16

        def unit(u, d=d, blocks=blocks, lanes16=lanes16, depth=depth, bias_ref=bias_ref):
            r, n = u // blocks, u % blocks
            lead = pl.ds(r, lanes16, stride=d)
            q_idx = (lead, pl.ds(pl.multiple_of(n * depth, depth), depth))
            if blocks == 1:
                return q_idx, q_idx, bias_ref[...]
            k_idx = (lead, pl.ds(pl.multiple_of(jnp.maximum(n - 1, 0) * depth, depth), 2 * depth))
            return q_idx, k_idx, bias_ref[jnp.minimum(n, 1)]

        def step(i, carry, unit=unit, first_group=(gi == 0)):
            process([unit(ATT_UNITS * i + j) for j in range(ATT_UNITS)], first_group)
            return carry

        lax.fori_loop(0, d * blocks // ATT_UNITS, step, 0)

    def scatter(r, carry):
        rows = pl.ds(r, per_res, stride=RESIDUES)
        for g in range(Q_PER_KV):
            nat_s.at[g][rows, :] = acc3[g, r] / l3[g, r]
        return carry

    lax.fori_loop(0, RESIDUES, scatter, 0)
    for g in range(Q_PER_KV):
        o_ref[g] = nat_s[g].astype(o_ref.dtype)


def _window_bias(d):
    lanes16 = RESIDUES // d

    def order(n):
        a, ll = np.divmod(np.arange(n), n // lanes16)
        return lanes16 * ll + a

    i = order(ATT_BLK)[:, None]
    if d == RESIDUES:
        return jnp.asarray(np.where(order(ATT_BLK)[None, :] <= i, 0.0, NEG_BIG).astype(np.float32))
    j = order(2 * ATT_BLK)[None, :]
    dist = i + ATT_BLK - j
    inner = np.where((dist >= 0) & (dist <= N_KEYS), 0.0, NEG_BIG)
    first = np.where(j <= i, 0.0, NEG_BIG)
    return jnp.asarray(np.stack([first, inner]).astype(np.float32))


def _attn_prompt(q4, k, v, n_seq, seq):
    biases = [_window_bias(d) for d in DILATIONS]
    slab = (RESIDUES, seq // RESIDUES, LANES)
    return pl.pallas_call(
        _attn_prompt_body,
        grid=(n_seq, 2),
        in_specs=[pl.BlockSpec((2, seq, LANES), lambda b, pp: (pp, b, 0)),
                  pl.BlockSpec((seq, LANES), lambda b, pp: (b, pp)),
                  pl.BlockSpec((seq, LANES), lambda b, pp: (b, pp))] + [_resident(x.shape) for x in biases],
        out_specs=pl.BlockSpec((2, seq, LANES), lambda b, pp: (pp, b, 0)),
        out_shape=jax.ShapeDtypeStruct(q4.shape, BF16),
        scratch_shapes=[pltpu.VMEM((2,) + slab, F32), pltpu.VMEM(slab, F32), pltpu.VMEM(slab, F32)]
                       + [pltpu.VMEM((2,) + slab, F32)] * 3 + [pltpu.VMEM((2, seq, LANES), F32)],
        compiler_params=_params(2),
        name="attn_prompt",
    )(q4, k, v, *biases)


def _attn_sample_pieces(q_ref, kn_ref, vn_ref, ck_ref, cv_ref, mult_ref, att_ref, ko_ref, vo_ref):
    t = kn_ref.shape[0]
    n_past = ck_ref.shape[2]
    tail = slice(n_past - LANES, n_past)

    def shift(c_ref, n_ref, o_ref, rows):
        is_new = lax.broadcasted_iota(jnp.int32, (LANES, LANES), 1) >= LANES - t
        moved = pltpu.roll(c_ref[0, rows, :], n_past - t, 1)
        o_ref[0, rows, :] = moved
        new_rows = jnp.concatenate([n_ref[:, rows], jnp.zeros((LANES - t, LANES), F32)], axis=0)
        o_ref[0, rows, tail] = jnp.where(is_new, pltpu.roll(new_rows.T, LANES - t, 1), moved[:, tail])

    def attend():
        mult = mult_ref[...]
        e0 = lax.broadcasted_iota(jnp.int32, (t, LANES), 1) < HEAD_DIM
        zero = jnp.zeros((t, LANES), F32)
        pairs = N_KV_HEADS // 2
        parts = []
        for pp in range(pairs):
            for g in range(Q_PER_KV):
                qg = q_ref[pp * Q_PER_KV + g]
                for masked in (jnp.where(e0, qg, zero), jnp.where(e0, zero, qg)):
                    parts.append(jnp.concatenate([masked if j == pp else zero for j in range(pairs)], axis=1))
        lhs = jnp.concatenate(parts, axis=0).astype(BF16)
        pad = jnp.zeros((LANES - t, D_KV), F32)
        kn = jnp.concatenate([kn_ref[...], pad], axis=0).astype(BF16)
        vn = jnp.concatenate([vn_ref[...], pad], axis=0).astype(BF16)
        s = jnp.concatenate([jnp.dot(lhs, ck_ref[0].astype(BF16), preferred_element_type=F32),
                             lax.dot_general(lhs, kn, _NT, preferred_element_type=F32)], axis=1)
        s = jnp.where(mult > 0.0, s, NEG_BIG)
        m = jnp.max(s, axis=1, keepdims=True)
        p = jnp.exp(s - m) * mult
        l = jnp.sum(p, axis=1, keepdims=True)
        pb = p.astype(BF16)
        pv = (lax.dot_general(pb[:, :n_past], cv_ref[0].astype(BF16), _NT, preferred_element_type=F32)
              + jnp.dot(pb[:, n_past:], vn, preferred_element_type=F32))
        o = pv / l
        for pp in range(pairs):
            lanes = slice(pp * LANES, (pp + 1) * LANES)
            for g in range(Q_PER_KV):
                r0 = (pp * Q_PER_KV + g) * 2 * t
                att_ref[pp * Q_PER_KV + g] = jnp.where(e0, o[r0:r0 + t, lanes], o[r0 + t:r0 + 2 * t, lanes])

    pieces = [attend]
    for pp in range(N_KV_HEADS // 2):
        rows = slice(pp * LANES, (pp + 1) * LANES)
        pieces += [functools.partial(shift, ck_ref, kn_ref, ko_ref, rows),
                   functools.partial(shift, cv_ref, vn_ref, vo_ref, rows)]
    return pieces


def _key_multiplicity(t, n_past):
    idx = np.concatenate([np.arange(n_past + t), np.full(LANES - t, 10 ** 9)])[None, :]
    dist = n_past + np.arange(t)[:, None] - idx
    mult = np.zeros(dist.shape, np.float32)
    for d in DILATIONS:
        mult += (dist >= 0) & (dist % d == 0) & (dist <= N_KEYS * d)
    return jnp.asarray(np.tile(mult, (2 * N_KV_HEADS, 1)))


def _attn_sample_specs(q4, k_new, v_new, cache_k, cache_v):
    n_seq, _, n_past = cache_k.shape
    t = k_new.shape[0] // n_seq
    mult = _key_multiplicity(t, n_past)
    cache_blk = pl.BlockSpec((1, D_KV, n_past), lambda b: (b, 0, 0))
    new_blk = pl.BlockSpec((t, D_KV), lambda b: (b, 0))
    q_blk = pl.BlockSpec((4, t, LANES), lambda b: (0, b, 0))
    return ([q4, k_new, v_new, cache_k, cache_v, mult],
            [q_blk, new_blk, new_blk, cache_blk, cache_blk, _resident(mult.shape)],
            [q_blk, cache_blk, cache_blk],
            [jax.ShapeDtypeStruct(q4.shape, F32),
             jax.ShapeDtypeStruct(cache_k.shape, F32), jax.ShapeDtypeStruct(cache_v.shape, F32)])


def _conv_silu(delayed, cw_ref, cb_ref, xc_s):
    for j in range(D_XBC // LANES):
        lanes = slice(j * LANES, (j + 1) * LANES)
        y = cb_ref[:, lanes]
        for tap in range(CONV_WIDTH):
            y = y + delayed(CONV_WIDTH - 1 - tap, lanes) * cw_ref[tap:tap + 1, lanes]
        xc_s[:, lanes] = _silu(y)


def _select_sum(sel, x):
    total = None
    for _ in range(3):
        piece = x.astype(BF16)
        x = x - piece.astype(F32)
        part = jnp.dot(sel, piece, preferred_element_type=F32)
        total = part if total is None else total + part
    return total


def _ssd_scalars(dt_raw, dtb_ref, alog_ref, seg_mask, seg_end=None):
    dt = _softplus(dt_raw + dtb_ref[...])
    a = dt * (-jnp.exp(alog_ref[...]))
    a_cum = _select_sum(jnp.where(seg_mask, 1.0, 0.0).astype(BF16), a)
    if seg_end is None:
        a_end = jnp.broadcast_to(a_cum[CHUNK - 1:CHUNK, :], a_cum.shape)
    else:
        a_end = _select_sum(seg_end, a_cum)
    to_end = jnp.exp(a_end - a_cum) * dt
    return dt.T, a_cum, a_cum.T, a_end, to_end.T


def _ssd_diag(xc_s, blk, cb, a_cum, a_cum_t, dt_t, seg_mask):
    e0 = lax.broadcasted_iota(jnp.int32, (CHUNK, LANES), 1) < SSM_HEAD_DIM
    x_pair = xc_s[:, blk * LANES:(blk + 1) * LANES].astype(BF16)
    out, grow = [], []
    for e in range(2):
        h = 2 * blk + e
        a_col = jnp.broadcast_to(a_cum[:, h:h + 1], (CHUNK, CHUNK))
        seg = a_col - jnp.broadcast_to(a_cum_t[h:h + 1, :], (CHUNK, CHUNK))
        w = cb * jnp.where(seg_mask, jnp.exp(seg), 0.0) * dt_t[h:h + 1, :]
        out.append(jnp.dot(w.astype(BF16), x_pair, preferred_element_type=F32))
        grow.append(jnp.exp(a_col))
    return jnp.where(e0, out[0], out[1]), jnp.where(e0, grow[0], grow[1])


def _ssd_finish(xc_s, z_ref, rows, y_parts, dsk_ref, nrm_ref, y_ref):
    blocks_per_group = D_SSM // SSM_GROUPS // LANES
    for g in range(SSM_GROUPS):
        gated = []
        for p in range(blocks_per_group):
            blk = g * blocks_per_group + p
            lanes = slice(blk * LANES, (blk + 1) * LANES)
            y = y_parts[blk] + dsk_ref[:, lanes] * xc_s[:, lanes]
            gated.append(y * _silu(z_ref[rows, lanes]))
        ss = sum(jnp.sum(y * y, axis=-1, keepdims=True) for y in gated)
        inv = lax.rsqrt(ss / (D_SSM // SSM_GROUPS) + RMS_EPS)
        for p in range(blocks_per_group):
            blk = g * blocks_per_group + p
            lanes = slice(blk * LANES, (blk + 1) * LANES)
            y_ref[rows, lanes] = (gated[p] * inv * nrm_ref[:, lanes]).astype(y_ref.dtype)


def _ssd_prompt_body(xbc_ref, z_ref, dt_ref, cw_ref, cb_ref, dtb_ref, alog_ref, dsk_ref, nrm_ref,
                     y_ref, st_ref, st_s, xc_bufs):
    seq = xbc_ref.shape[0]
    st_s[...] = jnp.zeros_like(st_s)
    row = lax.broadcasted_iota(jnp.int32, (CHUNK, CHUNK), 0)
    col = lax.broadcasted_iota(jnp.int32, (CHUNK, CHUNK), 1)
    causal = row >= col
    row8 = lax.broadcasted_iota(jnp.int32, (SUBLANES, LANES), 0)
    heads_per_group = N_SSM_HEADS // SSM_GROUPS
    group_rows = heads_per_group * SSM_HEAD_DIM

    def chunk(c, xc_s):
        rows = pl.ds(pl.multiple_of(c * CHUNK, CHUNK), CHUNK)
        prev_rows = pl.ds(pl.multiple_of(jnp.maximum(c * CHUNK - SUBLANES, 0), SUBLANES), SUBLANES)
        have_prev = c > 0

        def delayed(k, lanes):
            cur = xbc_ref[rows, lanes]
            if k == 0:
                return cur
            rolled = pltpu.roll(cur, k, 0)
            prev = jnp.where(have_prev, xbc_ref[prev_rows, lanes], 0.0)
            head = jnp.where(row8 < k, pltpu.roll(prev, k, 0), rolled[0:SUBLANES])
            return jnp.concatenate([head, rolled[SUBLANES:]], axis=0)

        _conv_silu(delayed, cw_ref, cb_ref, xc_s)
        dt_t, a_cum, a_cum_t, a_end, to_end_t = _ssd_scalars(dt_ref[rows, :], dtb_ref, alog_ref, causal)
        x_t = xc_s[:, 0:D_SSM].T
        y_parts = []
        for g in range(SSM_GROUPS):
            bg = xc_s[:, D_SSM + g * D_STATE:D_SSM + (g + 1) * D_STATE].astype(BF16)
            cg = xc_s[:, D_SSM + (SSM_GROUPS + g) * D_STATE:D_SSM + (SSM_GROUPS + g + 1) * D_STATE].astype(BF16)
            cb = lax.dot_general(cg, bg, _NT, preferred_element_type=F32)
            grp = slice(g * group_rows, (g + 1) * group_rows)
            st_prev = st_s[grp, :]
            y_off = lax.dot_general(cg, st_prev.astype(BF16), _NT, preferred_element_type=F32)
            for p in range(group_rows // LANES):
                blk = g * (group_rows // LANES) + p
                y_diag, grow = _ssd_diag(xc_s, blk, cb, a_cum, a_cum_t, dt_t, causal)
                y_parts.append(y_diag + y_off[:, p * LANES:(p + 1) * LANES] * grow)
            xw = []
            for h in range(g * heads_per_group, (g + 1) * heads_per_group):
                hr = slice(h * SSM_HEAD_DIM, (h + 1) * SSM_HEAD_DIM)
                xw.append((x_t[hr, :] * to_end_t[h:h + 1, :]).astype(BF16))
            new = jnp.dot(jnp.concatenate(xw, axis=0), bg, preferred_element_type=F32)
            for i, h in enumerate(range(g * heads_per_group, (g + 1) * heads_per_group)):
                hr = slice(h * SSM_HEAD_DIM, (h + 1) * SSM_HEAD_DIM)
                lr = slice(i * SSM_HEAD_DIM, (i + 1) * SSM_HEAD_DIM)
                decay = jnp.exp(jnp.broadcast_to(a_end[0:1, h:h + 1], (SSM_HEAD_DIM, D_STATE)))
                st_s[hr, :] = decay * st_prev[lr, :] + new[lr, :]
        _ssd_finish(xc_s, z_ref, rows, y_parts, dsk_ref, nrm_ref, y_ref)

    def step(i, carry):
        for j in range(SSD_CHUNKS_PER_STEP):
            chunk(SSD_CHUNKS_PER_STEP * i + j, xc_bufs.at[j])
        return carry

    lax.fori_loop(0, seq // CHUNK // SSD_CHUNKS_PER_STEP, step, 0)
    st_ref[...] = st_s[...]


def _ssd_prompt(xbc, z, dt_raw, n_seq, seq, consts):
    rows = xbc.shape[0]
    blk = lambda w: pl.BlockSpec((seq, w), lambda b: (b, 0))
    state_blk = pl.BlockSpec((None, D_SSM, D_STATE), lambda b: (b, 0, 0))
    return pl.pallas_call(
        _ssd_prompt_body,
        grid=(n_seq,),
        in_specs=[blk(D_XBC), blk(D_SSM), blk(LANES)] + [_resident(c.shape) for c in consts],
        out_specs=[blk(D_SSM), state_blk],
        out_shape=[jax.ShapeDtypeStruct((rows, D_SSM), BF16),
                   jax.ShapeDtypeStruct((n_seq, D_SSM, D_STATE), F32)],
        scratch_shapes=[pltpu.VMEM((D_SSM, D_STATE), F32),
                        pltpu.VMEM((SSD_CHUNKS_PER_STEP, CHUNK, D_XBC), F32)],
        compiler_params=_params(1),
        name="ssd_prompt",
    )(xbc, z, dt_raw, *consts)


def _ssd_sample_body(xbc_ref, pre_ref, z_ref, dt_ref, h0_ref, cw_ref, cb_ref, dtb_ref, alog_ref, dsk_ref,
                     nrm_ref, y_ref, st_ref, xc_s, yoff_s, aend_s):
    t = CHUNK // SAMPLE_SEQS
    row = lax.broadcasted_iota(jnp.int32, (CHUNK, CHUNK), 0)
    col = lax.broadcasted_iota(jnp.int32, (CHUNK, CHUNK), 1)
    same_seq = (row // t) == (col // t)
    seg_mask = same_seq & (row >= col)
    seg_end = jnp.where(col == (row // t) * t + (t - 1), 1.0, 0.0).astype(BF16)
    step = lax.broadcasted_iota(jnp.int32, (CHUNK, LANES), 0) % t
    rows = pl.ds(0, CHUNK)
    heads_per_group = N_SSM_HEADS // SSM_GROUPS
    group_rows = heads_per_group * SSM_HEAD_DIM

    def delayed(k, lanes):
        cur = xbc_ref[:, lanes]
        if k == 0:
            return cur
        return jnp.where(step < k, pltpu.roll(pre_ref[:, lanes], (k - t) % CHUNK, 0), pltpu.roll(cur, k, 0))

    _conv_silu(delayed, cw_ref, cb_ref, xc_s)
    dt_t, a_cum, a_cum_t, a_end, to_end_t = _ssd_scalars(
        dt_ref[...], dtb_ref, alog_ref, seg_mask, seg_end)
    aend_s[...] = a_end
    x_t = xc_s[:, 0:D_SSM].T
    col_seq = lax.broadcasted_iota(jnp.int32, (SSM_HEAD_DIM, CHUNK), 1) // t
    y_parts = []
    for g in range(SSM_GROUPS):
        bg = xc_s[:, D_SSM + g * D_STATE:D_SSM + (g + 1) * D_STATE].astype(BF16)
        cg_lanes = slice(D_SSM + (SSM_GROUPS + g) * D_STATE, D_SSM + (SSM_GROUPS + g + 1) * D_STATE)
        cg = xc_s[:, cg_lanes].astype(BF16)
        cb = lax.dot_general(cg, bg, _NT, preferred_element_type=F32)
        grp = slice(g * group_rows, (g + 1) * group_rows)
        xw = []
        for h in range(g * heads_per_group, (g + 1) * heads_per_group):
            hr = slice(h * SSM_HEAD_DIM, (h + 1) * SSM_HEAD_DIM)
            xw.append(x_t[hr, :] * to_end_t[h:h + 1, :])

        def per_seq(b, carry, g=g, bg=bg, cg_lanes=cg_lanes, grp=grp, xw=xw):
            seq_rows = pl.ds(pl.multiple_of(b * t, t), t)
            h_prev = h0_ref[b, grp, :]
            cg_b = xc_s[seq_rows, cg_lanes].astype(BF16)
            yoff_s[seq_rows, g * group_rows:(g + 1) * group_rows] = lax.dot_general(
                cg_b, h_prev.astype(BF16), _NT, preferred_element_type=F32)
            mine = col_seq == b
            xw_b = jnp.concatenate([jnp.where(mine, w, 0.0).astype(BF16) for w in xw], axis=0)
            new = jnp.dot(xw_b, bg, preferred_element_type=F32)
            a_last = aend_s[pl.ds(b * t, 1), :]
            for i, h in enumerate(range(g * heads_per_group, (g + 1) * heads_per_group)):
                lr = slice(i * SSM_HEAD_DIM, (i + 1) * SSM_HEAD_DIM)
                decay = jnp.exp(jnp.broadcast_to(a_last[:, h:h + 1], (SSM_HEAD_DIM, D_STATE)))
                st_ref[b, h * SSM_HEAD_DIM:(h + 1) * SSM_HEAD_DIM, :] = decay * h_prev[lr, :] + new[lr, :]
            return carry

        lax.fori_loop(0, SAMPLE_SEQS, per_seq, 0)
        for p in range(group_rows // LANES):
            blk = g * (group_rows // LANES) + p
            lanes = slice(blk * LANES, (blk + 1) * LANES)
            y_diag, grow = _ssd_diag(xc_s, blk, cb, a_cum, a_cum_t, dt_t, seg_mask)
            y_parts.append(y_diag + yoff_s[:, lanes] * grow)
    _ssd_finish(xc_s, z_ref, rows, y_parts, dsk_ref, nrm_ref, y_ref)


def _ssd_sample(xbc, prefix_tiles, z, dt_raw, h0, consts):
    rows = xbc.shape[0]
    n_seq = h0.shape[0]
    blk = lambda w: pl.BlockSpec((CHUNK, w), lambda i: (i, 0))
    state_blk = pl.BlockSpec((SAMPLE_SEQS, D_SSM, D_STATE), lambda i: (i, 0, 0))
    return pl.pallas_call(
        _ssd_sample_body,
        grid=(rows // CHUNK,),
        in_specs=[blk(D_XBC), blk(D_XBC), blk(D_SSM), blk(LANES), state_blk]
                 + [_resident(c.shape) for c in consts],
        out_specs=[blk(D_SSM), state_blk],
        out_shape=[jax.ShapeDtypeStruct((rows, D_SSM), BF16),
                   jax.ShapeDtypeStruct((n_seq, D_SSM, D_STATE), F32)],
        scratch_shapes=[pltpu.VMEM((CHUNK, D_XBC), F32), pltpu.VMEM((CHUNK, D_SSM), F32),
                        pltpu.VMEM((CHUNK, LANES), F32)],
        compiler_params=_params(1),
        name="ssd_sample",
    )(xbc, prefix_tiles, z, dt_raw, h0, *consts)


def _out_mlp_body(x_ref, att_ref, ssm_ref, woa_ref, wos_ref, g_ref, wu_ref, wd_ref, gf_ref, o_ref, side_work=()):
    side_work = list(side_work)
    att = jnp.concatenate([att_ref[j] for j in range(att_ref.shape[0])], axis=1).astype(BF16)
    h = (x_ref[...] + jnp.dot(att, woa_ref[...], preferred_element_type=F32)
         + jnp.dot(ssm_ref[...], wos_ref[...], preferred_element_type=F32))
    hn = _rms(h, g_ref[...]).astype(BF16)
    mlp = None
    for c in range(D_FF // FF_TILE):
        if side_work:
            side_work.pop(0)()
        cols = slice(c * FF_TILE, (c + 1) * FF_TILE)
        u = jnp.maximum(jnp.dot(hn, wu_ref[:, cols], preferred_element_type=F32), 0.0)
        down = jnp.dot((u * u).astype(BF16), wd_ref[cols, :], preferred_element_type=F32)
        mlp = down if mlp is None else mlp + down
    for work in side_work:
        work()
    o_ref[...] = _rms(h + mlp, gf_ref[...])


def _out_mlp_attn_body(*refs):
    n_mlp, n_att = 9, 6
    o_ref = refs[n_mlp + n_att]
    pieces = _attn_sample_pieces(*refs[n_mlp:n_mlp + n_att], *refs[n_mlp + n_att + 1:])
    _out_mlp_body(*refs[:n_mlp], o_ref, side_work=pieces)


def _out_mlp(x2d, att4, ssm, w_out_att, w_out_ssm, norm_mlp, w_up, w_down, norm_final, sample=None):
    rows = x2d.shape[0]
    tm = min(ROW_TILE, rows) if sample is None else rows // sample[3].shape[0]
    row_blk = lambda w: pl.BlockSpec((tm, w), lambda i: (i, 0))
    operands = [x2d, att4, ssm, w_out_att, w_out_ssm, norm_mlp, w_up, w_down, norm_final]
    in_specs = [row_blk(D_MODEL), pl.BlockSpec((4, tm, LANES), lambda i: (0, i, 0)), row_blk(D_SSM),
                _resident((D_ATT, D_MODEL)), _resident((D_SSM, D_MODEL)), _resident((1, D_MODEL)),
                _resident((D_MODEL, D_FF)), _resident((D_FF, D_MODEL)), _resident((1, D_MODEL))]
    out_specs = [row_blk(D_MODEL)]
    out_shape = [jax.ShapeDtypeStruct((rows, D_MODEL), F32)]
    body = _out_mlp_body
    if sample is not None:
        att_operands, att_in, att_out, att_shape = _attn_sample_specs(*sample)
        assert rows == tm * sample[3].shape[0] and tm % SUBLANES == 0
        operands += att_operands
        in_specs += att_in
        out_specs += att_out
        out_shape += att_shape
        body = _out_mlp_attn_body
    out = pl.pallas_call(
        body,
        grid=(rows // tm,),
        in_specs=in_specs,
        out_specs=out_specs,
        out_shape=out_shape,
        compiler_params=_params(1),
        name="out_mlp",
    )(*operands)
    return out[0] if sample is None else out


def _q_permutation():
    perm = []
    for pp in range(N_KV_HEADS // 2):
        for g in range(Q_PER_KV):
            for e in range(2):
                base = (2 * pp + e) * Q_PER_KV * HEAD_DIM + g * HEAD_DIM
                perm += list(range(base, base + HEAD_DIM))
    return np.asarray(perm, np.int32)


def _to_lanes(buf):
    n, pos = buf.shape[:2]
    return jnp.transpose(buf, (0, 2, 3, 1)).reshape(n, D_KV, pos)


def _from_lanes(buf_t):
    n, _, pos = buf_t.shape
    return jnp.transpose(buf_t.reshape(n, N_KV_HEADS, HEAD_DIM, pos), (0, 3, 1, 2))[None]


def _pad_lanes(v):
    v = v.reshape(1, -1).astype(F32)
    return jnp.pad(v, ((0, 0), (0, LANES - v.shape[1])))


def kernel(x_prompt, x_sample, cache_k, cache_v, state_conv, state_ssm, w_in, w_out, conv_w, conv_b, dt_bias, a_log, d_skip, ssm_norm, norm_mix, norm_mlp, w_up, w_down, norm_final):
    depth = w_in.shape[0]
    assert depth == 1, "single-layer step"
    n_p, seq, _ = x_prompt.shape
    n_s, t_new, _ = x_sample.shape
    n_past = cache_k.shape[2]
    assert seq == MAX_WINDOW and n_past == MAX_WINDOW and t_new == SUBLANES and n_s % SAMPLE_SEQS == 0

    perm = _q_permutation()
    w = w_in[0]
    w_pad = jnp.concatenate(
        [w[:, perm], w[:, D_ATT:], jnp.zeros((D_MODEL, D_IN_PAD - w.shape[1]), F32)], axis=1).astype(BF16)
    w_out_att = w_out[0][perm, :].astype(BF16)
    w_out_ssm = w_out[0][D_ATT:, :].astype(BF16)
    w_up_b, w_down_b = w_up[0].astype(BF16), w_down[0].astype(BF16)
    g_mix, g_mlp, g_fin = (v.reshape(1, D_MODEL) for v in (norm_mix[0], norm_mlp[0], norm_final))
    ssd_consts = (conv_w[0], conv_b[0].reshape(1, D_XBC), _pad_lanes(dt_bias[0]), _pad_lanes(a_log[0]),
                  jnp.repeat(d_skip[0], SSM_HEAD_DIM).reshape(1, D_SSM), ssm_norm[0].reshape(1, D_SSM))

    xp = x_prompt.reshape(n_p * seq, D_MODEL)
    q4, k, v, z, xbc, dt_raw, k_t, v_t = _in_proj(xp, g_mix, w_pad, _rope_table(jnp.arange(seq)), seq=seq)
    att4 = _attn_prompt(q4, k, v, n_p, seq)
    y_ssm, st_p = _ssd_prompt(xbc, z, dt_raw, n_p, seq, ssd_consts)
    k_prompt, v_prompt = _from_lanes(k_t), _from_lanes(v_t)
    conv_prompt = xbc.reshape(1, n_p, seq, D_XBC)[:, :, seq - (CONV_WIDTH - 1):]

    xs = x_sample.reshape(n_s * t_new, D_MODEL)
    rows_tile = min(ROW_TILE, n_s * t_new)
    pos = PAST_LEN + (jnp.arange(rows_tile) % t_new)
    q4s, ks, vs, zs, xbcs, dts = _in_proj(xs, g_mix, w_pad, _rope_table(pos))
    y_prompt, att4s, k_sample, v_sample = _out_mlp(
        xp, att4, y_ssm, w_out_att, w_out_ssm, g_mlp, w_up_b, w_down_b, g_fin,
        sample=(q4s, ks, vs, _to_lanes(cache_k[0]), _to_lanes(cache_v[0])))
    prefix_tiles = jnp.pad(state_conv[0], ((0, 0), (t_new - (CONV_WIDTH - 1), 0), (0, 0))).reshape(n_s * t_new, D_XBC)
    y_ssm_s, st_s = _ssd_sample(xbcs, prefix_tiles, zs, dts, state_ssm[0].reshape(n_s, D_SSM, D_STATE), ssd_consts)
    y_sample = _out_mlp(xs, att4s, y_ssm_s, w_out_att, w_out_ssm, g_mlp, w_up_b, w_down_b, g_fin)
    conv_sample = xbcs.reshape(1, n_s, t_new, D_XBC)[:, :, t_new - (CONV_WIDTH - 1):]

    return (y_prompt.reshape(n_p, seq, D_MODEL), y_sample.reshape(n_s, t_new, D_MODEL),
            k_prompt, v_prompt, conv_prompt,
            st_p.reshape(1, n_p, N_SSM_HEADS, SSM_HEAD_DIM, D_STATE),
            _from_lanes(k_sample), _from_lanes(v_sample), conv_sample,
            st_s.reshape(1, n_s, N_SSM_HEADS, SSM_HEAD_DIM, D_STATE))
```

```python
import functools

import numpy as np
import jax
import jax.numpy as jnp
from jax import lax
from jax.experimental import pallas as pl
from jax.experimental.pallas import tpu as pltpu

F32 = jnp.float32
BF16 = jnp.bfloat16

LANES = 128
SUBLANES = 8

D_MODEL = 1024
HEAD_DIM = 64
N_KV_HEADS = 4
Q_PER_KV = 2
D_ATT = 512
D_KV = 256
ROT_HALF = 8
ROPE_THETA = 500000.0
DILATIONS = (1, 4, 16)
N_KEYS = 128
MAX_WINDOW = 2048
ATT_BLK = 128
ATT_UNITS = 8
REGROUP_STRIDE = 4
RESIDUES = 16
PAST_LEN = 8192
ATT_SCALE = HEAD_DIM ** -0.5
NEG_BIG = -1e30
N_SSM_HEADS = 8
SSM_HEAD_DIM = 64
D_SSM = 512
SSM_GROUPS = 2
D_STATE = 128
CONV_WIDTH = 4
CHUNK = 128
D_XBC = 1024
D_FF = 4096
RMS_EPS = 1e-5
D_IN_MAIN = D_ATT + 2 * D_KV + D_SSM + D_XBC
D_IN_PAD = D_IN_MAIN + LANES

VMEM_LIMIT = 56 * 1024 * 1024
ROW_TILE = 512
FF_TILE = 1024
SSD_CHUNKS_PER_STEP = 4
SAMPLE_SEQS = CHUNK // 8

_NT = (((1,), (1,)), ((), ()))


def _params(n_axes):
    return pltpu.CompilerParams(dimension_semantics=("arbitrary",) * n_axes,
                                vmem_limit_bytes=VMEM_LIMIT)


def _resident(shape):
    return pl.BlockSpec(shape, lambda *_: (0,) * len(shape), pipeline_mode=pl.Buffered(1))


def _rms(x, g):
    return x * lax.rsqrt(jnp.mean(x * x, axis=-1, keepdims=True) + RMS_EPS) * g


def _silu(x):
    half = 0.5 * x
    return half + half * jnp.tanh(half)


def _softplus(x):
    return jnp.maximum(x, 0.0) + jnp.log1p(jnp.exp(-jnp.abs(x)))


def _in_proj_body(x_ref, g_ref, w_ref, rope_ref, q_ref, k_ref, v_ref, z_ref, xbc_ref, dt_ref, *kv_t_refs):
    xb = _rms(x_ref[...], g_ref[...]).astype(BF16)
    tm = xb.shape[0]
    first_half = (lax.broadcasted_iota(jnp.int32, (tm, LANES), 1) % HEAD_DIM) < ROT_HALF

    def proj(lo, hi):
        return jnp.dot(xb, w_ref[:, lo:hi], preferred_element_type=F32)

    def rope(u, cos, sin):
        partner = jnp.where(first_half, pltpu.roll(u, LANES - ROT_HALF, 1), pltpu.roll(u, ROT_HALF, 1))
        return u * cos + partner * sin

    cq, sq = rope_ref[:, 0:128], rope_ref[:, 128:256]
    ck, sk = rope_ref[:, 256:384], rope_ref[:, 384:512]
    q = proj(0, D_ATT)
    for j in range(D_ATT // LANES):
        q_ref[j] = rope(q[:, j * LANES:(j + 1) * LANES], cq, sq)
    k = proj(D_ATT, D_ATT + D_KV)
    v = proj(D_ATT + D_KV, D_ATT + 2 * D_KV)
    v_ref[...] = v
    for j in range(D_KV // LANES):
        lanes = slice(j * LANES, (j + 1) * LANES)
        kj = rope(k[:, lanes], ck, sk)
        k_ref[:, lanes] = kj
        if kv_t_refs:
            kv_t_refs[0][lanes, :] = kj.T
            kv_t_refs[1][lanes, :] = v[:, lanes].T
    z_ref[...] = proj(D_ATT + 2 * D_KV, D_ATT + 2 * D_KV + D_SSM)
    xbc_ref[...] = proj(D_ATT + 2 * D_KV + D_SSM, D_IN_MAIN)
    dt_ref[...] = proj(D_IN_MAIN, D_IN_PAD)


def _in_proj(x2d, norm_mix, w_pad, rope_tab, seq=None):
    rows = x2d.shape[0]
    tm = min(ROW_TILE, rows)
    n_tab = rope_tab.shape[0] // tm
    row_blk = lambda w: pl.BlockSpec((tm, w), lambda i: (i, 0))
    out_specs = [pl.BlockSpec((4, tm, LANES), lambda i: (0, i, 0)),
                 row_blk(D_KV), row_blk(D_KV), row_blk(D_SSM), row_blk(D_XBC), row_blk(LANES)]
    out_shape = [jax.ShapeDtypeStruct((4, rows, LANES), F32),
                 jax.ShapeDtypeStruct((rows, D_KV), F32), jax.ShapeDtypeStruct((rows, D_KV), F32),
                 jax.ShapeDtypeStruct((rows, D_SSM), F32), jax.ShapeDtypeStruct((rows, D_XBC), F32),
                 jax.ShapeDtypeStruct((rows, LANES), F32)]
    if seq is not None:
        per_seq = seq // tm
        t_blk = pl.BlockSpec((None, D_KV, tm), lambda i: (i // per_seq, 0, i % per_seq))
        out_specs += [t_blk, t_blk]
        out_shape += [jax.ShapeDtypeStruct((rows // seq, D_KV, seq), F32)] * 2
    return pl.pallas_call(
        _in_proj_body,
        grid=(rows // tm,),
        in_specs=[row_blk(D_MODEL), _resident((1, D_MODEL)), _resident((D_MODEL, D_IN_PAD)),
                  pl.BlockSpec((tm, 4 * LANES), lambda i: (i % n_tab, 0))],
        out_specs=out_specs,
        out_shape=out_shape,
        compiler_params=_params(1),
        name="in_proj",
    )(x2d, norm_mix, w_pad, rope_tab)


def _rope_table(pos):
    inv = ROPE_THETA ** (-jnp.arange(0, 2 * ROT_HALF, 2, dtype=F32) / (2 * ROT_HALF))
    ang = pos.astype(F32)[:, None] * inv[None, :]
    cos, sin = jnp.cos(ang), jnp.sin(ang)
    n = pos.shape[0]
    ones = jnp.ones((n, HEAD_DIM - 2 * ROT_HALF), F32)
    cos_h = jnp.concatenate([cos, cos, ones], axis=1)
    sin_h = jnp.concatenate([-sin, sin, 0.0 * ones], axis=1)
    cos_l, sin_l = jnp.tile(cos_h, (1, 2)), jnp.tile(sin_h, (1, 2))
    return jnp.concatenate([cos_l * ATT_SCALE, sin_l * ATT_SCALE, cos_l, sin_l], axis=1)


def _attn_prompt_body(q_ref, k_ref, v_ref, bias1_ref, bias4_ref, caus_ref, o_ref,
                      q3, k3, v3, m3, l3, acc3, nat_s):
    seq = k_ref.shape[0]
    per_res = seq // RESIDUES
    e0 = lax.broadcasted_iota(jnp.int32, (ATT_BLK, LANES), 1) < HEAD_DIM
    zero = jnp.zeros((ATT_BLK, LANES), F32)

    def regroup(src, tmp, dst):
        quarter = seq // REGROUP_STRIDE
        for r in range(REGROUP_STRIDE):
            tmp[r * quarter:(r + 1) * quarter, :] = src[pl.ds(r, quarter, stride=REGROUP_STRIDE), :]
        for r in range(REGROUP_STRIDE):
            for a in range(RESIDUES // REGROUP_STRIDE):
                dst[r + REGROUP_STRIDE * a] = tmp[pl.ds(r * quarter + a, per_res, stride=REGROUP_STRIDE), :]

    for g in range(Q_PER_KV):
        regroup(q_ref.at[g], nat_s.at[g], q3.at[g])
    regroup(k_ref, nat_s.at[Q_PER_KV], k3)
    regroup(v_ref, nat_s.at[Q_PER_KV + 1], v3)

    def scores(q_idx, k_idx, bias):
        parts = []
        for g in range(Q_PER_KV):
            qg = q3.at[g][q_idx].reshape(ATT_BLK, LANES)
            parts += [jnp.where(e0, qg, zero), jnp.where(e0, zero, qg)]
        lhs = jnp.concatenate(parts, axis=0).astype(BF16)
        kb = k3[k_idx]
        nk = kb.shape[0] * kb.shape[1]
        kb = kb.reshape(nk, LANES).astype(BF16)
        vb = v3[k_idx].reshape(nk, LANES).astype(BF16)
        s = lax.dot_general(lhs, kb, _NT, preferred_element_type=F32)
        s = (s.reshape(4, ATT_BLK, nk) + bias[None]).reshape(4 * ATT_BLK, nk)
        m = jnp.max(s, axis=1, keepdims=True)
        p = jnp.exp(s - m)
        l = jnp.sum(p, axis=1, keepdims=True)
        pv = jnp.dot(p.astype(BF16), vb, preferred_element_type=F32)
        mb = jnp.broadcast_to(m, pv.shape)
        lb = jnp.broadcast_to(l, pv.shape)
        out = []
        for g in range(Q_PER_KV):
            r0, r1, r2 = 2 * g * ATT_BLK, (2 * g + 1) * ATT_BLK, (2 * g + 2) * ATT_BLK
            out.append((jnp.where(e0, mb[r0:r1], mb[r1:r2]), jnp.where(e0, lb[r0:r1], lb[r1:r2]),
                        jnp.where(e0, pv[r0:r1], pv[r1:r2])))
        return out

    def process(units, first_group):
        state = [(m3.at[g], l3.at[g], acc3.at[g]) for g in range(Q_PER_KV)]
        new = [scores(*unit) for unit in units]
        for i, (q_idx, _, _) in enumerate(units):
            for g in range(Q_PER_KV):
                mg, lg, acc = new[i][g]
                if not first_group:
                    m_old, l_old, acc_old = (ref[q_idx].reshape(ATT_BLK, LANES) for ref in state[g])
                    m_new = jnp.maximum(m_old, mg)
                    a_old = jnp.exp(m_old - m_new)
                    a_new = jnp.exp(mg - m_new)
                    mg, lg, acc = m_new, a_old * l_old + a_new * lg, a_old * acc_old + a_new * acc
                for ref, val in zip(state[g], (mg, lg, acc)):
                    ref[q_idx] = val.reshape(ref[q_idx].shape)

    for gi, (d, bias_ref) in enumerate(zip(DILATIONS, (bias1_ref, bias4_ref, caus_ref))):
        blocks = seq // d // ATT_BLK
        lanes16 = RESIDUES // d
        depth = ATT_BLK // lanes16

        def unit(u, d=d, blocks=blocks, lanes16=lanes16, depth=depth, bias_ref=bias_ref):
            r, n = u // blocks, u % blocks
            lead = pl.ds(r, lanes16, stride=d)
            q_idx = (lead, pl.ds(pl.multiple_of(n * depth, depth), depth))
            if blocks == 1:
                return q_idx, q_idx, bias_ref[...]
            k_idx = (lead, pl.ds(pl.multiple_of(jnp.maximum(n - 1, 0) * depth, depth), 2 * depth))
            return q_idx, k_idx, bias_ref[jnp.minimum(n, 1)]

        def step(i, carry, unit=unit, first_group=(gi == 0)):
            process([unit(ATT_UNITS * i + j) for j in range(ATT_UNITS)], first_group)
            return carry

        lax.fori_loop(0, d * blocks // ATT_UNITS, step, 0)

    quarter = seq // REGROUP_STRIDE
    for g in range(Q_PER_KV):
        tmp, out = nat_s.at[Q_PER_KV + g], nat_s.at[g]
        for r in range(REGROUP_STRIDE):
            for a in range(RESIDUES // REGROUP_STRIDE):
                res = r + REGROUP_STRIDE * a
                tmp[pl.ds(r * quarter + a, per_res, stride=REGROUP_STRIDE), :] = acc3[g, res] / l3[g, res]
        for r in range(REGROUP_STRIDE):
            out[pl.ds(r, quarter, stride=REGROUP_STRIDE), :] = tmp[r * quarter:(r + 1) * quarter, :]
        o_ref[g] = out[...].astype(o_ref.dtype)


def _window_bias(d):
    lanes16 = RESIDUES // d

    def order(n):
        a, ll = np.divmod(np.arange(n), n // lanes16)
        return lanes16 * ll + a

    i = order(ATT_BLK)[:, None]
    if d == RESIDUES:
        return jnp.asarray(np.where(order(ATT_BLK)[None, :] <= i, 0.0, NEG_BIG).astype(np.float32))
    j = order(2 * ATT_BLK)[None, :]
    dist = i + ATT_BLK - j
    inner = np.where((dist >= 0) & (dist <= N_KEYS), 0.0, NEG_BIG)
    first = np.where(j <= i, 0.0, NEG_BIG)
    return jnp.asarray(np.stack([first, inner]).astype(np.float32))


def _attn_prompt(q4, k, v, n_seq, seq):
    biases = [_window_bias(d) for d in DILATIONS]
    slab = (RESIDUES, seq // RESIDUES, LANES)
    return pl.pallas_call(
        _attn_prompt_body,
        grid=(n_seq, 2),
        in_specs=[pl.BlockSpec((2, seq, LANES), lambda b, pp: (pp, b, 0)),
                  pl.BlockSpec((seq, LANES), lambda b, pp: (b, pp)),
                  pl.BlockSpec((seq, LANES), lambda b, pp: (b, pp))] + [_resident(x.shape) for x in biases],
        out_specs=pl.BlockSpec((2, seq, LANES), lambda b, pp: (pp, b, 0)),
        out_shape=jax.ShapeDtypeStruct(q4.shape, BF16),
        scratch_shapes=[pltpu.VMEM((2,) + slab, F32), pltpu.VMEM(slab, F32), pltpu.VMEM(slab, F32)]
                       + [pltpu.VMEM((2,) + slab, F32)] * 3 + [pltpu.VMEM((Q_PER_KV + 2, seq, LANES), F32)],
        compiler_params=_params(2),
        name="attn_prompt",
    )(q4, k, v, *biases)


def _attn_sample_pieces(q_ref, kn_ref, vn_ref, ck_ref, cv_ref, mult_ref, att_ref, ko_ref, vo_ref):
    t = kn_ref.shape[0]
    n_past = ck_ref.shape[2]
    tail = slice(n_past - LANES, n_past)

    def shift(c_ref, n_ref, o_ref, rows):
        is_new = lax.broadcasted_iota(jnp.int32, (LANES, LANES), 1) >= LANES - t
        moved = pltpu.roll(c_ref[0, rows, :], n_past - t, 1)
        o_ref[0, rows, :] = moved
        new_rows = jnp.concatenate([n_ref[:, rows], jnp.zeros((LANES - t, LANES), F32)], axis=0)
        o_ref[0, rows, tail] = jnp.where(is_new, pltpu.roll(new_rows.T, LANES - t, 1), moved[:, tail])

    def attend():
        mult = mult_ref[...]
        e0 = lax.broadcasted_iota(jnp.int32, (t, LANES), 1) < HEAD_DIM
        zero = jnp.zeros((t, LANES), F32)
        pairs = N_KV_HEADS // 2
        parts = []
        for pp in range(pairs):
            for g in range(Q_PER_KV):
                qg = q_ref[pp * Q_PER_KV + g]
                for masked in (jnp.where(e0, qg, zero), jnp.where(e0, zero, qg)):
                    parts.append(jnp.concatenate([masked if j == pp else zero for j in range(pairs)], axis=1))
        lhs = jnp.concatenate(parts, axis=0).astype(BF16)
        pad = jnp.zeros((LANES - t, D_KV), F32)
        kn = jnp.concatenate([kn_ref[...], pad], axis=0).astype(BF16)
        vn = jnp.concatenate([vn_ref[...], pad], axis=0).astype(BF16)
        s = jnp.concatenate([jnp.dot(lhs, ck_ref[0].astype(BF16), preferred_element_type=F32),
                             lax.dot_general(lhs, kn, _NT, preferred_element_type=F32)], axis=1)
        s = jnp.where(mult > 0.0, s, NEG_BIG)
        m = jnp.max(s, axis=1, keepdims=True)
        p = jnp.exp(s - m) * mult
        l = jnp.sum(p, axis=1, keepdims=True)
        pb = p.astype(BF16)
        pv = (lax.dot_general(pb[:, :n_past], cv_ref[0].astype(BF16), _NT, preferred_element_type=F32)
              + jnp.dot(pb[:, n_past:], vn, preferred_element_type=F32))
        o = pv / l
        for pp in range(pairs):
            lanes = slice(pp * LANES, (pp + 1) * LANES)
            for g in range(Q_PER_KV):
                r0 = (pp * Q_PER_KV + g) * 2 * t
                att_ref[pp * Q_PER_KV + g] = jnp.where(e0, o[r0:r0 + t, lanes], o[r0 + t:r0 + 2 * t, lanes])

    pieces = [attend]
    for pp in range(N_KV_HEADS // 2):
        rows = slice(pp * LANES, (pp + 1) * LANES)
        pieces += [functools.partial(shift, ck_ref, kn_ref, ko_ref, rows),
                   functools.partial(shift, cv_ref, vn_ref, vo_ref, rows)]
    return pieces


def _key_multiplicity(t, n_past):
    idx = np.concatenate([np.arange(n_past + t), np.full(LANES - t, 10 ** 9)])[None, :]
    dist = n_past + np.arange(t)[:, None] - idx
    mult = np.zeros(dist.shape, np.float32)
    for d in DILATIONS:
        mult += (dist >= 0) & (dist % d == 0) & (dist <= N_KEYS * d)
    return jnp.asarray(np.tile(mult, (2 * N_KV_HEADS, 1)))


def _attn_sample_specs(q4, k_new, v_new, cache_k, cache_v):
    n_seq, _, n_past = cache_k.shape
    t = k_new.shape[0] // n_seq
    mult = _key_multiplicity(t, n_past)
    cache_blk = pl.BlockSpec((1, D_KV, n_past), lambda b: (b, 0, 0))
    new_blk = pl.BlockSpec((t, D_KV), lambda b: (b, 0))
    q_blk = pl.BlockSpec((4, t, LANES), lambda b: (0, b, 0))
    return ([q4, k_new, v_new, cache_k, cache_v, mult],
            [q_blk, new_blk, new_blk, cache_blk, cache_blk, _resident(mult.shape)],
            [q_blk, cache_blk, cache_blk],
            [jax.ShapeDtypeStruct(q4.shape, F32),
             jax.ShapeDtypeStruct(cache_k.shape, F32), jax.ShapeDtypeStruct(cache_v.shape, F32)])


def _conv_silu(delayed, cw_ref, cb_ref, xc_s):
    for j in range(D_XBC // LANES):
        lanes = slice(j * LANES, (j + 1) * LANES)
        y = cb_ref[:, lanes]
        for tap in range(CONV_WIDTH):
            y = y + delayed(CONV_WIDTH - 1 - tap, lanes) * cw_ref[tap:tap + 1, lanes]
        xc_s[:, lanes] = _silu(y)


def _select_sum(sel, x):
    total = None
    for _ in range(3):
        piece = x.astype(BF16)
        x = x - piece.astype(F32)
        part = jnp.dot(sel, piece, preferred_element_type=F32)
        total = part if total is None else total + part
    return total


def _ssd_scalars(dt_raw, dtb_ref, alog_ref, seg_mask, seg_end=None):
    dt = _softplus(dt_raw + dtb_ref[...])
    a = dt * (-jnp.exp(alog_ref[...]))
    a_cum = _select_sum(jnp.where(seg_mask, 1.0, 0.0).astype(BF16), a)
    if seg_end is None:
        a_end = jnp.broadcast_to(a_cum[CHUNK - 1:CHUNK, :], a_cum.shape)
    else:
        a_end = _select_sum(seg_end, a_cum)
    to_end = jnp.exp(a_end - a_cum) * dt
    return dt.T, a_cum, a_cum.T, a_end, to_end.T


def _ssd_diag(xc_s, blk, cb, a_cum, a_cum_t, dt_t, seg_mask):
    e0 = lax.broadcasted_iota(jnp.int32, (CHUNK, LANES), 1) < SSM_HEAD_DIM
    x_pair = xc_s[:, blk * LANES:(blk + 1) * LANES].astype(BF16)
    out, grow = [], []
    for e in range(2):
        h = 2 * blk + e
        a_col = jnp.broadcast_to(a_cum[:, h:h + 1], (CHUNK, CHUNK))
        seg = a_col - jnp.broadcast_to(a_cum_t[h:h + 1, :], (CHUNK, CHUNK))
        w = cb * jnp.where(seg_mask, jnp.exp(seg), 0.0) * dt_t[h:h + 1, :]
        out.append(jnp.dot(w.astype(BF16), x_pair, preferred_element_type=F32))
        grow.append(jnp.exp(a_col))
    return jnp.where(e0, out[0], out[1]), jnp.where(e0, grow[0], grow[1])


def _ssd_finish(xc_s, z_ref, rows, y_parts, dsk_ref, nrm_ref, y_ref):
    blocks_per_group = D_SSM // SSM_GROUPS // LANES
    for g in range(SSM_GROUPS):
        gated = []
        for p in range(blocks_per_group):
            blk = g * blocks_per_group + p
            lanes = slice(blk * LANES, (blk + 1) * LANES)
            y = y_parts[blk] + dsk_ref[:, lanes] * xc_s[:, lanes]
            gated.append(y * _silu(z_ref[rows, lanes]))
        ss = sum(jnp.sum(y * y, axis=-1, keepdims=True) for y in gated)
        inv = lax.rsqrt(ss / (D_SSM // SSM_GROUPS) + RMS_EPS)
        for p in range(blocks_per_group):
            blk = g * blocks_per_group + p
            lanes = slice(blk * LANES, (blk + 1) * LANES)
            y_ref[rows, lanes] = (gated[p] * inv * nrm_ref[:, lanes]).astype(y_ref.dtype)


def _ssd_prompt_body(xbc_ref, z_ref, dt_ref, cw_ref, cb_ref, dtb_ref, alog_ref, dsk_ref, nrm_ref,
                     y_ref, st_ref, st_s, xc_bufs):
    seq = xbc_ref.shape[0]
    st_s[...] = jnp.zeros_like(st_s)
    row = lax.broadcasted_iota(jnp.int32, (CHUNK, CHUNK), 0)
    col = lax.broadcasted_iota(jnp.int32, (CHUNK, CHUNK), 1)
    causal = row >= col
    row8 = lax.broadcasted_iota(jnp.int32, (SUBLANES, LANES), 0)
    heads_per_group = N_SSM_HEADS // SSM_GROUPS
    group_rows = heads_per_group * SSM_HEAD_DIM

    def chunk(c, xc_s):
        rows = pl.ds(pl.multiple_of(c * CHUNK, CHUNK), CHUNK)
        prev_rows = pl.ds(pl.multiple_of(jnp.maximum(c * CHUNK - SUBLANES, 0), SUBLANES), SUBLANES)
        have_prev = c > 0

        def delayed(k, lanes):
            cur = xbc_ref[rows, lanes]
            if k == 0:
                return cur
            rolled = pltpu.roll(cur, k, 0)
            prev = jnp.where(have_prev, xbc_ref[prev_rows, lanes], 0.0)
            head = jnp.where(row8 < k, pltpu.roll(prev, k, 0), rolled[0:SUBLANES])
            return jnp.concatenate([head, rolled[SUBLANES:]], axis=0)

        _conv_silu(delayed, cw_ref, cb_ref, xc_s)
        dt_t, a_cum, a_cum_t, a_end, to_end_t = _ssd_scalars(dt_ref[rows, :], dtb_ref, alog_ref, causal)
        x_t = xc_s[:, 0:D_SSM].T
        y_parts = []
        for g in range(SSM_GROUPS):
            bg = xc_s[:, D_SSM + g * D_STATE:D_SSM + (g + 1) * D_STATE].astype(BF16)
            cg = xc_s[:, D_SSM + (SSM_GROUPS + g) * D_STATE:D_SSM + (SSM_GROUPS + g + 1) * D_STATE].astype(BF16)
            cb = lax.dot_general(cg, bg, _NT, preferred_element_type=F32)
            grp = slice(g * group_rows, (g + 1) * group_rows)
            st_prev = st_s[grp, :]
            y_off = lax.dot_general(cg, st_prev.astype(BF16), _NT, preferred_element_type=F32)
            for p in range(group_rows // LANES):
                blk = g * (group_rows // LANES) + p
                y_diag, grow = _ssd_diag(xc_s, blk, cb, a_cum, a_cum_t, dt_t, causal)
                y_parts.append(y_diag + y_off[:, p * LANES:(p + 1) * LANES] * grow)
            xw = []
            for h in range(g * heads_per_group, (g + 1) * heads_per_group):
                hr = slice(h * SSM_HEAD_DIM, (h + 1) * SSM_HEAD_DIM)
                xw.append((x_t[hr, :] * to_end_t[h:h + 1, :]).astype(BF16))
            new = jnp.dot(jnp.concatenate(xw, axis=0), bg, preferred_element_type=F32)
            for i, h in enumerate(range(g * heads_per_group, (g + 1) * heads_per_group)):
                hr = slice(h * SSM_HEAD_DIM, (h + 1) * SSM_HEAD_DIM)
                lr = slice(i * SSM_HEAD_DIM, (i + 1) * SSM_HEAD_DIM)
                decay = jnp.exp(jnp.broadcast_to(a_end[0:1, h:h + 1], (SSM_HEAD_DIM, D_STATE)))
                st_s[hr, :] = decay * st_prev[lr, :] + new[lr, :]
        _ssd_finish(xc_s, z_ref, rows, y_parts, dsk_ref, nrm_ref, y_ref)

    def step(i, carry):
        for j in range(SSD_CHUNKS_PER_STEP):
            chunk(SSD_CHUNKS_PER_STEP * i + j, xc_bufs.at[j])
        return carry

    lax.fori_loop(0, seq // CHUNK // SSD_CHUNKS_PER_STEP, step, 0)
    st_ref[...] = st_s[...]


def _ssd_prompt(xbc, z, dt_raw, n_seq, seq, consts):
    rows = xbc.shape[0]
    blk = lambda w: pl.BlockSpec((seq, w), lambda b: (b, 0))
    state_blk = pl.BlockSpec((None, D_SSM, D_STATE), lambda b: (b, 0, 0))
    return pl.pallas_call(
        _ssd_prompt_body,
        grid=(n_seq,),
        in_specs=[blk(D_XBC), blk(D_SSM), blk(LANES)] + [_resident(c.shape) for c in consts],
        out_specs=[blk(D_SSM), state_blk],
        out_shape=[jax.ShapeDtypeStruct((rows, D_SSM), BF16),
                   jax.ShapeDtypeStruct((n_seq, D_SSM, D_STATE), F32)],
        scratch_shapes=[pltpu.VMEM((D_SSM, D_STATE), F32),
                        pltpu.VMEM((SSD_CHUNKS_PER_STEP, CHUNK, D_XBC), F32)],
        compiler_params=_params(1),
        name="ssd_prompt",
    )(xbc, z, dt_raw, *consts)


def _ssd_sample_body(xbc_ref, pre_ref, z_ref, dt_ref, h0_ref, cw_ref, cb_ref, dtb_ref, alog_ref, dsk_ref,
                     nrm_ref, y_ref, st_ref, xc_s, yoff_s, aend_s):
    t = CHUNK // SAMPLE_SEQS
    row = lax.broadcasted_iota(jnp.int32, (CHUNK, CHUNK), 0)
    col = lax.broadcasted_iota(jnp.int32, (CHUNK, CHUNK), 1)
    same_seq = (row // t) == (col // t)
    seg_mask = same_seq & (row >= col)
    seg_end = jnp.where(col == (row // t) * t + (t - 1), 1.0, 0.0).astype(BF16)
    step = lax.broadcasted_iota(jnp.int32, (CHUNK, LANES), 0) % t
    rows = pl.ds(0, CHUNK)
    heads_per_group = N_SSM_HEADS // SSM_GROUPS
    group_rows = heads_per_group * SSM_HEAD_DIM

    def delayed(k, lanes):
        cur = xbc_ref[:, lanes]
        if k == 0:
            return cur
        return jnp.where(step < k, pltpu.roll(pre_ref[:, lanes], (k - t) % CHUNK, 0), pltpu.roll(cur, k, 0))

    _conv_silu(delayed, cw_ref, cb_ref, xc_s)
    dt_t, a_cum, a_cum_t, a_end, to_end_t = _ssd_scalars(
        dt_ref[...], dtb_ref, alog_ref, seg_mask, seg_end)
    aend_s[...] = a_end
    x_t = xc_s[:, 0:D_SSM].T
    col_seq = lax.broadcasted_iota(jnp.int32, (SSM_HEAD_DIM, CHUNK), 1) // t
    y_parts = []
    for g in range(SSM_GROUPS):
        bg = xc_s[:, D_SSM + g * D_STATE:D_SSM + (g + 1) * D_STATE].astype(BF16)
        cg_lanes = slice(D_SSM + (SSM_GROUPS + g) * D_STATE, D_SSM + (SSM_GROUPS + g + 1) * D_STATE)
        cg = xc_s[:, cg_lanes].astype(BF16)
        cb = lax.dot_general(cg, bg, _NT, preferred_element_type=F32)
        grp = slice(g * group_rows, (g + 1) * group_rows)
        xw = []
        for h in range(g * heads_per_group, (g + 1) * heads_per_group):
            hr = slice(h * SSM_HEAD_DIM, (h + 1) * SSM_HEAD_DIM)
            xw.append(x_t[hr, :] * to_end_t[h:h + 1, :])

        def per_seq(b, carry, g=g, bg=bg, cg_lanes=cg_lanes, grp=grp, xw=xw):
            seq_rows = pl.ds(pl.multiple_of(b * t, t), t)
            h_prev = h0_ref[b, grp, :]
            cg_b = xc_s[seq_rows, cg_lanes].astype(BF16)
            yoff_s[seq_rows, g * group_rows:(g + 1) * group_rows] = lax.dot_general(
                cg_b, h_prev.astype(BF16), _NT, preferred_element_type=F32)
            mine = col_seq == b
            xw_b = jnp.concatenate([jnp.where(mine, w, 0.0).astype(BF16) for w in xw], axis=0)
            new = jnp.dot(xw_b, bg, preferred_element_type=F32)
            a_last = aend_s[pl.ds(b * t, 1), :]
            for i, h in enumerate(range(g * heads_per_group, (g + 1) * heads_per_group)):
                lr = slice(i * SSM_HEAD_DIM, (i + 1) * SSM_HEAD_DIM)
                decay = jnp.exp(jnp.broadcast_to(a_last[:, h:h + 1], (SSM_HEAD_DIM, D_STATE)))
                st_ref[b, h * SSM_HEAD_DIM:(h + 1) * SSM_HEAD_DIM, :] = decay * h_prev[lr, :] + new[lr, :]
            return carry

        lax.fori_loop(0, SAMPLE_SEQS, per_seq, 0)
        for p in range(group_rows // LANES):
            blk = g * (group_rows // LANES) + p
            lanes = slice(blk * LANES, (blk + 1) * LANES)
            y_diag, grow = _ssd_diag(xc_s, blk, cb, a_cum, a_cum_t, dt_t, seg_mask)
            y_parts.append(y_diag + yoff_s[:, lanes] * grow)
    _ssd_finish(xc_s, z_ref, rows, y_parts, dsk_ref, nrm_ref, y_ref)


def _ssd_sample(xbc, prefix_tiles, z, dt_raw, h0, consts):
    rows = xbc.shape[0]
    n_seq = h0.shape[0]
    blk = lambda w: pl.BlockSpec((CHUNK, w), lambda i: (i, 0))
    state_blk = pl.BlockSpec((SAMPLE_SEQS, D_SSM, D_STATE), lambda i: (i, 0, 0))
    return pl.pallas_call(
        _ssd_sample_body,
        grid=(rows // CHUNK,),
        in_specs=[blk(D_XBC), blk(D_XBC), blk(D_SSM), blk(LANES), state_blk]
                 + [_resident(c.shape) for c in consts],
        out_specs=[blk(D_SSM), state_blk],
        out_shape=[jax.ShapeDtypeStruct((rows, D_SSM), BF16),
                   jax.ShapeDtypeStruct((n_seq, D_SSM, D_STATE), F32)],
        scratch_shapes=[pltpu.VMEM((CHUNK, D_XBC), F32), pltpu.VMEM((CHUNK, D_SSM), F32),
                        pltpu.VMEM((CHUNK, LANES), F32)],
        compiler_params=_params(1),
        name="ssd_sample",
    )(xbc, prefix_tiles, z, dt_raw, h0, *consts)


def _out_mlp_body(x_ref, att_ref, ssm_ref, woa_ref, wos_ref, g_ref, wu_ref, wd_ref, gf_ref, o_ref, side_work=()):
    side_work = list(side_work)
    att = jnp.concatenate([att_ref[j] for j in range(att_ref.shape[0])], axis=1).astype(BF16)
    h = (x_ref[...] + jnp.dot(att, woa_ref[...], preferred_element_type=F32)
         + jnp.dot(ssm_ref[...], wos_ref[...], preferred_element_type=F32))
    hn = _rms(h, g_ref[...]).astype(BF16)
    mlp = None
    for c in range(D_FF // FF_TILE):
        if side_work:
            side_work.pop(0)()
        cols = slice(c * FF_TILE, (c + 1) * FF_TILE)
        u = jnp.maximum(jnp.dot(hn, wu_ref[:, cols], preferred_element_type=F32), 0.0)
        down = jnp.dot((u * u).astype(BF16), wd_ref[cols, :], preferred_element_type=F32)
        mlp = down if mlp is None else mlp + down
    for work in side_work:
        work()
    o_ref[...] = _rms(h + mlp, gf_ref[...])


def _out_mlp_attn_body(*refs):
    n_mlp, n_att = 9, 6
    o_ref = refs[n_mlp + n_att]
    pieces = _attn_sample_pieces(*refs[n_mlp:n_mlp + n_att], *refs[n_mlp + n_att + 1:])
    _out_mlp_body(*refs[:n_mlp], o_ref, side_work=pieces)


def _out_mlp(x2d, att4, ssm, w_out_att, w_out_ssm, norm_mlp, w_up, w_down, norm_final, sample=None):
    rows = x2d.shape[0]
    tm = min(ROW_TILE, rows) if sample is None else rows // sample[3].shape[0]
    row_blk = lambda w: pl.BlockSpec((tm, w), lambda i: (i, 0))
    operands = [x2d, att4, ssm, w_out_att, w_out_ssm, norm_mlp, w_up, w_down, norm_final]
    in_specs = [row_blk(D_MODEL), pl.BlockSpec((4, tm, LANES), lambda i: (0, i, 0)), row_blk(D_SSM),
                _resident((D_ATT, D_MODEL)), _resident((D_SSM, D_MODEL)), _resident((1, D_MODEL)),
                _resident((D_MODEL, D_FF)), _resident((D_FF, D_MODEL)), _resident((1, D_MODEL))]
    out_specs = [row_blk(D_MODEL)]
    out_shape = [jax.ShapeDtypeStruct((rows, D_MODEL), F32)]
    body = _out_mlp_body
    if sample is not None:
        att_operands, att_in, att_out, att_shape = _attn_sample_specs(*sample)
        assert rows == tm * sample[3].shape[0] and tm % SUBLANES == 0
        operands += att_operands
        in_specs += att_in
        out_specs += att_out
        out_shape += att_shape
        body = _out_mlp_attn_body
    out = pl.pallas_call(
        body,
        grid=(rows // tm,),
        in_specs=in_specs,
        out_specs=out_specs,
        out_shape=out_shape,
        compiler_params=_params(1),
        name="out_mlp",
    )(*operands)
    return out[0] if sample is None else out


def _q_permutation():
    perm = []
    for pp in range(N_KV_HEADS // 2):
        for g in range(Q_PER_KV):
            for e in range(2):
                base = (2 * pp + e) * Q_PER_KV * HEAD_DIM + g * HEAD_DIM
                perm += list(range(base, base + HEAD_DIM))
    return np.asarray(perm, np.int32)


def _to_lanes(buf):
    n, pos = buf.shape[:2]
    return jnp.transpose(buf, (0, 2, 3, 1)).reshape(n, D_KV, pos)


def _from_lanes(buf_t):
    n, _, pos = buf_t.shape
    return jnp.transpose(buf_t.reshape(n, N_KV_HEADS, HEAD_DIM, pos), (0, 3, 1, 2))[None]


def _pad_lanes(v):
    v = v.reshape(1, -1).astype(F32)
    return jnp.pad(v, ((0, 0), (0, LANES - v.shape[1])))


def kernel(x_prompt, x_sample, cache_k, cache_v, state_conv, state_ssm, w_in, w_out, conv_w, conv_b, dt_bias, a_log, d_skip, ssm_norm, norm_mix, norm_mlp, w_up, w_down, norm_final):
    depth = w_in.shape[0]
    assert depth == 1, "single-layer step"
    n_p, seq, _ = x_prompt.shape
    n_s, t_new, _ = x_sample.shape
    n_past = cache_k.shape[2]
    assert seq == MAX_WINDOW and n_past == MAX_WINDOW and t_new == SUBLANES and n_s % SAMPLE_SEQS == 0

    perm = _q_permutation()
    w = w_in[0]
    w_pad = jnp.concatenate(
        [w[:, perm], w[:, D_ATT:], jnp.zeros((D_MODEL, D_IN_PAD - w.shape[1]), F32)], axis=1).astype(BF16)
    w_out_att = w_out[0][perm, :].astype(BF16)
    w_out_ssm = w_out[0][D_ATT:, :].astype(BF16)
    w_up_b, w_down_b = w_up[0].astype(BF16), w_down[0].astype(BF16)
    g_mix, g_mlp, g_fin = (v.reshape(1, D_MODEL) for v in (norm_mix[0], norm_mlp[0], norm_final))
    ssd_consts = (conv_w[0], conv_b[0].reshape(1, D_XBC), _pad_lanes(dt_bias[0]), _pad_lanes(a_log[0]),
                  jnp.repeat(d_skip[0], SSM_HEAD_DIM).reshape(1, D_SSM), ssm_norm[0].reshape(1, D_SSM))

    xp = x_prompt.reshape(n_p * seq, D_MODEL)
    q4, k, v, z, xbc, dt_raw, k_t, v_t = _in_proj(xp, g_mix, w_pad, _rope_table(jnp.arange(seq)), seq=seq)
    att4 = _attn_prompt(q4, k, v, n_p, seq)
    y_ssm, st_p = _ssd_prompt(xbc, z, dt_raw, n_p, seq, ssd_consts)
    k_prompt, v_prompt = _from_lanes(k_t), _from_lanes(v_t)
    conv_prompt = xbc.reshape(1, n_p, seq, D_XBC)[:, :, seq - (CONV_WIDTH - 1):]

    xs = x_sample.reshape(n_s * t_new, D_MODEL)
    rows_tile = min(ROW_TILE, n_s * t_new)
    pos = PAST_LEN + (jnp.arange(rows_tile) % t_new)
    q4s, ks, vs, zs, xbcs, dts = _in_proj(xs, g_mix, w_pad, _rope_table(pos))
    y_prompt, att4s, k_sample, v_sample = _out_mlp(
        xp, att4, y_ssm, w_out_att, w_out_ssm, g_mlp, w_up_b, w_down_b, g_fin,
        sample=(q4s, ks, vs, _to_lanes(cache_k[0]), _to_lanes(cache_v[0])))
    prefix_tiles = jnp.pad(state_conv[0], ((0, 0), (t_new - (CONV_WIDTH - 1), 0), (0, 0))).reshape(n_s * t_new, D_XBC)
    y_ssm_s, st_s = _ssd_sample(xbcs, prefix_tiles, zs, dts, state_ssm[0].reshape(n_s, D_SSM, D_STATE), ssd_consts)
    y_sample = _out_mlp(xs, att4s, y_ssm_s, w_out_att, w_out_ssm, g_mlp, w_up_b, w_down_b, g_fin)
    conv_sample = xbcs.reshape(1, n_s, t_new, D_XBC)[:, :, t_new - (CONV_WIDTH - 1):]

    return (y_prompt.reshape(n_p, seq, D_MODEL), y_sample.reshape(n_s, t_new, D_MODEL),
            k_prompt, v_prompt, conv_prompt,
            st_p.reshape(1, n_p, N_SSM_HEADS, SSM_HEAD_DIM, D_STATE),
            _from_lanes(k_sample), _from_lanes(v_sample), conv_sample,
            st_s.reshape(1, n_s, N_SSM_HEADS, SSM_HEAD_DIM, D_STATE))
```

```python
import functools

import numpy as np
import jax
import jax.numpy as jnp
from jax import lax
from jax.experimental import pallas as pl
from jax.experimental.pallas import tpu as pltpu

F32 = jnp.float32
BF16 = jnp.bfloat16

LANES = 128
SUBLANES = 8

D_MODEL = 1024
HEAD_DIM = 64
N_KV_HEADS = 4
Q_PER_KV = 2
D_ATT = 512
D_KV = 256
ROT_HALF = 8
ROPE_THETA = 500000.0
DILATIONS = (1, 4, 16)
N_KEYS = 128
MAX_WINDOW = 2048
ATT_BLK = 128
ATT_UNITS = 8
REGROUP_STRIDE = 4
RESIDUES = 16
PAST_LEN = 8192
ATT_SCALE = HEAD_DIM ** -0.5
NEG_BIG = -1e30
N_SSM_HEADS = 8
SSM_HEAD_DIM = 64
D_SSM = 512
SSM_GROUPS = 2
D_STATE = 128
CONV_WIDTH = 4
CHUNK = 128
D_XBC = 1024
D_FF = 4096
RMS_EPS = 1e-5
D_IN_MAIN = D_ATT + 2 * D_KV + D_SSM + D_XBC
D_IN_PAD = D_IN_MAIN + LANES

VMEM_LIMIT = 56 * 1024 * 1024
ROW_TILE = 512
FF_TILE = 1024
SSD_CHUNKS_PER_STEP = 4
SAMPLE_SEQS = CHUNK // 8

_NT = (((1,), (1,)), ((), ()))


def _params(n_axes):
    return pltpu.CompilerParams(dimension_semantics=("arbitrary",) * n_axes,
                                vmem_limit_bytes=VMEM_LIMIT)


def _resident(shape):
    return pl.BlockSpec(shape, lambda *_: (0,) * len(shape), pipeline_mode=pl.Buffered(1))


def _rms(x, g):
    return x * lax.rsqrt(jnp.mean(x * x, axis=-1, keepdims=True) + RMS_EPS) * g


def _silu(x):
    half = 0.5 * x
    return half + half * jnp.tanh(half)


def _softplus(x):
    return jnp.maximum(x, 0.0) + jnp.log1p(jnp.exp(-jnp.abs(x)))


def _in_proj_body(x_ref, g_ref, w_ref, rope_ref, q_ref, k_ref, v_ref, z_ref, xbc_ref, dt_ref, *kv_t_refs):
    xb = _rms(x_ref[...], g_ref[...]).astype(BF16)
    tm = xb.shape[0]
    first_half = (lax.broadcasted_iota(jnp.int32, (tm, LANES), 1) % HEAD_DIM) < ROT_HALF

    def proj(lo, hi):
        return jnp.dot(xb, w_ref[:, lo:hi], preferred_element_type=F32)

    def rope(u, cos, sin):
        partner = jnp.where(first_half, pltpu.roll(u, LANES - ROT_HALF, 1), pltpu.roll(u, ROT_HALF, 1))
        return u * cos + partner * sin

    cq, sq = rope_ref[:, 0:128], rope_ref[:, 128:256]
    ck, sk = rope_ref[:, 256:384], rope_ref[:, 384:512]
    q = proj(0, D_ATT)
    for j in range(D_ATT // LANES):
        q_ref[j] = rope(q[:, j * LANES:(j + 1) * LANES], cq, sq)
    k = proj(D_ATT, D_ATT + D_KV)
    v = proj(D_ATT + D_KV, D_ATT + 2 * D_KV)
    v_ref[...] = v
    for j in range(D_KV // LANES):
        lanes = slice(j * LANES, (j + 1) * LANES)
        kj = rope(k[:, lanes], ck, sk)
        k_ref[:, lanes] = kj
        if kv_t_refs:
            kv_t_refs[0][lanes, :] = kj.T
            kv_t_refs[1][lanes, :] = v[:, lanes].T
    z_ref[...] = proj(D_ATT + 2 * D_KV, D_ATT + 2 * D_KV + D_SSM)
    xbc_ref[...] = proj(D_ATT + 2 * D_KV + D_SSM, D_IN_MAIN)
    dt_ref[...] = proj(D_IN_MAIN, D_IN_PAD)


def _in_proj(x2d, norm_mix, w_pad, rope_tab, seq=None):
    rows = x2d.shape[0]
    tm = min(ROW_TILE, rows)
    n_tab = rope_tab.shape[0] // tm
    row_blk = lambda w: pl.BlockSpec((tm, w), lambda i: (i, 0))
    out_specs = [pl.BlockSpec((4, tm, LANES), lambda i: (0, i, 0)),
                 row_blk(D_KV), row_blk(D_KV), row_blk(D_SSM), row_blk(D_XBC), row_blk(LANES)]
    out_shape = [jax.ShapeDtypeStruct((4, rows, LANES), F32),
                 jax.ShapeDtypeStruct((rows, D_KV), F32), jax.ShapeDtypeStruct((rows, D_KV), F32),
                 jax.ShapeDtypeStruct((rows, D_SSM), F32), jax.ShapeDtypeStruct((rows, D_XBC), F32),
                 jax.ShapeDtypeStruct((rows, LANES), F32)]
    if seq is not None:
        per_seq = seq // tm
        t_blk = pl.BlockSpec((None, D_KV, tm), lambda i: (i // per_seq, 0, i % per_seq))
        out_specs += [t_blk, t_blk]
        out_shape += [jax.ShapeDtypeStruct((rows // seq, D_KV, seq), F32)] * 2
    return pl.pallas_call(
        _in_proj_body,
        grid=(rows // tm,),
        in_specs=[row_blk(D_MODEL), _resident((1, D_MODEL)), _resident((D_MODEL, D_IN_PAD)),
                  pl.BlockSpec((tm, 4 * LANES), lambda i: (i % n_tab, 0))],
        out_specs=out_specs,
        out_shape=out_shape,
        compiler_params=_params(1),
        name="in_proj",
    )(x2d, norm_mix, w_pad, rope_tab)


def _rope_table(pos):
    inv = ROPE_THETA ** (-jnp.arange(0, 2 * ROT_HALF, 2, dtype=F32) / (2 * ROT_HALF))
    ang = pos.astype(F32)[:, None] * inv[None, :]
    cos, sin = jnp.cos(ang), jnp.sin(ang)
    n = pos.shape[0]
    ones = jnp.ones((n, HEAD_DIM - 2 * ROT_HALF), F32)
    cos_h = jnp.concatenate([cos, cos, ones], axis=1)
    sin_h = jnp.concatenate([-sin, sin, 0.0 * ones], axis=1)
    cos_l, sin_l = jnp.tile(cos_h, (1, 2)), jnp.tile(sin_h, (1, 2))
    return jnp.concatenate([cos_l * ATT_SCALE, sin_l * ATT_SCALE, cos_l, sin_l], axis=1)


def _attn_prompt_body(q_ref, k_ref, v_ref, bias1_ref, bias4_ref, caus_ref, o_ref,
                      q3, k3, v3, m3, l3, acc3, nat_s):
    seq = k_ref.shape[0]
    per_res = seq // RESIDUES
    e0 = lax.broadcasted_iota(jnp.int32, (ATT_BLK, LANES), 1) < HEAD_DIM
    zero = jnp.zeros((ATT_BLK, LANES), F32)

    def regroup(src, tmp, dst):
        quarter = seq // REGROUP_STRIDE
        for r in range(REGROUP_STRIDE):
            tmp[r * quarter:(r + 1) * quarter, :] = src[pl.ds(r, quarter, stride=REGROUP_STRIDE), :]
        for r in range(REGROUP_STRIDE):
            for a in range(RESIDUES // REGROUP_STRIDE):
                dst[r + REGROUP_STRIDE * a] = tmp[pl.ds(r * quarter + a, per_res, stride=REGROUP_STRIDE), :]

    for g in range(Q_PER_KV):
        regroup(q_ref.at[g], nat_s.at[g], q3.at[g])
    regroup(k_ref, nat_s.at[Q_PER_KV], k3)
    regroup(v_ref, nat_s.at[Q_PER_KV + 1], v3)

    def scores(q_idx, k_idx, bias):
        parts = []
        for g in range(Q_PER_KV):
            qg = q3.at[g][q_idx].reshape(ATT_BLK, LANES)
            parts += [jnp.where(e0, qg, zero), jnp.where(e0, zero, qg)]
        lhs = jnp.concatenate(parts, axis=0).astype(BF16)
        kb = k3[k_idx]
        nk = kb.shape[0] * kb.shape[1]
        kb = kb.reshape(nk, LANES).astype(BF16)
        vb = v3[k_idx].reshape(nk, LANES).astype(BF16)
        s = lax.dot_general(lhs, kb, _NT, preferred_element_type=F32)
        s = (s.reshape(4, ATT_BLK, nk) + bias[None]).reshape(4 * ATT_BLK, nk)
        m = jnp.max(s, axis=1, keepdims=True)
        p = jnp.exp(s - m)
        l = jnp.sum(p, axis=1, keepdims=True)
        pv = jnp.dot(p.astype(BF16), vb, preferred_element_type=F32)
        mb = jnp.broadcast_to(m, pv.shape)
        lb = jnp.broadcast_to(l, pv.shape)
        out = []
        for g in range(Q_PER_KV):
            r0, r1, r2 = 2 * g * ATT_BLK, (2 * g + 1) * ATT_BLK, (2 * g + 2) * ATT_BLK
            out.append((jnp.where(e0, mb[r0:r1], mb[r1:r2]), jnp.where(e0, lb[r0:r1], lb[r1:r2]),
                        jnp.where(e0, pv[r0:r1], pv[r1:r2])))
        return out

    def process(units, first_group):
        state = [(m3.at[g], l3.at[g], acc3.at[g]) for g in range(Q_PER_KV)]
        new = [scores(*unit) for unit in units]
        for i, (q_idx, _, _) in enumerate(units):
            for g in range(Q_PER_KV):
                mg, lg, acc = new[i][g]
                if not first_group:
                    m_old, l_old, acc_old = (ref[q_idx].reshape(ATT_BLK, LANES) for ref in state[g])
                    m_new = jnp.maximum(m_old, mg)
                    a_old = jnp.exp(m_old - m_new)
                    a_new = jnp.exp(mg - m_new)
                    mg, lg, acc = m_new, a_old * l_old + a_new * lg, a_old * acc_old + a_new * acc
                for ref, val in zip(state[g], (mg, lg, acc)):
                    ref[q_idx] = val.reshape(ref[q_idx].shape)

    for gi, (d, bias_ref) in enumerate(zip(DILATIONS, (bias1_ref, bias4_ref, caus_ref))):
        blocks = seq // d // ATT_BLK
        lanes16 = RESIDUES // d
        depth = ATT_BLK // lanes16

        def unit(u, d=d, blocks=blocks, lanes16=lanes16, depth=depth, bias_ref=bias_ref):
            r, n = u // blocks, u % blocks
            lead = pl.ds(r, lanes16, stride=d)
            q_idx = (lead, pl.ds(pl.multiple_of(n * depth, depth), depth))
            if blocks == 1:
                return q_idx, q_idx, bias_ref[...]
            k_idx = (lead, pl.ds(pl.multiple_of(jnp.maximum(n - 1, 0) * depth, depth), 2 * depth))
            return q_idx, k_idx, bias_ref[jnp.minimum(n, 1)]

        def step(i, carry, unit=unit, first_group=(gi == 0)):
            process([unit(ATT_UNITS * i + j) for j in range(ATT_UNITS)], first_group)
            return carry

        lax.fori_loop(0, d * blocks // ATT_UNITS, step, 0)

    quarter = seq // REGROUP_STRIDE
    for g in range(Q_PER_KV):
        tmp, out = nat_s.at[Q_PER_KV + g], nat_s.at[g]
        for r in range(REGROUP_STRIDE):
            for a in range(RESIDUES // REGROUP_STRIDE):
                res = r + REGROUP_STRIDE * a
                tmp[pl.ds(r * quarter + a, per_res, stride=REGROUP_STRIDE), :] = acc3[g, res] / l3[g, res]
        for r in range(REGROUP_STRIDE):
            out[pl.ds(r, quarter, stride=REGROUP_STRIDE), :] = tmp[r * quarter:(r + 1) * quarter, :]
        o_ref[g] = out[...].astype(o_ref.dtype)


def _window_bias(d):
    lanes16 = RESIDUES // d

    def order(n):
        a, ll = np.divmod(np.arange(n), n // lanes16)
        return lanes16 * ll + a

    i = order(ATT_BLK)[:, None]
    if d == RESIDUES:
        return jnp.asarray(np.where(order(ATT_BLK)[None, :] <= i, 0.0, NEG_BIG).astype(np.float32))
    j = order(2 * ATT_BLK)[None, :]
    dist = i + ATT_BLK - j
    inner = np.where((dist >= 0) & (dist <= N_KEYS), 0.0, NEG_BIG)
    first = np.where(j <= i, 0.0, NEG_BIG)
    return jnp.asarray(np.stack([first, inner]).astype(np.float32))


def _attn_prompt(q4, k, v, n_seq, seq):
    biases = [_window_bias(d) for d in DILATIONS]
    slab = (RESIDUES, seq // RESIDUES, LANES)
    return pl.pallas_call(
        _attn_prompt_body,
        grid=(n_seq, 2),
        in_specs=[pl.BlockSpec((2, seq, LANES), lambda b, pp: (pp, b, 0)),
                  pl.BlockSpec((seq, LANES), lambda b, pp: (b, pp)),
                  pl.BlockSpec((seq, LANES), lambda b, pp: (b, pp))] + [_resident(x.shape) for x in biases],
        out_specs=pl.BlockSpec((2, seq, LANES), lambda b, pp: (pp, b, 0)),
        out_shape=jax.ShapeDtypeStruct(q4.shape, BF16),
        scratch_shapes=[pltpu.VMEM((2,) + slab, F32), pltpu.VMEM(slab, F32), pltpu.VMEM(slab, F32)]
                       + [pltpu.VMEM((2,) + slab, F32)] * 3 + [pltpu.VMEM((Q_PER_KV + 2, seq, LANES), F32)],
        compiler_params=_params(2),
        name="attn_prompt",
    )(q4, k, v, *biases)


def _attn_sample_pieces(q_ref, kn_ref, vn_ref, ck_ref, cv_ref, mult_ref, att_ref, ko_ref, vo_ref):
    t = kn_ref.shape[0]
    n_past = ck_ref.shape[2]
    tail = slice(n_past - LANES, n_past)

    def shift(c_ref, n_ref, o_ref, rows):
        is_new = lax.broadcasted_iota(jnp.int32, (LANES, LANES), 1) >= LANES - t
        moved = pltpu.roll(c_ref[0, rows, :], n_past - t, 1)
        o_ref[0, rows, :] = moved
        new_rows = jnp.concatenate([n_ref[:, rows], jnp.zeros((LANES - t, LANES), F32)], axis=0)
        o_ref[0, rows, tail] = jnp.where(is_new, pltpu.roll(new_rows.T, LANES - t, 1), moved[:, tail])

    def attend():
        mult = mult_ref[...]
        e0 = lax.broadcasted_iota(jnp.int32, (t, LANES), 1) < HEAD_DIM
        zero = jnp.zeros((t, LANES), F32)
        pairs = N_KV_HEADS // 2
        parts = []
        for pp in range(pairs):
            for g in range(Q_PER_KV):
                qg = q_ref[pp * Q_PER_KV + g]
                for masked in (jnp.where(e0, qg, zero), jnp.where(e0, zero, qg)):
                    parts.append(jnp.concatenate([masked if j == pp else zero for j in range(pairs)], axis=1))
        lhs = jnp.concatenate(parts, axis=0).astype(BF16)
        pad = jnp.zeros((LANES - t, D_KV), F32)
        kn = jnp.concatenate([kn_ref[...], pad], axis=0).astype(BF16)
        vn = jnp.concatenate([vn_ref[...], pad], axis=0).astype(BF16)
        s = jnp.concatenate([jnp.dot(lhs, ck_ref[0].astype(BF16), preferred_element_type=F32),
                             lax.dot_general(lhs, kn, _NT, preferred_element_type=F32)], axis=1)
        s = jnp.where(mult > 0.0, s, NEG_BIG)
        m = jnp.max(s, axis=1, keepdims=True)
        p = jnp.exp(s - m) * mult
        l = jnp.sum(p, axis=1, keepdims=True)
        pb = p.astype(BF16)
        pv = (lax.dot_general(pb[:, :n_past], cv_ref[0].astype(BF16), _NT, preferred_element_type=F32)
              + jnp.dot(pb[:, n_past:], vn, preferred_element_type=F32))
        o = pv / l
        for pp in range(pairs):
            lanes = slice(pp * LANES, (pp + 1) * LANES)
            for g in range(Q_PER_KV):
                r0 = (pp * Q_PER_KV + g) * 2 * t
                att_ref[pp * Q_PER_KV + g] = jnp.where(e0, o[r0:r0 + t, lanes], o[r0 + t:r0 + 2 * t, lanes])

    pieces = [attend]
    for pp in range(N_KV_HEADS // 2):
        rows = slice(pp * LANES, (pp + 1) * LANES)
        pieces += [functools.partial(shift, ck_ref, kn_ref, ko_ref, rows),
                   functools.partial(shift, cv_ref, vn_ref, vo_ref, rows)]
    return pieces


def _key_multiplicity(t, n_past):
    idx = np.concatenate([np.arange(n_past + t), np.full(LANES - t, 10 ** 9)])[None, :]
    dist = n_past + np.arange(t)[:, None] - idx
    mult = np.zeros(dist.shape, np.float32)
    for d in DILATIONS:
        mult += (dist >= 0) & (dist % d == 0) & (dist <= N_KEYS * d)
    return jnp.asarray(np.tile(mult, (2 * N_KV_HEADS, 1)))


def _attn_sample_specs(q4, k_new, v_new, cache_k, cache_v):
    n_seq, _, n_past = cache_k.shape
    t = k_new.shape[0] // n_seq
    mult = _key_multiplicity(t, n_past)
    cache_blk = pl.BlockSpec((1, D_KV, n_past), lambda b: (b, 0, 0))
    new_blk = pl.BlockSpec((t, D_KV), lambda b: (b, 0))
    q_blk = pl.BlockSpec((4, t, LANES), lambda b: (0, b, 0))
    return ([q4, k_new, v_new, cache_k, cache_v, mult],
            [q_blk, new_blk, new_blk, cache_blk, cache_blk, _resident(mult.shape)],
            [q_blk, cache_blk, cache_blk],
            [jax.ShapeDtypeStruct(q4.shape, F32),
             jax.ShapeDtypeStruct(cache_k.shape, F32), jax.ShapeDtypeStruct(cache_v.shape, F32)])


def _conv_silu(delayed, cw_ref, cb_ref, xc_s):
    for j in range(D_XBC // LANES):
        lanes = slice(j * LANES, (j + 1) * LANES)
        y = cb_ref[:, lanes]
        for tap in range(CONV_WIDTH):
            y = y + delayed(CONV_WIDTH - 1 - tap, lanes) * cw_ref[tap:tap + 1, lanes]
        xc_s[:, lanes] = _silu(y)


def _select_sum(x, sel):
    heads = x.shape[0]
    x = jnp.concatenate([x, jnp.zeros((LANES - heads, x.shape[1]), F32)], axis=0)
    total = None
    for _ in range(3):
        piece = x.astype(BF16)
        x = x - piece.astype(F32)
        part = jnp.dot(piece, sel, preferred_element_type=F32)
        total = part if total is None else total + part
    return total[0:heads]


def _to_columns(x_t):
    pad = jnp.zeros((LANES - x_t.shape[0], x_t.shape[1]), F32)
    return jnp.concatenate([x_t, pad], axis=0).T


def _ssd_scalars(dt_raw, dtb_ref, alog_ref, before, seg_end=None):
    dt_t = _softplus(dt_raw.T[0:N_SSM_HEADS] + dtb_ref[...])
    a_t = dt_t * (-jnp.exp(alog_ref[...]))
    a_cum_t = _select_sum(a_t, before)
    if seg_end is None:
        a_end_t = jnp.broadcast_to(a_cum_t[:, CHUNK - 1:CHUNK], a_cum_t.shape)
    else:
        a_end_t = _select_sum(a_cum_t, seg_end)
    to_end_t = jnp.exp(a_end_t - a_cum_t) * dt_t
    return dt_t, _to_columns(a_cum_t), a_cum_t, a_end_t, to_end_t


def _ssd_diag(xc_s, blk, cb, a_cum, a_cum_t, dt_t, seg_mask):
    e0 = lax.broadcasted_iota(jnp.int32, (CHUNK, LANES), 1) < SSM_HEAD_DIM
    x_pair = xc_s[:, blk * LANES:(blk + 1) * LANES].astype(BF16)
    out, grow = [], []
    for e in range(2):
        h = 2 * blk + e
        a_col = jnp.broadcast_to(a_cum[:, h:h + 1], (CHUNK, CHUNK))
        seg = a_col - jnp.broadcast_to(a_cum_t[h:h + 1, :], (CHUNK, CHUNK))
        w = cb * jnp.where(seg_mask, jnp.exp(seg), 0.0) * dt_t[h:h + 1, :]
        out.append(jnp.dot(w.astype(BF16), x_pair, preferred_element_type=F32))
        grow.append(jnp.exp(a_col))
    return jnp.where(e0, out[0], out[1]), jnp.where(e0, grow[0], grow[1])


def _ssd_finish(xc_s, z_ref, rows, y_parts, dsk_ref, nrm_ref, y_ref):
    blocks_per_group = D_SSM // SSM_GROUPS // LANES
    for g in range(SSM_GROUPS):
        gated = []
        for p in range(blocks_per_group):
            blk = g * blocks_per_group + p
            lanes = slice(blk * LANES, (blk + 1) * LANES)
            y = y_parts[blk] + dsk_ref[:, lanes] * xc_s[:, lanes]
            gated.append(y * _silu(z_ref[rows, lanes]))
        ss = sum(jnp.sum(y * y, axis=-1, keepdims=True) for y in gated)
        inv = lax.rsqrt(ss / (D_SSM // SSM_GROUPS) + RMS_EPS)
        for p in range(blocks_per_group):
            blk = g * blocks_per_group + p
            lanes = slice(blk * LANES, (blk + 1) * LANES)
            y_ref[rows, lanes] = (gated[p] * inv * nrm_ref[:, lanes]).astype(y_ref.dtype)


def _ssd_prompt_body(xbc_ref, z_ref, dt_ref, cw_ref, cb_ref, dtb_ref, alog_ref, dsk_ref, nrm_ref,
                     y_ref, st_ref, st_s, xc_bufs):
    seq = xbc_ref.shape[0]
    st_s[...] = jnp.zeros_like(st_s)
    row = lax.broadcasted_iota(jnp.int32, (CHUNK, CHUNK), 0)
    col = lax.broadcasted_iota(jnp.int32, (CHUNK, CHUNK), 1)
    causal = row >= col
    before = jnp.where(row <= col, 1.0, 0.0).astype(BF16)
    row8 = lax.broadcasted_iota(jnp.int32, (SUBLANES, LANES), 0)
    heads_per_group = N_SSM_HEADS // SSM_GROUPS
    group_rows = heads_per_group * SSM_HEAD_DIM

    def chunk(c, xc_s):
        rows = pl.ds(pl.multiple_of(c * CHUNK, CHUNK), CHUNK)
        prev_rows = pl.ds(pl.multiple_of(jnp.maximum(c * CHUNK - SUBLANES, 0), SUBLANES), SUBLANES)
        have_prev = c > 0

        def delayed(k, lanes):
            cur = xbc_ref[rows, lanes]
            if k == 0:
                return cur
            rolled = pltpu.roll(cur, k, 0)
            prev = jnp.where(have_prev, xbc_ref[prev_rows, lanes], 0.0)
            head = jnp.where(row8 < k, pltpu.roll(prev, k, 0), rolled[0:SUBLANES])
            return jnp.concatenate([head, rolled[SUBLANES:]], axis=0)

        _conv_silu(delayed, cw_ref, cb_ref, xc_s)
        dt_t, a_cum, a_cum_t, a_end_t, to_end_t = _ssd_scalars(dt_ref[rows, :], dtb_ref, alog_ref, before)
        x_t = xc_s[:, 0:D_SSM].T
        y_parts = []
        for g in range(SSM_GROUPS):
            bg = xc_s[:, D_SSM + g * D_STATE:D_SSM + (g + 1) * D_STATE].astype(BF16)
            cg = xc_s[:, D_SSM + (SSM_GROUPS + g) * D_STATE:D_SSM + (SSM_GROUPS + g + 1) * D_STATE].astype(BF16)
            cb = lax.dot_general(cg, bg, _NT, preferred_element_type=F32)
            grp = slice(g * group_rows, (g + 1) * group_rows)
            st_prev = st_s[grp, :]
            y_off = lax.dot_general(cg, st_prev.astype(BF16), _NT, preferred_element_type=F32)
            for p in range(group_rows // LANES):
                blk = g * (group_rows // LANES) + p
                y_diag, grow = _ssd_diag(xc_s, blk, cb, a_cum, a_cum_t, dt_t, causal)
                y_parts.append(y_diag + y_off[:, p * LANES:(p + 1) * LANES] * grow)
            xw = []
            for h in range(g * heads_per_group, (g + 1) * heads_per_group):
                hr = slice(h * SSM_HEAD_DIM, (h + 1) * SSM_HEAD_DIM)
                xw.append((x_t[hr, :] * to_end_t[h:h + 1, :]).astype(BF16))
            new = jnp.dot(jnp.concatenate(xw, axis=0), bg, preferred_element_type=F32)
            for i, h in enumerate(range(g * heads_per_group, (g + 1) * heads_per_group)):
                hr = slice(h * SSM_HEAD_DIM, (h + 1) * SSM_HEAD_DIM)
                lr = slice(i * SSM_HEAD_DIM, (i + 1) * SSM_HEAD_DIM)
                decay = jnp.exp(jnp.broadcast_to(a_end_t[h:h + 1, 0:1], (SSM_HEAD_DIM, D_STATE)))
                st_s[hr, :] = decay * st_prev[lr, :] + new[lr, :]
        _ssd_finish(xc_s, z_ref, rows, y_parts, dsk_ref, nrm_ref, y_ref)

    def step(i, carry):
        for j in range(SSD_CHUNKS_PER_STEP):
            chunk(SSD_CHUNKS_PER_STEP * i + j, xc_bufs.at[j])
        return carry

    lax.fori_loop(0, seq // CHUNK // SSD_CHUNKS_PER_STEP, step, 0)
    st_ref[...] = st_s[...]


def _ssd_prompt(xbc, z, dt_raw, n_seq, seq, consts):
    rows = xbc.shape[0]
    blk = lambda w: pl.BlockSpec((seq, w), lambda b: (b, 0))
    state_blk = pl.BlockSpec((None, D_SSM, D_STATE), lambda b: (b, 0, 0))
    return pl.pallas_call(
        _ssd_prompt_body,
        grid=(n_seq,),
        in_specs=[blk(D_XBC), blk(D_SSM), blk(LANES)] + [_resident(c.shape) for c in consts],
        out_specs=[blk(D_SSM), state_blk],
        out_shape=[jax.ShapeDtypeStruct((rows, D_SSM), BF16),
                   jax.ShapeDtypeStruct((n_seq, D_SSM, D_STATE), F32)],
        scratch_shapes=[pltpu.VMEM((D_SSM, D_STATE), F32),
                        pltpu.VMEM((SSD_CHUNKS_PER_STEP, CHUNK, D_XBC), F32)],
        compiler_params=_params(1),
        name="ssd_prompt",
    )(xbc, z, dt_raw, *consts)


def _ssd_sample_body(xbc_ref, pre_ref, z_ref, dt_ref, h0_ref, cw_ref, cb_ref, dtb_ref, alog_ref, dsk_ref,
                     nrm_ref, y_ref, st_ref, xc_s, yoff_s, aend_s):
    t = CHUNK // SAMPLE_SEQS
    row = lax.broadcasted_iota(jnp.int32, (CHUNK, CHUNK), 0)
    col = lax.broadcasted_iota(jnp.int32, (CHUNK, CHUNK), 1)
    same_seq = (row // t) == (col // t)
    seg_mask = same_seq & (row >= col)
    before = jnp.where(same_seq & (row <= col), 1.0, 0.0).astype(BF16)
    seg_end = jnp.where(row == (col // t) * t + (t - 1), 1.0, 0.0).astype(BF16)
    step = lax.broadcasted_iota(jnp.int32, (CHUNK, LANES), 0) % t
    rows = pl.ds(0, CHUNK)
    heads_per_group = N_SSM_HEADS // SSM_GROUPS
    group_rows = heads_per_group * SSM_HEAD_DIM

    def delayed(k, lanes):
        cur = xbc_ref[:, lanes]
        if k == 0:
            return cur
        return jnp.where(step < k, pltpu.roll(pre_ref[:, lanes], (k - t) % CHUNK, 0), pltpu.roll(cur, k, 0))

    _conv_silu(delayed, cw_ref, cb_ref, xc_s)
    dt_t, a_cum, a_cum_t, a_end_t, to_end_t = _ssd_scalars(dt_ref[...], dtb_ref, alog_ref, before, seg_end)
    aend_s[...] = _to_columns(a_end_t)
    x_t = xc_s[:, 0:D_SSM].T
    col_seq = lax.broadcasted_iota(jnp.int32, (SSM_HEAD_DIM, CHUNK), 1) // t
    y_parts = []
    for g in range(SSM_GROUPS):
        bg = xc_s[:, D_SSM + g * D_STATE:D_SSM + (g + 1) * D_STATE].astype(BF16)
        cg_lanes = slice(D_SSM + (SSM_GROUPS + g) * D_STATE, D_SSM + (SSM_GROUPS + g + 1) * D_STATE)
        cg = xc_s[:, cg_lanes].astype(BF16)
        cb = lax.dot_general(cg, bg, _NT, preferred_element_type=F32)
        grp = slice(g * group_rows, (g + 1) * group_rows)
        xw = []
        for h in range(g * heads_per_group, (g + 1) * heads_per_group):
            hr = slice(h * SSM_HEAD_DIM, (h + 1) * SSM_HEAD_DIM)
            xw.append(x_t[hr, :] * to_end_t[h:h + 1, :])

        def per_seq(b, carry, g=g, bg=bg, cg_lanes=cg_lanes, grp=grp, xw=xw):
            seq_rows = pl.ds(pl.multiple_of(b * t, t), t)
            h_prev = h0_ref[b, grp, :]
            cg_b = xc_s[seq_rows, cg_lanes].astype(BF16)
            yoff_s[seq_rows, g * group_rows:(g + 1) * group_rows] = lax.dot_general(
                cg_b, h_prev.astype(BF16), _NT, preferred_element_type=F32)
            mine = col_seq == b
            xw_b = jnp.concatenate([jnp.where(mine, w, 0.0).astype(BF16) for w in xw], axis=0)
            new = jnp.dot(xw_b, bg, preferred_element_type=F32)
            a_last = aend_s[pl.ds(b * t, 1), :]
            for i, h in enumerate(range(g * heads_per_group, (g + 1) * heads_per_group)):
                lr = slice(i * SSM_HEAD_DIM, (i + 1) * SSM_HEAD_DIM)
                decay = jnp.exp(jnp.broadcast_to(a_last[:, h:h + 1], (SSM_HEAD_DIM, D_STATE)))
                st_ref[b, h * SSM_HEAD_DIM:(h + 1) * SSM_HEAD_DIM, :] = decay * h_prev[lr, :] + new[lr, :]
            return carry

        lax.fori_loop(0, SAMPLE_SEQS, per_seq, 0, unroll=4)
        for p in range(group_rows // LANES):
            blk = g * (group_rows // LANES) + p
            lanes = slice(blk * LANES, (blk + 1) * LANES)
            y_diag, grow = _ssd_diag(xc_s, blk, cb, a_cum, a_cum_t, dt_t, seg_mask)
            y_parts.append(y_diag + yoff_s[:, lanes] * grow)
    _ssd_finish(xc_s, z_ref, rows, y_parts, dsk_ref, nrm_ref, y_ref)


def _ssd_sample(xbc, prefix_tiles, z, dt_raw, h0, consts):
    rows = xbc.shape[0]
    n_seq = h0.shape[0]
    blk = lambda w: pl.BlockSpec((CHUNK, w), lambda i: (i, 0))
    state_blk = pl.BlockSpec((SAMPLE_SEQS, D_SSM, D_STATE), lambda i: (i, 0, 0))
    return pl.pallas_call(
        _ssd_sample_body,
        grid=(rows // CHUNK,),
        in_specs=[blk(D_XBC), blk(D_XBC), blk(D_SSM), blk(LANES), state_blk]
                 + [_resident(c.shape) for c in consts],
        out_specs=[blk(D_SSM), state_blk],
        out_shape=[jax.ShapeDtypeStruct((rows, D_SSM), BF16),
                   jax.ShapeDtypeStruct((n_seq, D_SSM, D_STATE), F32)],
        scratch_shapes=[pltpu.VMEM((CHUNK, D_XBC), F32), pltpu.VMEM((CHUNK, D_SSM), F32),
                        pltpu.VMEM((CHUNK, LANES), F32)],
        compiler_params=_params(1),
        name="ssd_sample",
    )(xbc, prefix_tiles, z, dt_raw, h0, *consts)


def _out_mlp_body(x_ref, att_ref, ssm_ref, woa_ref, wos_ref, g_ref, wu_ref, wd_ref, gf_ref, o_ref, side_work=()):
    side_work = list(side_work)
    att = jnp.concatenate([att_ref[j] for j in range(att_ref.shape[0])], axis=1).astype(BF16)
    h = (x_ref[...] + jnp.dot(att, woa_ref[...], preferred_element_type=F32)
         + jnp.dot(ssm_ref[...], wos_ref[...], preferred_element_type=F32))
    hn = _rms(h, g_ref[...]).astype(BF16)
    mlp = None
    for c in range(D_FF // FF_TILE):
        if side_work:
            side_work.pop(0)()
        cols = slice(c * FF_TILE, (c + 1) * FF_TILE)
        u = jnp.maximum(jnp.dot(hn, wu_ref[:, cols], preferred_element_type=F32), 0.0)
        down = jnp.dot((u * u).astype(BF16), wd_ref[cols, :], preferred_element_type=F32)
        mlp = down if mlp is None else mlp + down
    for work in side_work:
        work()
    o_ref[...] = _rms(h + mlp, gf_ref[...])


def _out_mlp_attn_body(*refs):
    n_mlp, n_att = 9, 6
    o_ref = refs[n_mlp + n_att]
    pieces = _attn_sample_pieces(*refs[n_mlp:n_mlp + n_att], *refs[n_mlp + n_att + 1:])
    _out_mlp_body(*refs[:n_mlp], o_ref, side_work=pieces)


def _out_mlp(x2d, att4, ssm, w_out_att, w_out_ssm, norm_mlp, w_up, w_down, norm_final, sample=None):
    rows = x2d.shape[0]
    tm = min(ROW_TILE, rows) if sample is None else rows // sample[3].shape[0]
    row_blk = lambda w: pl.BlockSpec((tm, w), lambda i: (i, 0))
    operands = [x2d, att4, ssm, w_out_att, w_out_ssm, norm_mlp, w_up, w_down, norm_final]
    in_specs = [row_blk(D_MODEL), pl.BlockSpec((4, tm, LANES), lambda i: (0, i, 0)), row_blk(D_SSM),
                _resident((D_ATT, D_MODEL)), _resident((D_SSM, D_MODEL)), _resident((1, D_MODEL)),
                _resident((D_MODEL, D_FF)), _resident((D_FF, D_MODEL)), _resident((1, D_MODEL))]
    out_specs = [row_blk(D_MODEL)]
    out_shape = [jax.ShapeDtypeStruct((rows, D_MODEL), F32)]
    body = _out_mlp_body
    if sample is not None:
        att_operands, att_in, att_out, att_shape = _attn_sample_specs(*sample)
        assert rows == tm * sample[3].shape[0] and tm % SUBLANES == 0
        operands += att_operands
        in_specs += att_in
        out_specs += att_out
        out_shape += att_shape
        body = _out_mlp_attn_body
    out = pl.pallas_call(
        body,
        grid=(rows // tm,),
        in_specs=in_specs,
        out_specs=out_specs,
        out_shape=out_shape,
        compiler_params=_params(1),
        name="out_mlp",
    )(*operands)
    return out[0] if sample is None else out


def _q_permutation():
    perm = []
    for pp in range(N_KV_HEADS // 2):
        for g in range(Q_PER_KV):
            for e in range(2):
                base = (2 * pp + e) * Q_PER_KV * HEAD_DIM + g * HEAD_DIM
                perm += list(range(base, base + HEAD_DIM))
    return np.asarray(perm, np.int32)


def _to_lanes(buf):
    n, pos = buf.shape[:2]
    return jnp.transpose(buf, (0, 2, 3, 1)).reshape(n, D_KV, pos)


def _from_lanes(buf_t):
    n, _, pos = buf_t.shape
    return jnp.transpose(buf_t.reshape(n, N_KV_HEADS, HEAD_DIM, pos), (0, 3, 1, 2))[None]


def _head_rows(v):
    return jnp.broadcast_to(v.astype(F32)[:, None], (v.shape[0], LANES))


def kernel(x_prompt, x_sample, cache_k, cache_v, state_conv, state_ssm, w_in, w_out, conv_w, conv_b, dt_bias, a_log, d_skip, ssm_norm, norm_mix, norm_mlp, w_up, w_down, norm_final):
    depth = w_in.shape[0]
    assert depth == 1, "single-layer step"
    n_p, seq, _ = x_prompt.shape
    n_s, t_new, _ = x_sample.shape
    n_past = cache_k.shape[2]
    assert seq == MAX_WINDOW and n_past == MAX_WINDOW and t_new == SUBLANES and n_s % SAMPLE_SEQS == 0

    perm = _q_permutation()
    w = w_in[0]
    w_pad = jnp.concatenate(
        [w[:, perm], w[:, D_ATT:], jnp.zeros((D_MODEL, D_IN_PAD - w.shape[1]), F32)], axis=1).astype(BF16)
    w_out_att = w_out[0][perm, :].astype(BF16)
    w_out_ssm = w_out[0][D_ATT:, :].astype(BF16)
    w_up_b, w_down_b = w_up[0].astype(BF16), w_down[0].astype(BF16)
    g_mix, g_mlp, g_fin = (v.reshape(1, D_MODEL) for v in (norm_mix[0], norm_mlp[0], norm_final))
    ssd_consts = (conv_w[0], conv_b[0].reshape(1, D_XBC), _head_rows(dt_bias[0]), _head_rows(a_log[0]),
                  jnp.repeat(d_skip[0], SSM_HEAD_DIM).reshape(1, D_SSM), ssm_norm[0].reshape(1, D_SSM))

    xp = x_prompt.reshape(n_p * seq, D_MODEL)
    q4, k, v, z, xbc, dt_raw, k_t, v_t = _in_proj(xp, g_mix, w_pad, _rope_table(jnp.arange(seq)), seq=seq)
    att4 = _attn_prompt(q4, k, v, n_p, seq)
    y_ssm, st_p = _ssd_prompt(xbc, z, dt_raw, n_p, seq, ssd_consts)
    k_prompt, v_prompt = _from_lanes(k_t), _from_lanes(v_t)
    conv_prompt = xbc.reshape(1, n_p, seq, D_XBC)[:, :, seq - (CONV_WIDTH - 1):]

    xs = x_sample.reshape(n_s * t_new, D_MODEL)
    rows_tile = min(ROW_TILE, n_s * t_new)
    pos = PAST_LEN + (jnp.arange(rows_tile) % t_new)
    q4s, ks, vs, zs, xbcs, dts = _in_proj(xs, g_mix, w_pad, _rope_table(pos))
    y_prompt, att4s, k_sample, v_sample = _out_mlp(
        xp, att4, y_ssm, w_out_att, w_out_ssm, g_mlp, w_up_b, w_down_b, g_fin,
        sample=(q4s, ks, vs, _to_lanes(cache_k[0]), _to_lanes(cache_v[0])))
    prefix_tiles = jnp.pad(state_conv[0], ((0, 0), (t_new - (CONV_WIDTH - 1), 0), (0, 0))).reshape(n_s * t_new, D_XBC)
    y_ssm_s, st_s = _ssd_sample(xbcs, prefix_tiles, zs, dts, state_ssm[0].reshape(n_s, D_SSM, D_STATE), ssd_consts)
    y_sample = _out_mlp(xs, att4s, y_ssm_s, w_out_att, w_out_ssm, g_mlp, w_up_b, w_down_b, g_fin)
    conv_sample = xbcs.reshape(1, n_s, t_new, D_XBC)[:, :, t_new - (CONV_WIDTH - 1):]

    return (y_prompt.reshape(n_p, seq, D_MODEL), y_sample.reshape(n_s, t_new, D_MODEL),
            k_prompt, v_prompt, conv_prompt,
            st_p.reshape(1, n_p, N_SSM_HEADS, SSM_HEAD_DIM, D_STATE),
            _from_lanes(k_sample), _from_lanes(v_sample), conv_sample,
            st_s.reshape(1, n_s, N_SSM_HEADS, SSM_HEAD_DIM, D_STATE))
```

```python
import functools

import numpy as np
import jax
import jax.numpy as jnp
from jax import lax
from jax.experimental import pallas as pl
from jax.experimental.pallas import tpu as pltpu

F32 = jnp.float32
BF16 = jnp.bfloat16

LANES = 128
SUBLANES = 8

D_MODEL = 1024
HEAD_DIM = 64
N_KV_HEADS = 4
Q_PER_KV = 2
D_ATT = 512
D_KV = 256
ROT_HALF = 8
ROPE_THETA = 500000.0
DILATIONS = (1, 4, 16)
N_KEYS = 128
MAX_WINDOW = 2048
ATT_BLK = 128
ATT_UNITS = 8
REGROUP_STRIDE = 4
RESIDUES = 16
PAST_LEN = 8192
ATT_SCALE = HEAD_DIM ** -0.5
NEG_BIG = -1e30
N_SSM_HEADS = 8
SSM_HEAD_DIM = 64
D_SSM = 512
SSM_GROUPS = 2
D_STATE = 128
CONV_WIDTH = 4
CHUNK = 128
D_XBC = 1024
D_FF = 4096
RMS_EPS = 1e-5
D_IN_MAIN = D_ATT + 2 * D_KV + D_SSM + D_XBC
D_IN_PAD = D_IN_MAIN + LANES

VMEM_LIMIT = 56 * 1024 * 1024
ROW_TILE = 512
FF_TILE = 1024
SSD_CHUNKS_PER_STEP = 4
SAMPLE_SEQS = CHUNK // 8

_NT = (((1,), (1,)), ((), ()))


def _params(n_axes):
    return pltpu.CompilerParams(dimension_semantics=("arbitrary",) * n_axes,
                                vmem_limit_bytes=VMEM_LIMIT)


def _resident(shape):
    return pl.BlockSpec(shape, lambda *_: (0,) * len(shape), pipeline_mode=pl.Buffered(1))


def _rms(x, g):
    return x * lax.rsqrt(jnp.mean(x * x, axis=-1, keepdims=True) + RMS_EPS) * g


def _silu(x):
    half = 0.5 * x
    return half + half * jnp.tanh(half)


def _softplus(x):
    return jnp.maximum(x, 0.0) + jnp.log1p(jnp.exp(-jnp.abs(x)))


def _in_proj_body(x_ref, g_ref, w_ref, rope_ref, q_ref, k_ref, v_ref, z_ref, xbc_ref, dt_ref, *kv_t_refs):
    xb = _rms(x_ref[...], g_ref[...]).astype(BF16)
    tm = xb.shape[0]
    first_half = (lax.broadcasted_iota(jnp.int32, (tm, LANES), 1) % HEAD_DIM) < ROT_HALF

    def proj(lo, hi):
        return jnp.dot(xb, w_ref[:, lo:hi], preferred_element_type=F32)

    def rope(u, cos, sin):
        partner = jnp.where(first_half, pltpu.roll(u, LANES - ROT_HALF, 1), pltpu.roll(u, ROT_HALF, 1))
        return u * cos + partner * sin

    cq, sq = rope_ref[:, 0:128], rope_ref[:, 128:256]
    ck, sk = rope_ref[:, 256:384], rope_ref[:, 384:512]
    q = proj(0, D_ATT)
    for j in range(D_ATT // LANES):
        q_ref[j] = rope(q[:, j * LANES:(j + 1) * LANES], cq, sq)
    k = proj(D_ATT, D_ATT + D_KV)
    v = proj(D_ATT + D_KV, D_ATT + 2 * D_KV)
    v_ref[...] = v
    for j in range(D_KV // LANES):
        lanes = slice(j * LANES, (j + 1) * LANES)
        kj = rope(k[:, lanes], ck, sk)
        k_ref[:, lanes] = kj
        if kv_t_refs:
            kv_t_refs[0][lanes, :] = kj.T
            kv_t_refs[1][lanes, :] = v[:, lanes].T
    z_ref[...] = proj(D_ATT + 2 * D_KV, D_ATT + 2 * D_KV + D_SSM)
    xbc_ref[...] = proj(D_ATT + 2 * D_KV + D_SSM, D_IN_MAIN)
    dt_ref[...] = proj(D_IN_MAIN, D_IN_PAD)


def _in_proj(x2d, norm_mix, w_pad, rope_tab, seq=None):
    rows = x2d.shape[0]
    tm = min(ROW_TILE, rows)
    n_tab = rope_tab.shape[0] // tm
    row_blk = lambda w: pl.BlockSpec((tm, w), lambda i: (i, 0))
    out_specs = [pl.BlockSpec((4, tm, LANES), lambda i: (0, i, 0)),
                 row_blk(D_KV), row_blk(D_KV), row_blk(D_SSM), row_blk(D_XBC), row_blk(LANES)]
    out_shape = [jax.ShapeDtypeStruct((4, rows, LANES), F32),
                 jax.ShapeDtypeStruct((rows, D_KV), F32), jax.ShapeDtypeStruct((rows, D_KV), F32),
                 jax.ShapeDtypeStruct((rows, D_SSM), F32), jax.ShapeDtypeStruct((rows, D_XBC), F32),
                 jax.ShapeDtypeStruct((rows, LANES), F32)]
    if seq is not None:
        per_seq = seq // tm
        t_blk = pl.BlockSpec((None, D_KV, tm), lambda i: (i // per_seq, 0, i % per_seq))
        out_specs += [t_blk, t_blk]
        out_shape += [jax.ShapeDtypeStruct((rows // seq, D_KV, seq), F32)] * 2
    return pl.pallas_call(
        _in_proj_body,
        grid=(rows // tm,),
        in_specs=[row_blk(D_MODEL), _resident((1, D_MODEL)), _resident((D_MODEL, D_IN_PAD)),
                  pl.BlockSpec((tm, 4 * LANES), lambda i: (i % n_tab, 0))],
        out_specs=out_specs,
        out_shape=out_shape,
        compiler_params=_params(1),
        name="in_proj",
    )(x2d, norm_mix, w_pad, rope_tab)


def _rope_table(pos):
    inv = ROPE_THETA ** (-jnp.arange(0, 2 * ROT_HALF, 2, dtype=F32) / (2 * ROT_HALF))
    ang = pos.astype(F32)[:, None] * inv[None, :]
    cos, sin = jnp.cos(ang), jnp.sin(ang)
    n = pos.shape[0]
    ones = jnp.ones((n, HEAD_DIM - 2 * ROT_HALF), F32)
    cos_h = jnp.concatenate([cos, cos, ones], axis=1)
    sin_h = jnp.concatenate([-sin, sin, 0.0 * ones], axis=1)
    cos_l, sin_l = jnp.tile(cos_h, (1, 2)), jnp.tile(sin_h, (1, 2))
    return jnp.concatenate([cos_l * ATT_SCALE, sin_l * ATT_SCALE, cos_l, sin_l], axis=1)


def _attn_prompt_body(q_ref, k_ref, v_ref, bias1_ref, bias4_ref, caus_ref, o_ref,
                      q3, k3, vhead3, m3, l3, acc3, nat_s):
    seq = k_ref.shape[0]
    per_res = seq // RESIDUES
    e0 = lax.broadcasted_iota(jnp.int32, (ATT_BLK, LANES), 1) < HEAD_DIM
    zero = jnp.zeros((ATT_BLK, LANES), F32)

    def regroup(src, tmp, dst):
        quarter = seq // REGROUP_STRIDE
        for r in range(REGROUP_STRIDE):
            tmp[r * quarter:(r + 1) * quarter, :] = src[pl.ds(r, quarter, stride=REGROUP_STRIDE), :]
        for r in range(REGROUP_STRIDE):
            for a in range(RESIDUES // REGROUP_STRIDE):
                dst[r + REGROUP_STRIDE * a] = tmp[pl.ds(r * quarter + a, per_res, stride=REGROUP_STRIDE), :]

    for g in range(Q_PER_KV):
        regroup(q_ref.at[g], nat_s.at[g], q3.at[g])
    regroup(k_ref, nat_s.at[Q_PER_KV], k3)
    regroup(v_ref, nat_s.at[Q_PER_KV + 1], vhead3.at[0])
    for r in range(RESIDUES):
        v = vhead3[0, r]
        vhead3[0, r] = jnp.where(e0, v, 0.0)
        vhead3[1, r] = jnp.where(e0, 0.0, v)

    def scores(q_idx, k_idx, bias):
        parts = []
        for g in range(Q_PER_KV):
            qg = q3.at[g][q_idx].reshape(ATT_BLK, LANES)
            parts += [jnp.where(e0, qg, zero), jnp.where(e0, zero, qg)]
        lhs = jnp.concatenate(parts, axis=0).astype(BF16)
        kb = k3[k_idx]
        nk = kb.shape[0] * kb.shape[1]
        kb = kb.reshape(nk, LANES).astype(BF16)
        s = lax.dot_general(lhs, kb, _NT, preferred_element_type=F32)
        s = (s.reshape(4, ATT_BLK, nk) + bias[None]).reshape(4 * ATT_BLK, nk)
        m = jnp.max(s, axis=1, keepdims=True)
        p = jnp.exp(s - m)
        mb = jnp.broadcast_to(m, (4 * ATT_BLK, LANES))
        key_lane = lax.broadcasted_iota(jnp.int32, (nk, LANES), 1) < HEAD_DIM
        ones_of = (jnp.where(key_lane, 1.0, 0.0), jnp.where(key_lane, 0.0, 1.0))
        rhs = jnp.concatenate(
            [jnp.concatenate([vhead3.at[e][k_idx].reshape(nk, LANES), ones_of[e]], axis=1) for e in range(2)],
            axis=0).astype(BF16)
        p2 = jnp.concatenate(
            [jnp.concatenate([p[(2 * g + e) * ATT_BLK:(2 * g + e + 1) * ATT_BLK] for e in range(2)], axis=1)
             for g in range(Q_PER_KV)], axis=0).astype(BF16)
        pvl = jnp.dot(p2, rhs, preferred_element_type=F32)
        out = []
        for g in range(Q_PER_KV):
            r0, r1, r2 = 2 * g * ATT_BLK, (2 * g + 1) * ATT_BLK, (2 * g + 2) * ATT_BLK
            blk = pvl[g * ATT_BLK:(g + 1) * ATT_BLK]
            out.append((jnp.where(e0, mb[r0:r1], mb[r1:r2]), blk[:, LANES:], blk[:, :LANES]))
        return out

    def process(units, first_group):
        state = [(m3.at[g], l3.at[g], acc3.at[g]) for g in range(Q_PER_KV)]
        new = [scores(*unit) for unit in units]
        for i, (q_idx, _, _) in enumerate(units):
            for g in range(Q_PER_KV):
                mg, lg, acc = new[i][g]
                if not first_group:
                    m_old, l_old, acc_old = (ref[q_idx].reshape(ATT_BLK, LANES) for ref in state[g])
                    m_new = jnp.maximum(m_old, mg)
                    a_old = jnp.exp(m_old - m_new)
                    a_new = jnp.exp(mg - m_new)
                    mg, lg, acc = m_new, a_old * l_old + a_new * lg, a_old * acc_old + a_new * acc
                for ref, val in zip(state[g], (mg, lg, acc)):
                    ref[q_idx] = val.reshape(ref[q_idx].shape)

    for gi, (d, bias_ref) in enumerate(zip(DILATIONS, (bias1_ref, bias4_ref, caus_ref))):
        blocks = seq // d // ATT_BLK
        lanes16 = RESIDUES // d
        depth = ATT_BLK // lanes16

        def unit(u, d=d, blocks=blocks, lanes16=lanes16, depth=depth, bias_ref=bias_ref):
            r, n = u // blocks, u % blocks
            lead = pl.ds(r, lanes16, stride=d)
            q_idx = (lead, pl.ds(pl.multiple_of(n * depth, depth), depth))
            if blocks == 1:
                return q_idx, q_idx, bias_ref[...]
            k_idx = (lead, pl.ds(pl.multiple_of(jnp.maximum(n - 1, 0) * depth, depth), 2 * depth))
            return q_idx, k_idx, bias_ref[jnp.minimum(n, 1)]

        def step(i, carry, unit=unit, first_group=(gi == 0)):
            process([unit(ATT_UNITS * i + j) for j in range(ATT_UNITS)], first_group)
            return carry

        lax.fori_loop(0, d * blocks // ATT_UNITS, step, 0)

    quarter = seq // REGROUP_STRIDE
    for g in range(Q_PER_KV):
        tmp, out = nat_s.at[Q_PER_KV + g], nat_s.at[g]
        for r in range(REGROUP_STRIDE):
            for a in range(RESIDUES // REGROUP_STRIDE):
                res = r + REGROUP_STRIDE * a
                tmp[pl.ds(r * quarter + a, per_res, stride=REGROUP_STRIDE), :] = acc3[g, res] / l3[g, res]
        for r in range(REGROUP_STRIDE):
            out[pl.ds(r, quarter, stride=REGROUP_STRIDE), :] = tmp[r * quarter:(r + 1) * quarter, :]
        o_ref[g] = out[...].astype(o_ref.dtype)


def _window_bias(d):
    lanes16 = RESIDUES // d

    def order(n):
        a, ll = np.divmod(np.arange(n), n // lanes16)
        return lanes16 * ll + a

    i = order(ATT_BLK)[:, None]
    if d == RESIDUES:
        return jnp.asarray(np.where(order(ATT_BLK)[None, :] <= i, 0.0, NEG_BIG).astype(np.float32))
    j = order(2 * ATT_BLK)[None, :]
    dist = i + ATT_BLK - j
    inner = np.where((dist >= 0) & (dist <= N_KEYS), 0.0, NEG_BIG)
    first = np.where(j <= i, 0.0, NEG_BIG)
    return jnp.asarray(np.stack([first, inner]).astype(np.float32))


def _attn_prompt(q4, k, v, n_seq, seq):
    biases = [_window_bias(d) for d in DILATIONS]
    slab = (RESIDUES, seq // RESIDUES, LANES)
    return pl.pallas_call(
        _attn_prompt_body,
        grid=(n_seq, 2),
        in_specs=[pl.BlockSpec((2, seq, LANES), lambda b, pp: (pp, b, 0)),
                  pl.BlockSpec((seq, LANES), lambda b, pp: (b, pp)),
                  pl.BlockSpec((seq, LANES), lambda b, pp: (b, pp))] + [_resident(x.shape) for x in biases],
        out_specs=pl.BlockSpec((2, seq, LANES), lambda b, pp: (pp, b, 0)),
        out_shape=jax.ShapeDtypeStruct(q4.shape, BF16),
        scratch_shapes=[pltpu.VMEM((2,) + slab, F32), pltpu.VMEM(slab, F32), pltpu.VMEM((2,) + slab, F32)]
                       + [pltpu.VMEM((2,) + slab, F32)] * 3 + [pltpu.VMEM((Q_PER_KV + 2, seq, LANES), F32)],
        compiler_params=_params(2),
        name="attn_prompt",
    )(q4, k, v, *biases)


def _attn_sample_pieces(q_ref, kn_ref, vn_ref, ck_ref, cv_ref, mult_ref, att_ref, ko_ref, vo_ref):
    t = kn_ref.shape[0]
    n_past = ck_ref.shape[2]
    tail = slice(n_past - LANES, n_past)

    def shift(c_ref, n_ref, o_ref, rows):
        is_new = lax.broadcasted_iota(jnp.int32, (LANES, LANES), 1) >= LANES - t
        moved = pltpu.roll(c_ref[0, rows, :], n_past - t, 1)
        o_ref[0, rows, :] = moved
        new_rows = jnp.concatenate([n_ref[:, rows], jnp.zeros((LANES - t, LANES), F32)], axis=0)
        o_ref[0, rows, tail] = jnp.where(is_new, pltpu.roll(new_rows.T, LANES - t, 1), moved[:, tail])

    def attend():
        mult = mult_ref[...]
        e0 = lax.broadcasted_iota(jnp.int32, (t, LANES), 1) < HEAD_DIM
        zero = jnp.zeros((t, LANES), F32)
        pairs = N_KV_HEADS // 2
        parts = []
        for pp in range(pairs):
            for g in range(Q_PER_KV):
                qg = q_ref[pp * Q_PER_KV + g]
                for masked in (jnp.where(e0, qg, zero), jnp.where(e0, zero, qg)):
                    parts.append(jnp.concatenate([masked if j == pp else zero for j in range(pairs)], axis=1))
        lhs = jnp.concatenate(parts, axis=0).astype(BF16)
        pad = jnp.zeros((LANES - t, D_KV), F32)
        kn = jnp.concatenate([kn_ref[...], pad], axis=0).astype(BF16)
        vn = jnp.concatenate([vn_ref[...], pad], axis=0).astype(BF16)
        s = jnp.concatenate([jnp.dot(lhs, ck_ref[0].astype(BF16), preferred_element_type=F32),
                             lax.dot_general(lhs, kn, _NT, preferred_element_type=F32)], axis=1)
        s = jnp.where(mult > 0.0, s, NEG_BIG)
        m = jnp.max(s, axis=1, keepdims=True)
        p = jnp.exp(s - m) * mult
        l = jnp.sum(p, axis=1, keepdims=True)
        pb = p.astype(BF16)
        pv = (lax.dot_general(pb[:, :n_past], cv_ref[0].astype(BF16), _NT, preferred_element_type=F32)
              + jnp.dot(pb[:, n_past:], vn, preferred_element_type=F32))
        o = pv / l
        for pp in range(pairs):
            lanes = slice(pp * LANES, (pp + 1) * LANES)
            for g in range(Q_PER_KV):
                r0 = (pp * Q_PER_KV + g) * 2 * t
                att_ref[pp * Q_PER_KV + g] = jnp.where(e0, o[r0:r0 + t, lanes], o[r0 + t:r0 + 2 * t, lanes])

    pieces = [attend]
    for pp in range(N_KV_HEADS // 2):
        rows = slice(pp * LANES, (pp + 1) * LANES)
        pieces += [functools.partial(shift, ck_ref, kn_ref, ko_ref, rows),
                   functools.partial(shift, cv_ref, vn_ref, vo_ref, rows)]
    return pieces


def _key_multiplicity(t, n_past):
    idx = np.concatenate([np.arange(n_past + t), np.full(LANES - t, 10 ** 9)])[None, :]
    dist = n_past + np.arange(t)[:, None] - idx
    mult = np.zeros(dist.shape, np.float32)
    for d in DILATIONS:
        mult += (dist >= 0) & (dist % d == 0) & (dist <= N_KEYS * d)
    return jnp.asarray(np.tile(mult, (2 * N_KV_HEADS, 1)))


def _attn_sample_specs(q4, k_new, v_new, cache_k, cache_v):
    n_seq, _, n_past = cache_k.shape
    t = k_new.shape[0] // n_seq
    mult = _key_multiplicity(t, n_past)
    cache_blk = pl.BlockSpec((1, D_KV, n_past), lambda b: (b, 0, 0))
    new_blk = pl.BlockSpec((t, D_KV), lambda b: (b, 0))
    q_blk = pl.BlockSpec((4, t, LANES), lambda b: (0, b, 0))
    return ([q4, k_new, v_new, cache_k, cache_v, mult],
            [q_blk, new_blk, new_blk, cache_blk, cache_blk, _resident(mult.shape)],
            [q_blk, cache_blk, cache_blk],
            [jax.ShapeDtypeStruct(q4.shape, F32),
             jax.ShapeDtypeStruct(cache_k.shape, F32), jax.ShapeDtypeStruct(cache_v.shape, F32)])


def _conv_silu(delayed, cw_ref, cb_ref, xc_s):
    for j in range(D_XBC // LANES):
        lanes = slice(j * LANES, (j + 1) * LANES)
        y = cb_ref[:, lanes]
        for tap in range(CONV_WIDTH):
            y = y + delayed(CONV_WIDTH - 1 - tap, lanes) * cw_ref[tap:tap + 1, lanes]
        xc_s[:, lanes] = _silu(y)


def _select_sum(x, sel):
    heads = x.shape[0]
    x = jnp.concatenate([x, jnp.zeros((LANES - heads, x.shape[1]), F32)], axis=0)
    total = None
    for _ in range(3):
        piece = x.astype(BF16)
        x = x - piece.astype(F32)
        part = jnp.dot(piece, sel, preferred_element_type=F32)
        total = part if total is None else total + part
    return total[0:heads]


def _to_columns(x_t):
    pad = jnp.zeros((LANES - x_t.shape[0], x_t.shape[1]), F32)
    return jnp.concatenate([x_t, pad], axis=0).T


def _ssd_scalars(dt_raw, dtb_ref, alog_ref, before, seg_end=None):
    dt_t = _softplus(dt_raw.T[0:N_SSM_HEADS] + dtb_ref[...])
    a_t = dt_t * (-jnp.exp(alog_ref[...]))
    a_cum_t = _select_sum(a_t, before)
    if seg_end is None:
        a_end_t = jnp.broadcast_to(a_cum_t[:, CHUNK - 1:CHUNK], a_cum_t.shape)
    else:
        a_end_t = _select_sum(a_cum_t, seg_end)
    to_end_t = jnp.exp(a_end_t - a_cum_t) * dt_t
    return dt_t, _to_columns(a_cum_t), a_cum_t, a_end_t, to_end_t


def _ssd_diag(xc_s, blk, cb, a_cum, a_cum_t, dt_t, seg_mask):
    e0 = lax.broadcasted_iota(jnp.int32, (CHUNK, LANES), 1) < SSM_HEAD_DIM
    x_pair = xc_s[:, blk * LANES:(blk + 1) * LANES].astype(BF16)
    out, grow = [], []
    for e in range(2):
        h = 2 * blk + e
        a_col = jnp.broadcast_to(a_cum[:, h:h + 1], (CHUNK, CHUNK))
        seg = a_col - jnp.broadcast_to(a_cum_t[h:h + 1, :], (CHUNK, CHUNK))
        w = cb * jnp.where(seg_mask, jnp.exp(seg), 0.0) * dt_t[h:h + 1, :]
        out.append(jnp.dot(w.astype(BF16), x_pair, preferred_element_type=F32))
        grow.append(jnp.exp(a_col))
    return jnp.where(e0, out[0], out[1]), jnp.where(e0, grow[0], grow[1])


def _ssd_finish(xc_s, z_ref, rows, y_parts, dsk_ref, nrm_ref, y_ref):
    blocks_per_group = D_SSM // SSM_GROUPS // LANES
    for g in range(SSM_GROUPS):
        gated = []
        for p in range(blocks_per_group):
            blk = g * blocks_per_group + p
            lanes = slice(blk * LANES, (blk + 1) * LANES)
            y = y_parts[blk] + dsk_ref[:, lanes] * xc_s[:, lanes]
            gated.append(y * _silu(z_ref[rows, lanes]))
        ss = sum(jnp.sum(y * y, axis=-1, keepdims=True) for y in gated)
        inv = lax.rsqrt(ss / (D_SSM // SSM_GROUPS) + RMS_EPS)
        for p in range(blocks_per_group):
            blk = g * blocks_per_group + p
            lanes = slice(blk * LANES, (blk + 1) * LANES)
            y_ref[rows, lanes] = (gated[p] * inv * nrm_ref[:, lanes]).astype(y_ref.dtype)


def _ssd_prompt_body(xbc_ref, z_ref, dt_ref, cw_ref, cb_ref, dtb_ref, alog_ref, dsk_ref, nrm_ref,
                     y_ref, st_ref, st_s, xc_bufs):
    seq = xbc_ref.shape[0]
    st_s[...] = jnp.zeros_like(st_s)
    row = lax.broadcasted_iota(jnp.int32, (CHUNK, CHUNK), 0)
    col = lax.broadcasted_iota(jnp.int32, (CHUNK, CHUNK), 1)
    causal = row >= col
    before = jnp.where(row <= col, 1.0, 0.0).astype(BF16)
    row8 = lax.broadcasted_iota(jnp.int32, (SUBLANES, LANES), 0)
    heads_per_group = N_SSM_HEADS // SSM_GROUPS
    group_rows = heads_per_group * SSM_HEAD_DIM

    def chunk(c, xc_s):
        rows = pl.ds(pl.multiple_of(c * CHUNK, CHUNK), CHUNK)
        prev_rows = pl.ds(pl.multiple_of(jnp.maximum(c * CHUNK - SUBLANES, 0), SUBLANES), SUBLANES)
        have_prev = c > 0

        def delayed(k, lanes):
            cur = xbc_ref[rows, lanes]
            if k == 0:
                return cur
            rolled = pltpu.roll(cur, k, 0)
            prev = jnp.where(have_prev, xbc_ref[prev_rows, lanes], 0.0)
            head = jnp.where(row8 < k, pltpu.roll(prev, k, 0), rolled[0:SUBLANES])
            return jnp.concatenate([head, rolled[SUBLANES:]], axis=0)

        _conv_silu(delayed, cw_ref, cb_ref, xc_s)
        dt_t, a_cum, a_cum_t, a_end_t, to_end_t = _ssd_scalars(dt_ref[rows, :], dtb_ref, alog_ref, before)
        x_t = xc_s[:, 0:D_SSM].T
        y_parts = []
        for g in range(SSM_GROUPS):
            bg = xc_s[:, D_SSM + g * D_STATE:D_SSM + (g + 1) * D_STATE].astype(BF16)
            cg = xc_s[:, D_SSM + (SSM_GROUPS + g) * D_STATE:D_SSM + (SSM_GROUPS + g + 1) * D_STATE].astype(BF16)
            cb = lax.dot_general(cg, bg, _NT, preferred_element_type=F32)
            grp = slice(g * group_rows, (g + 1) * group_rows)
            st_prev = st_s[grp, :]
            y_off = lax.dot_general(cg, st_prev.astype(BF16), _NT, preferred_element_type=F32)
            for p in range(group_rows // LANES):
                blk = g * (group_rows // LANES) + p
                y_diag, grow = _ssd_diag(xc_s, blk, cb, a_cum, a_cum_t, dt_t, causal)
                y_parts.append(y_diag + y_off[:, p * LANES:(p + 1) * LANES] * grow)
            xw = []
            for h in range(g * heads_per_group, (g + 1) * heads_per_group):
                hr = slice(h * SSM_HEAD_DIM, (h + 1) * SSM_HEAD_DIM)
                xw.append((x_t[hr, :] * to_end_t[h:h + 1, :]).astype(BF16))
            new = jnp.dot(jnp.concatenate(xw, axis=0), bg, preferred_element_type=F32)
            for i, h in enumerate(range(g * heads_per_group, (g + 1) * heads_per_group)):
                hr = slice(h * SSM_HEAD_DIM, (h + 1) * SSM_HEAD_DIM)
                lr = slice(i * SSM_HEAD_DIM, (i + 1) * SSM_HEAD_DIM)
                decay = jnp.exp(jnp.broadcast_to(a_end_t[h:h + 1, 0:1], (SSM_HEAD_DIM, D_STATE)))
                st_s[hr, :] = decay * st_prev[lr, :] + new[lr, :]
        _ssd_finish(xc_s, z_ref, rows, y_parts, dsk_ref, nrm_ref, y_ref)

    def step(i, carry):
        for j in range(SSD_CHUNKS_PER_STEP):
            chunk(SSD_CHUNKS_PER_STEP * i + j, xc_bufs.at[j])
        return carry

    lax.fori_loop(0, seq // CHUNK // SSD_CHUNKS_PER_STEP, step, 0)
    st_ref[...] = st_s[...]


def _ssd_prompt(xbc, z, dt_raw, n_seq, seq, consts):
    rows = xbc.shape[0]
    blk = lambda w: pl.BlockSpec((seq, w), lambda b: (b, 0))
    state_blk = pl.BlockSpec((None, D_SSM, D_STATE), lambda b: (b, 0, 0))
    return pl.pallas_call(
        _ssd_prompt_body,
        grid=(n_seq,),
        in_specs=[blk(D_XBC), blk(D_SSM), blk(LANES)] + [_resident(c.shape) for c in consts],
        out_specs=[blk(D_SSM), state_blk],
        out_shape=[jax.ShapeDtypeStruct((rows, D_SSM), BF16),
                   jax.ShapeDtypeStruct((n_seq, D_SSM, D_STATE), F32)],
        scratch_shapes=[pltpu.VMEM((D_SSM, D_STATE), F32),
                        pltpu.VMEM((SSD_CHUNKS_PER_STEP, CHUNK, D_XBC), F32)],
        compiler_params=_params(1),
        name="ssd_prompt",
    )(xbc, z, dt_raw, *consts)


def _ssd_sample_body(xbc_ref, pre_ref, z_ref, dt_ref, h0_ref, cw_ref, cb_ref, dtb_ref, alog_ref, dsk_ref,
                     nrm_ref, y_ref, st_ref, xc_s, yoff_s, aend_s):
    t = CHUNK // SAMPLE_SEQS
    row = lax.broadcasted_iota(jnp.int32, (CHUNK, CHUNK), 0)
    col = lax.broadcasted_iota(jnp.int32, (CHUNK, CHUNK), 1)
    same_seq = (row // t) == (col // t)
    seg_mask = same_seq & (row >= col)
    before = jnp.where(same_seq & (row <= col), 1.0, 0.0).astype(BF16)
    seg_end = jnp.where(row == (col // t) * t + (t - 1), 1.0, 0.0).astype(BF16)
    step = lax.broadcasted_iota(jnp.int32, (CHUNK, LANES), 0) % t
    rows = pl.ds(0, CHUNK)
    heads_per_group = N_SSM_HEADS // SSM_GROUPS
    group_rows = heads_per_group * SSM_HEAD_DIM

    def delayed(k, lanes):
        cur = xbc_ref[:, lanes]
        if k == 0:
            return cur
        return jnp.where(step < k, pltpu.roll(pre_ref[:, lanes], (k - t) % CHUNK, 0), pltpu.roll(cur, k, 0))

    _conv_silu(delayed, cw_ref, cb_ref, xc_s)
    dt_t, a_cum, a_cum_t, a_end_t, to_end_t = _ssd_scalars(dt_ref[...], dtb_ref, alog_ref, before, seg_end)
    aend_s[...] = _to_columns(a_end_t)
    x_t = xc_s[:, 0:D_SSM].T
    col_seq = lax.broadcasted_iota(jnp.int32, (SSM_HEAD_DIM, CHUNK), 1) // t
    y_parts = []
    for g in range(SSM_GROUPS):
        bg = xc_s[:, D_SSM + g * D_STATE:D_SSM + (g + 1) * D_STATE].astype(BF16)
        cg_lanes = slice(D_SSM + (SSM_GROUPS + g) * D_STATE, D_SSM + (SSM_GROUPS + g + 1) * D_STATE)
        cg = xc_s[:, cg_lanes].astype(BF16)
        cb = lax.dot_general(cg, bg, _NT, preferred_element_type=F32)
        grp = slice(g * group_rows, (g + 1) * group_rows)
        xw = []
        for h in range(g * heads_per_group, (g + 1) * heads_per_group):
            hr = slice(h * SSM_HEAD_DIM, (h + 1) * SSM_HEAD_DIM)
            xw.append(x_t[hr, :] * to_end_t[h:h + 1, :])

        def per_seq(b, carry, g=g, bg=bg, cg_lanes=cg_lanes, grp=grp, xw=xw):
            seq_rows = pl.ds(pl.multiple_of(b * t, t), t)
            h_prev = h0_ref[b, grp, :]
            cg_b = xc_s[seq_rows, cg_lanes].astype(BF16)
            yoff_s[seq_rows, g * group_rows:(g + 1) * group_rows] = lax.dot_general(
                cg_b, h_prev.astype(BF16), _NT, preferred_element_type=F32)
            mine = col_seq == b
            xw_b = jnp.concatenate([jnp.where(mine, w, 0.0).astype(BF16) for w in xw], axis=0)
            new = jnp.dot(xw_b, bg, preferred_element_type=F32)
            a_last = aend_s[pl.ds(b * t, 1), :]
            for i, h in enumerate(range(g * heads_per_group, (g + 1) * heads_per_group)):
                lr = slice(i * SSM_HEAD_DIM, (i + 1) * SSM_HEAD_DIM)
                decay = jnp.exp(jnp.broadcast_to(a_last[:, h:h + 1], (SSM_HEAD_DIM, D_STATE)))
                st_ref[b, h * SSM_HEAD_DIM:(h + 1) * SSM_HEAD_DIM, :] = decay * h_prev[lr, :] + new[lr, :]
            return carry

        lax.fori_loop(0, SAMPLE_SEQS, per_seq, 0, unroll=4)
        for p in range(group_rows // LANES):
            blk = g * (group_rows // LANES) + p
            lanes = slice(blk * LANES, (blk + 1) * LANES)
            y_diag, grow = _ssd_diag(xc_s, blk, cb, a_cum, a_cum_t, dt_t, seg_mask)
            y_parts.append(y_diag + yoff_s[:, lanes] * grow)
    _ssd_finish(xc_s, z_ref, rows, y_parts, dsk_ref, nrm_ref, y_ref)


def _ssd_sample(xbc, prefix_tiles, z, dt_raw, h0, consts):
    rows = xbc.shape[0]
    n_seq = h0.shape[0]
    blk = lambda w: pl.BlockSpec((CHUNK, w), lambda i: (i, 0))
    state_blk = pl.BlockSpec((SAMPLE_SEQS, D_SSM, D_STATE), lambda i: (i, 0, 0))
    return pl.pallas_call(
        _ssd_sample_body,
        grid=(rows // CHUNK,),
        in_specs=[blk(D_XBC), blk(D_XBC), blk(D_SSM), blk(LANES), state_blk]
                 + [_resident(c.shape) for c in consts],
        out_specs=[blk(D_SSM), state_blk],
        out_shape=[jax.ShapeDtypeStruct((rows, D_SSM), BF16),
                   jax.ShapeDtypeStruct((n_seq, D_SSM, D_STATE), F32)],
        scratch_shapes=[pltpu.VMEM((CHUNK, D_XBC), F32), pltpu.VMEM((CHUNK, D_SSM), F32),
                        pltpu.VMEM((CHUNK, LANES), F32)],
        compiler_params=_params(1),
        name="ssd_sample",
    )(xbc, prefix_tiles, z, dt_raw, h0, *consts)


def _out_mlp_body(x_ref, att_ref, ssm_ref, woa_ref, wos_ref, g_ref, wu_ref, wd_ref, gf_ref, o_ref, side_work=()):
    side_work = list(side_work)
    att = jnp.concatenate([att_ref[j] for j in range(att_ref.shape[0])], axis=1).astype(BF16)
    h = (x_ref[...] + jnp.dot(att, woa_ref[...], preferred_element_type=F32)
         + jnp.dot(ssm_ref[...], wos_ref[...], preferred_element_type=F32))
    hn = _rms(h, g_ref[...]).astype(BF16)
    mlp = None
    for c in range(D_FF // FF_TILE):
        if side_work:
            side_work.pop(0)()
        cols = slice(c * FF_TILE, (c + 1) * FF_TILE)
        u = jnp.maximum(jnp.dot(hn, wu_ref[:, cols], preferred_element_type=F32), 0.0)
        down = jnp.dot((u * u).astype(BF16), wd_ref[cols, :], preferred_element_type=F32)
        mlp = down if mlp is None else mlp + down
    for work in side_work:
        work()
    o_ref[...] = _rms(h + mlp, gf_ref[...])


def _out_mlp_attn_body(*refs):
    n_mlp, n_att = 9, 6
    o_ref = refs[n_mlp + n_att]
    pieces = _attn_sample_pieces(*refs[n_mlp:n_mlp + n_att], *refs[n_mlp + n_att + 1:])
    _out_mlp_body(*refs[:n_mlp], o_ref, side_work=pieces)


def _out_mlp(x2d, att4, ssm, w_out_att, w_out_ssm, norm_mlp, w_up, w_down, norm_final, sample=None):
    rows = x2d.shape[0]
    tm = min(ROW_TILE, rows) if sample is None else rows // sample[3].shape[0]
    row_blk = lambda w: pl.BlockSpec((tm, w), lambda i: (i, 0))
    operands = [x2d, att4, ssm, w_out_att, w_out_ssm, norm_mlp, w_up, w_down, norm_final]
    in_specs = [row_blk(D_MODEL), pl.BlockSpec((4, tm, LANES), lambda i: (0, i, 0)), row_blk(D_SSM),
                _resident((D_ATT, D_MODEL)), _resident((D_SSM, D_MODEL)), _resident((1, D_MODEL)),
                _resident((D_MODEL, D_FF)), _resident((D_FF, D_MODEL)), _resident((1, D_MODEL))]
    out_specs = [row_blk(D_MODEL)]
    out_shape = [jax.ShapeDtypeStruct((rows, D_MODEL), F32)]
    body = _out_mlp_body
    if sample is not None:
        att_operands, att_in, att_out, att_shape = _attn_sample_specs(*sample)
        assert rows == tm * sample[3].shape[0] and tm % SUBLANES == 0
        operands += att_operands
        in_specs += att_in
        out_specs += att_out
        out_shape += att_shape
        body = _out_mlp_attn_body
    out = pl.pallas_call(
        body,
        grid=(rows // tm,),
        in_specs=in_specs,
        out_specs=out_specs,
        out_shape=out_shape,
        compiler_params=_params(1),
        name="out_mlp",
    )(*operands)
    return out[0] if sample is None else out


def _q_permutation():
    perm = []
    for pp in range(N_KV_HEADS // 2):
        for g in range(Q_PER_KV):
            for e in range(2):
                base = (2 * pp + e) * Q_PER_KV * HEAD_DIM + g * HEAD_DIM
                perm += list(range(base, base + HEAD_DIM))
    return np.asarray(perm, np.int32)


def _to_lanes(buf):
    n, pos = buf.shape[:2]
    return jnp.transpose(buf, (0, 2, 3, 1)).reshape(n, D_KV, pos)


def _from_lanes(buf_t):
    n, _, pos = buf_t.shape
    return jnp.transpose(buf_t.reshape(n, N_KV_HEADS, HEAD_DIM, pos), (0, 3, 1, 2))[None]


def _head_rows(v):
    return jnp.broadcast_to(v.astype(F32)[:, None], (v.shape[0], LANES))


def kernel(x_prompt, x_sample, cache_k, cache_v, state_conv, state_ssm, w_in, w_out, conv_w, conv_b, dt_bias, a_log, d_skip, ssm_norm, norm_mix, norm_mlp, w_up, w_down, norm_final):
    depth = w_in.shape[0]
    assert depth == 1, "single-layer step"
    n_p, seq, _ = x_prompt.shape
    n_s, t_new, _ = x_sample.shape
    n_past = cache_k.shape[2]
    assert seq == MAX_WINDOW and n_past == MAX_WINDOW and t_new == SUBLANES and n_s % SAMPLE_SEQS == 0

    perm = _q_permutation()
    w = w_in[0]
    w_pad = jnp.concatenate(
        [w[:, perm], w[:, D_ATT:], jnp.zeros((D_MODEL, D_IN_PAD - w.shape[1]), F32)], axis=1).astype(BF16)
    w_out_att = w_out[0][perm, :].astype(BF16)
    w_out_ssm = w_out[0][D_ATT:, :].astype(BF16)
    w_up_b, w_down_b = w_up[0].astype(BF16), w_down[0].astype(BF16)
    g_mix, g_mlp, g_fin = (v.reshape(1, D_MODEL) for v in (norm_mix[0], norm_mlp[0], norm_final))
    ssd_consts = (conv_w[0], conv_b[0].reshape(1, D_XBC), _head_rows(dt_bias[0]), _head_rows(a_log[0]),
                  jnp.repeat(d_skip[0], SSM_HEAD_DIM).reshape(1, D_SSM), ssm_norm[0].reshape(1, D_SSM))

    xp = x_prompt.reshape(n_p * seq, D_MODEL)
    q4, k, v, z, xbc, dt_raw, k_t, v_t = _in_proj(xp, g_mix, w_pad, _rope_table(jnp.arange(seq)), seq=seq)
    att4 = _attn_prompt(q4, k, v, n_p, seq)
    y_ssm, st_p = _ssd_prompt(xbc, z, dt_raw, n_p, seq, ssd_consts)
    k_prompt, v_prompt = _from_lanes(k_t), _from_lanes(v_t)
    conv_prompt = xbc.reshape(1, n_p, seq, D_XBC)[:, :, seq - (CONV_WIDTH - 1):]

    xs = x_sample.reshape(n_s * t_new, D_MODEL)
    rows_tile = min(ROW_TILE, n_s * t_new)
    pos = PAST_LEN + (jnp.arange(rows_tile) % t_new)
    q4s, ks, vs, zs, xbcs, dts = _in_proj(xs, g_mix, w_pad, _rope_table(pos))
    y_prompt, att4s, k_sample, v_sample = _out_mlp(
        xp, att4, y_ssm, w_out_att, w_out_ssm, g_mlp, w_up_b, w_down_b, g_fin,
        sample=(q4s, ks, vs, _to_lanes(cache_k[0]), _to_lanes(cache_v[0])))
    prefix_tiles = jnp.pad(state_conv[0], ((0, 0), (t_new - (CONV_WIDTH - 1), 0), (0, 0))).reshape(n_s * t_new, D_XBC)
    y_ssm_s, st_s = _ssd_sample(xbcs, prefix_tiles, zs, dts, state_ssm[0].reshape(n_s, D_SSM, D_STATE), ssd_consts)
    y_sample = _out_mlp(xs, att4s, y_ssm_s, w_out_att, w_out_ssm, g_mlp, w_up_b, w_down_b, g_fin)
    conv_sample = xbcs.reshape(1, n_s, t_new, D_XBC)[:, :, t_new - (CONV_WIDTH - 1):]

    return (y_prompt.reshape(n_p, seq, D_MODEL), y_sample.reshape(n_s, t_new, D_MODEL),
            k_prompt, v_prompt, conv_prompt,
            st_p.reshape(1, n_p, N_SSM_HEADS, SSM_HEAD_DIM, D_STATE),
            _from_lanes(k_sample), _from_lanes(v_sample), conv_sample,
            st_s.reshape(1, n_s, N_SSM_HEADS, SSM_HEAD_DIM, D_STATE))
```

```python
import functools

import numpy as np
import jax
import jax.numpy as jnp
from jax import lax
from jax.experimental import pallas as pl
from jax.experimental.pallas import tpu as pltpu

F32 = jnp.float32
BF16 = jnp.bfloat16

LANES = 128
SUBLANES = 8

D_MODEL = 1024
HEAD_DIM = 64
N_KV_HEADS = 4
Q_PER_KV = 2
D_ATT = 512
D_KV = 256
ROT_HALF = 8
ROPE_THETA = 500000.0
DILATIONS = (1, 4, 16)
N_KEYS = 128
MAX_WINDOW = 2048
ATT_BLK = 128
ATT_UNITS = 8
REGROUP_STRIDE = 4
RESIDUES = 16
PAST_LEN = 8192
ATT_SCALE = HEAD_DIM ** -0.5
NEG_BIG = -1e30
N_SSM_HEADS = 8
SSM_HEAD_DIM = 64
D_SSM = 512
SSM_GROUPS = 2
D_STATE = 128
CONV_WIDTH = 4
CHUNK = 128
D_XBC = 1024
D_FF = 4096
RMS_EPS = 1e-5
D_IN_MAIN = D_ATT + 2 * D_KV + D_SSM + D_XBC
D_IN_PAD = D_IN_MAIN + LANES

VMEM_LIMIT = 56 * 1024 * 1024
ROW_TILE = 512
FF_TILE = 1024
SSD_CHUNKS_PER_STEP = 4
SAMPLE_SEQS = CHUNK // 8

_NT = (((1,), (1,)), ((), ()))


def _params(n_axes):
    return pltpu.CompilerParams(dimension_semantics=("arbitrary",) * n_axes,
                                vmem_limit_bytes=VMEM_LIMIT)


def _resident(shape):
    return pl.BlockSpec(shape, lambda *_: (0,) * len(shape), pipeline_mode=pl.Buffered(1))


def _rms(x, g):
    return x * lax.rsqrt(jnp.mean(x * x, axis=-1, keepdims=True) + RMS_EPS) * g


def _silu(x):
    half = 0.5 * x
    return half + half * jnp.tanh(half)


def _softplus(x):
    return jnp.maximum(x, 0.0) + jnp.log1p(jnp.exp(-jnp.abs(x)))


def _in_proj_body(x_ref, g_ref, w_ref, rope_ref, q_ref, k_ref, v_ref, z_ref, xbc_ref, dt_ref, *kv_t_refs):
    xb = _rms(x_ref[...], g_ref[...]).astype(BF16)
    tm = xb.shape[0]
    first_half = (lax.broadcasted_iota(jnp.int32, (tm, LANES), 1) % HEAD_DIM) < ROT_HALF

    def proj(lo, hi):
        return jnp.dot(xb, w_ref[:, lo:hi], preferred_element_type=F32)

    def rope(u, cos, sin):
        partner = jnp.where(first_half, pltpu.roll(u, LANES - ROT_HALF, 1), pltpu.roll(u, ROT_HALF, 1))
        return u * cos + partner * sin

    cq, sq = rope_ref[:, 0:128], rope_ref[:, 128:256]
    ck, sk = rope_ref[:, 256:384], rope_ref[:, 384:512]
    q = proj(0, D_ATT)
    q = [rope(q[:, j * LANES:(j + 1) * LANES], cq, sq) for j in range(D_ATT // LANES)]
    low_half = lax.broadcasted_iota(jnp.int32, (tm, LANES), 1) < HEAD_DIM
    for pp in range(N_KV_HEADS // 2):
        head0, head1 = q[2 * pp], q[2 * pp + 1]
        q_ref[Q_PER_KV * pp] = jnp.where(low_half, head0, pltpu.roll(head1, HEAD_DIM, 1))
        q_ref[Q_PER_KV * pp + 1] = jnp.where(low_half, pltpu.roll(head0, HEAD_DIM, 1), head1)
    k = proj(D_ATT, D_ATT + D_KV)
    v = proj(D_ATT + D_KV, D_ATT + 2 * D_KV)
    v_ref[...] = v
    for j in range(D_KV // LANES):
        lanes = slice(j * LANES, (j + 1) * LANES)
        kj = rope(k[:, lanes], ck, sk)
        k_ref[:, lanes] = kj
        if kv_t_refs:
            kv_t_refs[0][lanes, :] = kj.T
            kv_t_refs[1][lanes, :] = v[:, lanes].T
    z_ref[...] = proj(D_ATT + 2 * D_KV, D_ATT + 2 * D_KV + D_SSM)
    xbc_ref[...] = proj(D_ATT + 2 * D_KV + D_SSM, D_IN_MAIN)
    dt_ref[...] = proj(D_IN_MAIN, D_IN_PAD)


def _in_proj(x2d, norm_mix, w_pad, rope_tab, seq=None):
    rows = x2d.shape[0]
    tm = min(ROW_TILE, rows)
    n_tab = rope_tab.shape[0] // tm
    row_blk = lambda w: pl.BlockSpec((tm, w), lambda i: (i, 0))
    out_specs = [pl.BlockSpec((4, tm, LANES), lambda i: (0, i, 0)),
                 row_blk(D_KV), row_blk(D_KV), row_blk(D_SSM), row_blk(D_XBC), row_blk(LANES)]
    out_shape = [jax.ShapeDtypeStruct((4, rows, LANES), F32),
                 jax.ShapeDtypeStruct((rows, D_KV), F32), jax.ShapeDtypeStruct((rows, D_KV), F32),
                 jax.ShapeDtypeStruct((rows, D_SSM), F32), jax.ShapeDtypeStruct((rows, D_XBC), F32),
                 jax.ShapeDtypeStruct((rows, LANES), F32)]
    if seq is not None:
        per_seq = seq // tm
        t_blk = pl.BlockSpec((None, D_KV, tm), lambda i: (i // per_seq, 0, i % per_seq))
        out_specs += [t_blk, t_blk]
        out_shape += [jax.ShapeDtypeStruct((rows // seq, D_KV, seq), F32)] * 2
    return pl.pallas_call(
        _in_proj_body,
        grid=(rows // tm,),
        in_specs=[row_blk(D_MODEL), _resident((1, D_MODEL)), _resident((D_MODEL, D_IN_PAD)),
                  pl.BlockSpec((tm, 4 * LANES), lambda i: (i % n_tab, 0))],
        out_specs=out_specs,
        out_shape=out_shape,
        compiler_params=_params(1),
        name="in_proj",
    )(x2d, norm_mix, w_pad, rope_tab)


def _rope_table(pos):
    pos = np.asarray(pos, np.float64)
    inv = ROPE_THETA ** (-np.arange(0, 2 * ROT_HALF, 2, dtype=np.float64) / (2 * ROT_HALF))
    ang = pos[:, None] * inv[None, :]
    cos, sin = np.cos(ang), np.sin(ang)
    ones = np.ones((pos.shape[0], HEAD_DIM - 2 * ROT_HALF))
    cos_h = np.concatenate([cos, cos, ones], axis=1)
    sin_h = np.concatenate([-sin, sin, 0.0 * ones], axis=1)
    cos_l, sin_l = np.tile(cos_h, (1, 2)), np.tile(sin_h, (1, 2))
    table = np.concatenate([cos_l * ATT_SCALE, sin_l * ATT_SCALE, cos_l, sin_l], axis=1)
    return jnp.asarray(table.astype(np.float32))


def _attn_prompt_body(q_ref, k_ref, v_ref, bias1_ref, bias4_ref, caus_ref, o_ref,
                      q3, k3, vhead3, m3, l3, acc3, nat_s):
    seq = k_ref.shape[0]
    per_res = seq // RESIDUES
    e0 = lax.broadcasted_iota(jnp.int32, (ATT_BLK, LANES), 1) < HEAD_DIM
    zero = jnp.zeros((ATT_BLK, LANES), F32)

    def regroup(src, tmp, dst):
        quarter = seq // REGROUP_STRIDE
        for r in range(REGROUP_STRIDE):
            tmp[r * quarter:(r + 1) * quarter, :] = src[pl.ds(r, quarter, stride=REGROUP_STRIDE), :]
        for r in range(REGROUP_STRIDE):
            for a in range(RESIDUES // REGROUP_STRIDE):
                dst[r + REGROUP_STRIDE * a] = tmp[pl.ds(r * quarter + a, per_res, stride=REGROUP_STRIDE), :]

    for g in range(Q_PER_KV):
        regroup(q_ref.at[g], nat_s.at[g], q3.at[g])
    regroup(k_ref, nat_s.at[Q_PER_KV], k3)
    regroup(v_ref, nat_s.at[Q_PER_KV + 1], vhead3.at[0])
    for r in range(RESIDUES):
        v = vhead3[0, r]
        vhead3[0, r] = jnp.where(e0, v, 0.0)
        vhead3[1, r] = jnp.where(e0, 0.0, v)

    def scores(q_idx, k_idx, bias):
        parts = []
        for g in range(Q_PER_KV):
            qg = q3.at[g][q_idx].reshape(ATT_BLK, LANES)
            parts += [jnp.where(e0, qg, zero), jnp.where(e0, zero, qg)]
        lhs = jnp.concatenate(parts, axis=0).astype(BF16)
        kb = k3[k_idx]
        nk = kb.shape[0] * kb.shape[1]
        kb = kb.reshape(nk, LANES).astype(BF16)
        s = lax.dot_general(lhs, kb, _NT, preferred_element_type=F32)
        s = (s.reshape(4, ATT_BLK, nk) + bias[None]).reshape(4 * ATT_BLK, nk)
        m = jnp.max(s, axis=1, keepdims=True)
        p = jnp.exp(s - m)
        mb = jnp.broadcast_to(m, (4 * ATT_BLK, LANES))
        key_lane = lax.broadcasted_iota(jnp.int32, (nk, LANES), 1) < HEAD_DIM
        ones_of = (jnp.where(key_lane, 1.0, 0.0), jnp.where(key_lane, 0.0, 1.0))
        rhs = jnp.concatenate(
            [jnp.concatenate([vhead3.at[e][k_idx].reshape(nk, LANES), ones_of[e]], axis=1) for e in range(2)],
            axis=0).astype(BF16)
        p2 = jnp.concatenate(
            [jnp.concatenate([p[(2 * g + e) * ATT_BLK:(2 * g + e + 1) * ATT_BLK] for e in range(2)], axis=1)
             for g in range(Q_PER_KV)], axis=0).astype(BF16)
        pvl = jnp.dot(p2, rhs, preferred_element_type=F32)
        out = []
        for g in range(Q_PER_KV):
            r0, r1, r2 = 2 * g * ATT_BLK, (2 * g + 1) * ATT_BLK, (2 * g + 2) * ATT_BLK
            blk = pvl[g * ATT_BLK:(g + 1) * ATT_BLK]
            out.append((jnp.where(e0, mb[r0:r1], mb[r1:r2]), blk[:, LANES:], blk[:, :LANES]))
        return out

    def process(units, first_group):
        state = [(m3.at[g], l3.at[g], acc3.at[g]) for g in range(Q_PER_KV)]
        new = [scores(*unit) for unit in units]
        for i, (q_idx, _, _) in enumerate(units):
            for g in range(Q_PER_KV):
                mg, lg, acc = new[i][g]
                if not first_group:
                    m_old, l_old, acc_old = (ref[q_idx].reshape(ATT_BLK, LANES) for ref in state[g])
                    m_new = jnp.maximum(m_old, mg)
                    a_old = jnp.exp(m_old - m_new)
                    a_new = jnp.exp(mg - m_new)
                    mg, lg, acc = m_new, a_old * l_old + a_new * lg, a_old * acc_old + a_new * acc
                for ref, val in zip(state[g], (mg, lg, acc)):
                    ref[q_idx] = val.reshape(ref[q_idx].shape)

    for gi, (d, bias_ref) in enumerate(zip(DILATIONS, (bias1_ref, bias4_ref, caus_ref))):
        blocks = seq // d // ATT_BLK
        lanes16 = RESIDUES // d
        depth = ATT_BLK // lanes16

        def unit(u, d=d, blocks=blocks, lanes16=lanes16, depth=depth, bias_ref=bias_ref):
            r, n = u // blocks, u % blocks
            lead = pl.ds(r, lanes16, stride=d)
            q_idx = (lead, pl.ds(pl.multiple_of(n * depth, depth), depth))
            if blocks == 1:
                return q_idx, q_idx, bias_ref[...]
            k_idx = (lead, pl.ds(pl.multiple_of(jnp.maximum(n - 1, 0) * depth, depth), 2 * depth))
            return q_idx, k_idx, bias_ref[jnp.minimum(n, 1)]

        def step(i, carry, unit=unit, first_group=(gi == 0)):
            process([unit(ATT_UNITS * i + j) for j in range(ATT_UNITS)], first_group)
            return carry

        lax.fori_loop(0, d * blocks // ATT_UNITS, step, 0)

    quarter = seq // REGROUP_STRIDE
    for g in range(Q_PER_KV):
        tmp, out = nat_s.at[Q_PER_KV + g], nat_s.at[g]
        for r in range(REGROUP_STRIDE):
            for a in range(RESIDUES // REGROUP_STRIDE):
                res = r + REGROUP_STRIDE * a
                tmp[pl.ds(r * quarter + a, per_res, stride=REGROUP_STRIDE), :] = acc3[g, res] / l3[g, res]
        for r in range(REGROUP_STRIDE):
            out[pl.ds(r, quarter, stride=REGROUP_STRIDE), :] = tmp[r * quarter:(r + 1) * quarter, :]
        o_ref[g] = out[...].astype(o_ref.dtype)


def _window_bias(d):
    lanes16 = RESIDUES // d

    def order(n):
        a, ll = np.divmod(np.arange(n), n // lanes16)
        return lanes16 * ll + a

    i = order(ATT_BLK)[:, None]
    if d == RESIDUES:
        return jnp.asarray(np.where(order(ATT_BLK)[None, :] <= i, 0.0, NEG_BIG).astype(np.float32))
    j = order(2 * ATT_BLK)[None, :]
    dist = i + ATT_BLK - j
    inner = np.where((dist >= 0) & (dist <= N_KEYS), 0.0, NEG_BIG)
    first = np.where(j <= i, 0.0, NEG_BIG)
    return jnp.asarray(np.stack([first, inner]).astype(np.float32))


def _attn_prompt(q4, k, v, n_seq, seq):
    biases = [_window_bias(d) for d in DILATIONS]
    slab = (RESIDUES, seq // RESIDUES, LANES)
    return pl.pallas_call(
        _attn_prompt_body,
        grid=(n_seq, 2),
        in_specs=[pl.BlockSpec((2, seq, LANES), lambda b, pp: (pp, b, 0)),
                  pl.BlockSpec((seq, LANES), lambda b, pp: (b, pp)),
                  pl.BlockSpec((seq, LANES), lambda b, pp: (b, pp))] + [_resident(x.shape) for x in biases],
        out_specs=pl.BlockSpec((2, seq, LANES), lambda b, pp: (pp, b, 0)),
        out_shape=jax.ShapeDtypeStruct(q4.shape, BF16),
        scratch_shapes=[pltpu.VMEM((2,) + slab, F32), pltpu.VMEM(slab, F32), pltpu.VMEM((2,) + slab, F32)]
                       + [pltpu.VMEM((2,) + slab, F32)] * 3 + [pltpu.VMEM((Q_PER_KV + 2, seq, LANES), F32)],
        compiler_params=_params(2),
        name="attn_prompt",
    )(q4, k, v, *biases)


def _attn_sample_pieces(q_ref, kn_ref, vn_ref, ck_ref, cv_ref, mult_ref, att_ref, ko_ref, vo_ref):
    t = kn_ref.shape[0]
    n_past = ck_ref.shape[2]
    tail = slice(n_past - LANES, n_past)

    def shift(c_ref, n_ref, o_ref, rows):
        is_new = lax.broadcasted_iota(jnp.int32, (LANES, LANES), 1) >= LANES - t
        moved = pltpu.roll(c_ref[0, rows, :], n_past - t, 1)
        o_ref[0, rows, :] = moved
        new_rows = jnp.concatenate([n_ref[:, rows], jnp.zeros((LANES - t, LANES), F32)], axis=0)
        o_ref[0, rows, tail] = jnp.where(is_new, pltpu.roll(new_rows.T, LANES - t, 1), moved[:, tail])

    def attend():
        mult = mult_ref[...]
        e0 = lax.broadcasted_iota(jnp.int32, (t, LANES), 1) < HEAD_DIM
        zero = jnp.zeros((t, LANES), F32)
        pairs = N_KV_HEADS // 2
        parts = []
        for pp in range(pairs):
            for g in range(Q_PER_KV):
                qg = q_ref[pp * Q_PER_KV + g]
                for masked in (jnp.where(e0, qg, zero), jnp.where(e0, zero, qg)):
                    parts.append(jnp.concatenate([masked if j == pp else zero for j in range(pairs)], axis=1))
        lhs = jnp.concatenate(parts, axis=0).astype(BF16)
        pad = jnp.zeros((LANES - t, D_KV), F32)
        kn = jnp.concatenate([kn_ref[...], pad], axis=0).astype(BF16)
        vn = jnp.concatenate([vn_ref[...], pad], axis=0).astype(BF16)
        s = jnp.concatenate([jnp.dot(lhs, ck_ref[0].astype(BF16), preferred_element_type=F32),
                             lax.dot_general(lhs, kn, _NT, preferred_element_type=F32)], axis=1)
        s = jnp.where(mult > 0.0, s, NEG_BIG)
        m = jnp.max(s, axis=1, keepdims=True)
        p = jnp.exp(s - m) * mult
        l = jnp.sum(p, axis=1, keepdims=True)
        pb = p.astype(BF16)
        pv = (lax.dot_general(pb[:, :n_past], cv_ref[0].astype(BF16), _NT, preferred_element_type=F32)
              + jnp.dot(pb[:, n_past:], vn, preferred_element_type=F32))
        o = pv / l
        for pp in range(pairs):
            lanes = slice(pp * LANES, (pp + 1) * LANES)
            for g in range(Q_PER_KV):
                r0 = (pp * Q_PER_KV + g) * 2 * t
                att_ref[pp * Q_PER_KV + g] = jnp.where(e0, o[r0:r0 + t, lanes], o[r0 + t:r0 + 2 * t, lanes])

    pieces = [attend]
    for pp in range(N_KV_HEADS // 2):
        rows = slice(pp * LANES, (pp + 1) * LANES)
        pieces += [functools.partial(shift, ck_ref, kn_ref, ko_ref, rows),
                   functools.partial(shift, cv_ref, vn_ref, vo_ref, rows)]
    return pieces


def _key_multiplicity(t, n_past):
    idx = np.concatenate([np.arange(n_past + t), np.full(LANES - t, 10 ** 9)])[None, :]
    dist = n_past + np.arange(t)[:, None] - idx
    mult = np.zeros(dist.shape, np.float32)
    for d in DILATIONS:
        mult += (dist >= 0) & (dist % d == 0) & (dist <= N_KEYS * d)
    return jnp.asarray(np.tile(mult, (2 * N_KV_HEADS, 1)))


def _attn_sample_specs(q4, k_new, v_new, cache_k, cache_v):
    n_seq, _, n_past = cache_k.shape
    t = k_new.shape[0] // n_seq
    mult = _key_multiplicity(t, n_past)
    cache_blk = pl.BlockSpec((1, D_KV, n_past), lambda b: (b, 0, 0))
    new_blk = pl.BlockSpec((t, D_KV), lambda b: (b, 0))
    q_blk = pl.BlockSpec((4, t, LANES), lambda b: (0, b, 0))
    return ([q4, k_new, v_new, cache_k, cache_v, mult],
            [q_blk, new_blk, new_blk, cache_blk, cache_blk, _resident(mult.shape)],
            [q_blk, cache_blk, cache_blk],
            [jax.ShapeDtypeStruct(q4.shape, F32),
             jax.ShapeDtypeStruct(cache_k.shape, F32), jax.ShapeDtypeStruct(cache_v.shape, F32)])


def _conv_silu(delayed, cw_ref, cb_ref, xc_s):
    for j in range(D_XBC // LANES):
        lanes = slice(j * LANES, (j + 1) * LANES)
        y = cb_ref[:, lanes]
        for tap in range(CONV_WIDTH):
            y = y + delayed(CONV_WIDTH - 1 - tap, lanes) * cw_ref[tap:tap + 1, lanes]
        xc_s[:, lanes] = _silu(y)


def _select_sum(x, sel):
    heads = x.shape[0]
    x = jnp.concatenate([x, jnp.zeros((LANES - heads, x.shape[1]), F32)], axis=0)
    total = None
    for _ in range(3):
        piece = x.astype(BF16)
        x = x - piece.astype(F32)
        part = jnp.dot(piece, sel, preferred_element_type=F32)
        total = part if total is None else total + part
    return total[0:heads]


def _to_columns(x_t):
    pad = jnp.zeros((LANES - x_t.shape[0], x_t.shape[1]), F32)
    return jnp.concatenate([x_t, pad], axis=0).T


def _ssd_scalars(dt_raw, dtb_ref, alog_ref, before, seg_end=None):
    dt_t = _softplus(dt_raw.T[0:N_SSM_HEADS] + dtb_ref[...])
    a_t = dt_t * (-jnp.exp(alog_ref[...]))
    a_cum_t = _select_sum(a_t, before)
    if seg_end is None:
        a_end_t = jnp.broadcast_to(a_cum_t[:, CHUNK - 1:CHUNK], a_cum_t.shape)
    else:
        a_end_t = _select_sum(a_cum_t, seg_end)
    to_end_t = jnp.exp(a_end_t - a_cum_t) * dt_t
    return dt_t, _to_columns(a_cum_t), a_cum_t, a_end_t, to_end_t


def _ssd_diag(xc_s, blk, cb, a_cum, a_cum_t, dt_t, seg_mask):
    e0 = lax.broadcasted_iota(jnp.int32, (CHUNK, LANES), 1) < SSM_HEAD_DIM
    x_pair = xc_s[:, blk * LANES:(blk + 1) * LANES].astype(BF16)
    out, grow = [], []
    for e in range(2):
        h = 2 * blk + e
        a_col = jnp.broadcast_to(a_cum[:, h:h + 1], (CHUNK, CHUNK))
        seg = a_col - jnp.broadcast_to(a_cum_t[h:h + 1, :], (CHUNK, CHUNK))
        w = cb * jnp.where(seg_mask, jnp.exp(seg), 0.0) * dt_t[h:h + 1, :]
        out.append(jnp.dot(w.astype(BF16), x_pair, preferred_element_type=F32))
        grow.append(jnp.exp(a_col))
    return jnp.where(e0, out[0], out[1]), jnp.where(e0, grow[0], grow[1])


def _ssd_finish(xc_s, z_ref, rows, y_parts, dsk_ref, nrm_ref, y_ref):
    blocks_per_group = D_SSM // SSM_GROUPS // LANES
    for g in range(SSM_GROUPS):
        gated = []
        for p in range(blocks_per_group):
            blk = g * blocks_per_group + p
            lanes = slice(blk * LANES, (blk + 1) * LANES)
            y = y_parts[blk] + dsk_ref[:, lanes] * xc_s[:, lanes]
            gated.append(y * _silu(z_ref[rows, lanes]))
        ss = sum(jnp.sum(y * y, axis=-1, keepdims=True) for y in gated)
        inv = lax.rsqrt(ss / (D_SSM // SSM_GROUPS) + RMS_EPS)
        for p in range(blocks_per_group):
            blk = g * blocks_per_group + p
            lanes = slice(blk * LANES, (blk + 1) * LANES)
            y_ref[rows, lanes] = (gated[p] * inv * nrm_ref[:, lanes]).astype(y_ref.dtype)


def _ssd_prompt_body(xbc_ref, z_ref, dt_ref, cw_ref, cb_ref, dtb_ref, alog_ref, dsk_ref, nrm_ref,
                     y_ref, st_ref, st_s, xc_bufs):
    seq = xbc_ref.shape[0]
    st_s[...] = jnp.zeros_like(st_s)
    row = lax.broadcasted_iota(jnp.int32, (CHUNK, CHUNK), 0)
    col = lax.broadcasted_iota(jnp.int32, (CHUNK, CHUNK), 1)
    causal = row >= col
    before = jnp.where(row <= col, 1.0, 0.0).astype(BF16)
    row8 = lax.broadcasted_iota(jnp.int32, (SUBLANES, LANES), 0)
    heads_per_group = N_SSM_HEADS // SSM_GROUPS
    group_rows = heads_per_group * SSM_HEAD_DIM

    def chunk(c, xc_s):
        rows = pl.ds(pl.multiple_of(c * CHUNK, CHUNK), CHUNK)
        prev_rows = pl.ds(pl.multiple_of(jnp.maximum(c * CHUNK - SUBLANES, 0), SUBLANES), SUBLANES)
        have_prev = c > 0

        def delayed(k, lanes):
            cur = xbc_ref[rows, lanes]
            if k == 0:
                return cur
            rolled = pltpu.roll(cur, k, 0)
            prev = jnp.where(have_prev, xbc_ref[prev_rows, lanes], 0.0)
            head = jnp.where(row8 < k, pltpu.roll(prev, k, 0), rolled[0:SUBLANES])
            return jnp.concatenate([head, rolled[SUBLANES:]], axis=0)

        _conv_silu(delayed, cw_ref, cb_ref, xc_s)
        dt_t, a_cum, a_cum_t, a_end_t, to_end_t = _ssd_scalars(dt_ref[rows, :], dtb_ref, alog_ref, before)
        x_t = xc_s[:, 0:D_SSM].T
        y_parts = []
        for g in range(SSM_GROUPS):
            bg = xc_s[:, D_SSM + g * D_STATE:D_SSM + (g + 1) * D_STATE].astype(BF16)
            cg = xc_s[:, D_SSM + (SSM_GROUPS + g) * D_STATE:D_SSM + (SSM_GROUPS + g + 1) * D_STATE].astype(BF16)
            cb = lax.dot_general(cg, bg, _NT, preferred_element_type=F32)
            grp = slice(g * group_rows, (g + 1) * group_rows)
            st_prev = st_s[grp, :]
            y_off = lax.dot_general(cg, st_prev.astype(BF16), _NT, preferred_element_type=F32)
            for p in range(group_rows // LANES):
                blk = g * (group_rows // LANES) + p
                y_diag, grow = _ssd_diag(xc_s, blk, cb, a_cum, a_cum_t, dt_t, causal)
                y_parts.append(y_diag + y_off[:, p * LANES:(p + 1) * LANES] * grow)
            xw = []
            for h in range(g * heads_per_group, (g + 1) * heads_per_group):
                hr = slice(h * SSM_HEAD_DIM, (h + 1) * SSM_HEAD_DIM)
                xw.append((x_t[hr, :] * to_end_t[h:h + 1, :]).astype(BF16))
            new = jnp.dot(jnp.concatenate(xw, axis=0), bg, preferred_element_type=F32)
            for i, h in enumerate(range(g * heads_per_group, (g + 1) * heads_per_group)):
                hr = slice(h * SSM_HEAD_DIM, (h + 1) * SSM_HEAD_DIM)
                lr = slice(i * SSM_HEAD_DIM, (i + 1) * SSM_HEAD_DIM)
                decay = jnp.exp(jnp.broadcast_to(a_end_t[h:h + 1, 0:1], (SSM_HEAD_DIM, D_STATE)))
                st_s[hr, :] = decay * st_prev[lr, :] + new[lr, :]
        _ssd_finish(xc_s, z_ref, rows, y_parts, dsk_ref, nrm_ref, y_ref)

    def step(i, carry):
        for j in range(SSD_CHUNKS_PER_STEP):
            chunk(SSD_CHUNKS_PER_STEP * i + j, xc_bufs.at[j])
        return carry

    lax.fori_loop(0, seq // CHUNK // SSD_CHUNKS_PER_STEP, step, 0)
    st_ref[...] = st_s[...]


def _ssd_prompt(xbc, z, dt_raw, n_seq, seq, consts):
    rows = xbc.shape[0]
    blk = lambda w: pl.BlockSpec((seq, w), lambda b: (b, 0))
    state_blk = pl.BlockSpec((None, D_SSM, D_STATE), lambda b: (b, 0, 0))
    return pl.pallas_call(
        _ssd_prompt_body,
        grid=(n_seq,),
        in_specs=[blk(D_XBC), blk(D_SSM), blk(LANES)] + [_resident(c.shape) for c in consts],
        out_specs=[blk(D_SSM), state_blk],
        out_shape=[jax.ShapeDtypeStruct((rows, D_SSM), BF16),
                   jax.ShapeDtypeStruct((n_seq, D_SSM, D_STATE), F32)],
        scratch_shapes=[pltpu.VMEM((D_SSM, D_STATE), F32),
                        pltpu.VMEM((SSD_CHUNKS_PER_STEP, CHUNK, D_XBC), F32)],
        compiler_params=_params(1),
        name="ssd_prompt",
    )(xbc, z, dt_raw, *consts)


def _ssd_sample_body(xbc_ref, pre_ref, z_ref, dt_ref, h0_ref, cw_ref, cb_ref, dtb_ref, alog_ref, dsk_ref,
                     nrm_ref, y_ref, st_ref, xc_s, yoff_s, aend_s):
    t = CHUNK // SAMPLE_SEQS
    row = lax.broadcasted_iota(jnp.int32, (CHUNK, CHUNK), 0)
    col = lax.broadcasted_iota(jnp.int32, (CHUNK, CHUNK), 1)
    same_seq = (row // t) == (col // t)
    seg_mask = same_seq & (row >= col)
    before = jnp.where(same_seq & (row <= col), 1.0, 0.0).astype(BF16)
    seg_end = jnp.where(row == (col // t) * t + (t - 1), 1.0, 0.0).astype(BF16)
    step = lax.broadcasted_iota(jnp.int32, (CHUNK, LANES), 0) % t
    rows = pl.ds(0, CHUNK)
    heads_per_group = N_SSM_HEADS // SSM_GROUPS
    group_rows = heads_per_group * SSM_HEAD_DIM

    def delayed(k, lanes):
        cur = xbc_ref[:, lanes]
        if k == 0:
            return cur
        return jnp.where(step < k, pltpu.roll(pre_ref[:, lanes], (k - t) % CHUNK, 0), pltpu.roll(cur, k, 0))

    _conv_silu(delayed, cw_ref, cb_ref, xc_s)
    dt_t, a_cum, a_cum_t, a_end_t, to_end_t = _ssd_scalars(dt_ref[...], dtb_ref, alog_ref, before, seg_end)
    aend_s[...] = _to_columns(a_end_t)
    x_t = xc_s[:, 0:D_SSM].T
    col_seq = lax.broadcasted_iota(jnp.int32, (SSM_HEAD_DIM, CHUNK), 1) // t
    y_parts = []
    for g in range(SSM_GROUPS):
        bg = xc_s[:, D_SSM + g * D_STATE:D_SSM + (g + 1) * D_STATE].astype(BF16)
        cg_lanes = slice(D_SSM + (SSM_GROUPS + g) * D_STATE, D_SSM + (SSM_GROUPS + g + 1) * D_STATE)
        cg = xc_s[:, cg_lanes].astype(BF16)
        cb = lax.dot_general(cg, bg, _NT, preferred_element_type=F32)
        grp = slice(g * group_rows, (g + 1) * group_rows)
        xw = []
        for h in range(g * heads_per_group, (g + 1) * heads_per_group):
            hr = slice(h * SSM_HEAD_DIM, (h + 1) * SSM_HEAD_DIM)
            xw.append(x_t[hr, :] * to_end_t[h:h + 1, :])

        def per_seq(b, carry, g=g, bg=bg, cg_lanes=cg_lanes, grp=grp, xw=xw):
            seq_rows = pl.ds(pl.multiple_of(b * t, t), t)
            h_prev = h0_ref[b, grp, :]
            cg_b = xc_s[seq_rows, cg_lanes].astype(BF16)
            yoff_s[seq_rows, g * group_rows:(g + 1) * group_rows] = lax.dot_general(
                cg_b, h_prev.astype(BF16), _NT, preferred_element_type=F32)
            mine = col_seq == b
            xw_b = jnp.concatenate([jnp.where(mine, w, 0.0).astype(BF16) for w in xw], axis=0)
            new = jnp.dot(xw_b, bg, preferred_element_type=F32)
            a_last = aend_s[pl.ds(b * t, 1), :]
            for i, h in enumerate(range(g * heads_per_group, (g + 1) * heads_per_group)):
                lr = slice(i * SSM_HEAD_DIM, (i + 1) * SSM_HEAD_DIM)
                decay = jnp.exp(jnp.broadcast_to(a_last[:, h:h + 1], (SSM_HEAD_DIM, D_STATE)))
                st_ref[b, h * SSM_HEAD_DIM:(h + 1) * SSM_HEAD_DIM, :] = decay * h_prev[lr, :] + new[lr, :]
            return carry

        lax.fori_loop(0, SAMPLE_SEQS, per_seq, 0, unroll=4)
        for p in range(group_rows // LANES):
            blk = g * (group_rows // LANES) + p
            lanes = slice(blk * LANES, (blk + 1) * LANES)
            y_diag, grow = _ssd_diag(xc_s, blk, cb, a_cum, a_cum_t, dt_t, seg_mask)
            y_parts.append(y_diag + yoff_s[:, lanes] * grow)
    _ssd_finish(xc_s, z_ref, rows, y_parts, dsk_ref, nrm_ref, y_ref)


def _ssd_sample(xbc, prefix_tiles, z, dt_raw, h0, consts):
    rows = xbc.shape[0]
    n_seq = h0.shape[0]
    blk = lambda w: pl.BlockSpec((CHUNK, w), lambda i: (i, 0))
    state_blk = pl.BlockSpec((SAMPLE_SEQS, D_SSM, D_STATE), lambda i: (i, 0, 0))
    return pl.pallas_call(
        _ssd_sample_body,
        grid=(rows // CHUNK,),
        in_specs=[blk(D_XBC), blk(D_XBC), blk(D_SSM), blk(LANES), state_blk]
                 + [_resident(c.shape) for c in consts],
        out_specs=[blk(D_SSM), state_blk],
        out_shape=[jax.ShapeDtypeStruct((rows, D_SSM), BF16),
                   jax.ShapeDtypeStruct((n_seq, D_SSM, D_STATE), F32)],
        scratch_shapes=[pltpu.VMEM((CHUNK, D_XBC), F32), pltpu.VMEM((CHUNK, D_SSM), F32),
                        pltpu.VMEM((CHUNK, LANES), F32)],
        compiler_params=_params(1),
        name="ssd_sample",
    )(xbc, prefix_tiles, z, dt_raw, h0, *consts)


def _out_mlp_body(x_ref, att_ref, ssm_ref, woa_ref, wos_ref, g_ref, wu_ref, wd_ref, gf_ref, o_ref, side_work=()):
    side_work = list(side_work)
    att = jnp.concatenate([att_ref[j] for j in range(att_ref.shape[0])], axis=1).astype(BF16)
    h = (x_ref[...] + jnp.dot(att, woa_ref[...], preferred_element_type=F32)
         + jnp.dot(ssm_ref[...], wos_ref[...], preferred_element_type=F32))
    hn = _rms(h, g_ref[...]).astype(BF16)
    mlp = None
    for c in range(D_FF // FF_TILE):
        if side_work:
            side_work.pop(0)()
        cols = slice(c * FF_TILE, (c + 1) * FF_TILE)
        u = jnp.maximum(jnp.dot(hn, wu_ref[:, cols], preferred_element_type=F32), 0.0)
        down = jnp.dot((u * u).astype(BF16), wd_ref[cols, :], preferred_element_type=F32)
        mlp = down if mlp is None else mlp + down
    for work in side_work:
        work()
    o_ref[...] = _rms(h + mlp, gf_ref[...])


def _out_mlp_attn_body(*refs):
    n_mlp, n_att = 9, 6
    o_ref = refs[n_mlp + n_att]
    pieces = _attn_sample_pieces(*refs[n_mlp:n_mlp + n_att], *refs[n_mlp + n_att + 1:])
    _out_mlp_body(*refs[:n_mlp], o_ref, side_work=pieces)


def _out_mlp(x2d, att4, ssm, w_out_att, w_out_ssm, norm_mlp, w_up, w_down, norm_final, sample=None):
    rows = x2d.shape[0]
    tm = min(ROW_TILE, rows) if sample is None else rows // sample[3].shape[0]
    row_blk = lambda w: pl.BlockSpec((tm, w), lambda i: (i, 0))
    operands = [x2d, att4, ssm, w_out_att, w_out_ssm, norm_mlp, w_up, w_down, norm_final]
    in_specs = [row_blk(D_MODEL), pl.BlockSpec((4, tm, LANES), lambda i: (0, i, 0)), row_blk(D_SSM),
                _resident((D_ATT, D_MODEL)), _resident((D_SSM, D_MODEL)), _resident((1, D_MODEL)),
                _resident((D_MODEL, D_FF)), _resident((D_FF, D_MODEL)), _resident((1, D_MODEL))]
    out_specs = [row_blk(D_MODEL)]
    out_shape = [jax.ShapeDtypeStruct((rows, D_MODEL), F32)]
    body = _out_mlp_body
    if sample is not None:
        att_operands, att_in, att_out, att_shape = _attn_sample_specs(*sample)
        assert rows == tm * sample[3].shape[0] and tm % SUBLANES == 0
        operands += att_operands
        in_specs += att_in
        out_specs += att_out
        out_shape += att_shape
        body = _out_mlp_attn_body
    out = pl.pallas_call(
        body,
        grid=(rows // tm,),
        in_specs=in_specs,
        out_specs=out_specs,
        out_shape=out_shape,
        compiler_params=_params(1),
        name="out_mlp",
    )(*operands)
    return out[0] if sample is None else out


def _q_permutation():
    perm = []
    for pp in range(N_KV_HEADS // 2):
        for g in range(Q_PER_KV):
            for e in range(2):
                base = (2 * pp + e) * Q_PER_KV * HEAD_DIM + g * HEAD_DIM
                perm += list(range(base, base + HEAD_DIM))
    return np.asarray(perm, np.int32)


def _to_lanes(buf):
    n, pos = buf.shape[:2]
    return jnp.transpose(buf, (0, 2, 3, 1)).reshape(n, D_KV, pos)


def _from_lanes(buf_t):
    n, _, pos = buf_t.shape
    return jnp.transpose(buf_t.reshape(n, N_KV_HEADS, HEAD_DIM, pos), (0, 3, 1, 2))[None]


def _head_rows(v):
    return jnp.broadcast_to(v.astype(F32)[:, None], (v.shape[0], LANES))


def kernel(x_prompt, x_sample, cache_k, cache_v, state_conv, state_ssm, w_in, w_out, conv_w, conv_b, dt_bias, a_log, d_skip, ssm_norm, norm_mix, norm_mlp, w_up, w_down, norm_final):
    depth = w_in.shape[0]
    assert depth == 1, "single-layer step"
    n_p, seq, _ = x_prompt.shape
    n_s, t_new, _ = x_sample.shape
    n_past = cache_k.shape[2]
    assert seq == MAX_WINDOW and n_past == MAX_WINDOW and t_new == SUBLANES and n_s % SAMPLE_SEQS == 0

    perm = _q_permutation()
    w = w_in[0]
    w_pad = jnp.pad(w.astype(BF16), ((0, 0), (0, D_IN_PAD - w.shape[1])))
    w_out_att = w_out[0][perm, :].astype(BF16)
    w_out_ssm = w_out[0][D_ATT:, :].astype(BF16)
    w_up_b, w_down_b = w_up[0].astype(BF16), w_down[0].astype(BF16)
    g_mix, g_mlp, g_fin = (v.reshape(1, D_MODEL) for v in (norm_mix[0], norm_mlp[0], norm_final))
    ssd_consts = (conv_w[0], conv_b[0].reshape(1, D_XBC), _head_rows(dt_bias[0]), _head_rows(a_log[0]),
                  jnp.repeat(d_skip[0], SSM_HEAD_DIM).reshape(1, D_SSM), ssm_norm[0].reshape(1, D_SSM))

    xp = x_prompt.reshape(n_p * seq, D_MODEL)
    q4, k, v, z, xbc, dt_raw, k_t, v_t = _in_proj(xp, g_mix, w_pad, _rope_table(np.arange(seq)), seq=seq)
    att4 = _attn_prompt(q4, k, v, n_p, seq)
    y_ssm, st_p = _ssd_prompt(xbc, z, dt_raw, n_p, seq, ssd_consts)
    k_prompt, v_prompt = _from_lanes(k_t), _from_lanes(v_t)
    conv_prompt = xbc.reshape(1, n_p, seq, D_XBC)[:, :, seq - (CONV_WIDTH - 1):]

    xs = x_sample.reshape(n_s * t_new, D_MODEL)
    rows_tile = min(ROW_TILE, n_s * t_new)
    pos = PAST_LEN + (np.arange(rows_tile) % t_new)
    q4s, ks, vs, zs, xbcs, dts = _in_proj(xs, g_mix, w_pad, _rope_table(pos))
    y_prompt, att4s, k_sample, v_sample = _out_mlp(
        xp, att4, y_ssm, w_out_att, w_out_ssm, g_mlp, w_up_b, w_down_b, g_fin,
        sample=(q4s, ks, vs, _to_lanes(cache_k[0]), _to_lanes(cache_v[0])))
    prefix_tiles = jnp.pad(state_conv[0], ((0, 0), (t_new - (CONV_WIDTH - 1), 0), (0, 0))).reshape(n_s * t_new, D_XBC)
    y_ssm_s, st_s = _ssd_sample(xbcs, prefix_tiles, zs, dts, state_ssm[0].reshape(n_s, D_SSM, D_STATE), ssd_consts)
    y_sample = _out_mlp(xs, att4s, y_ssm_s, w_out_att, w_out_ssm, g_mlp, w_up_b, w_down_b, g_fin)
    conv_sample = xbcs.reshape(1, n_s, t_new, D_XBC)[:, :, t_new - (CONV_WIDTH - 1):]

    return (y_prompt.reshape(n_p, seq, D_MODEL), y_sample.reshape(n_s, t_new, D_MODEL),
            k_prompt, v_prompt, conv_prompt,
            st_p.reshape(1, n_p, N_SSM_HEADS, SSM_HEAD_DIM, D_STATE),
            _from_lanes(k_sample), _from_lanes(v_sample), conv_sample,
            st_s.reshape(1, n_s, N_SSM_HEADS, SSM_HEAD_DIM, D_STATE))
```

```python
import functools

import numpy as np
import jax
import jax.numpy as jnp
from jax import lax
from jax.experimental import pallas as pl
from jax.experimental.pallas import tpu as pltpu

F32 = jnp.float32
BF16 = jnp.bfloat16

LANES = 128
SUBLANES = 8

D_MODEL = 1024
HEAD_DIM = 64
N_KV_HEADS = 4
Q_PER_KV = 2
D_ATT = 512
D_KV = 256
ROT_HALF = 8
ROPE_THETA = 500000.0
DILATIONS = (1, 4, 16)
N_KEYS = 128
MAX_WINDOW = 2048
ATT_BLK = 128
ATT_UNITS = 16
REGROUP_STRIDE = 4
RESIDUES = 16
PAST_LEN = 8192
ATT_SCALE = HEAD_DIM ** -0.5
NEG_BIG = -1e30
N_SSM_HEADS = 8
SSM_HEAD_DIM = 64
D_SSM = 512
SSM_GROUPS = 2
D_STATE = 128
CONV_WIDTH = 4
CHUNK = 128
D_XBC = 1024
D_FF = 4096
RMS_EPS = 1e-5
D_IN_MAIN = D_ATT + 2 * D_KV + D_SSM + D_XBC
D_IN_PAD = D_IN_MAIN + LANES

VMEM_LIMIT = 56 * 1024 * 1024
ROW_TILE = 512
FF_TILE = 1024
SSD_CHUNKS_PER_STEP = 4
SAMPLE_SEQS = CHUNK // 8

_NT = (((1,), (1,)), ((), ()))


def _params(n_axes):
    return pltpu.CompilerParams(dimension_semantics=("arbitrary",) * n_axes,
                                vmem_limit_bytes=VMEM_LIMIT)


def _resident(shape):
    return pl.BlockSpec(shape, lambda *_: (0,) * len(shape), pipeline_mode=pl.Buffered(1))


def _rms(x, g):
    return x * lax.rsqrt(jnp.mean(x * x, axis=-1, keepdims=True) + RMS_EPS) * g


def _silu(x):
    half = 0.5 * x
    return half + half * jnp.tanh(half)


def _softplus(x):
    return jnp.maximum(x, 0.0) + jnp.log1p(jnp.exp(-jnp.abs(x)))


def _in_proj_body(x_ref, g_ref, w_ref, rope_ref, q_ref, k_ref, v_ref, z_ref, xbc_ref, dt_ref, *kv_t_refs):
    xb = _rms(x_ref[...], g_ref[...]).astype(BF16)
    tm = xb.shape[0]
    first_half = (lax.broadcasted_iota(jnp.int32, (tm, LANES), 1) % HEAD_DIM) < ROT_HALF

    def proj(lo, hi):
        return jnp.dot(xb, w_ref[:, lo:hi], preferred_element_type=F32)

    def rope(u, cos, sin):
        partner = jnp.where(first_half, pltpu.roll(u, LANES - ROT_HALF, 1), pltpu.roll(u, ROT_HALF, 1))
        return u * cos + partner * sin

    cq, sq = rope_ref[:, 0:128], rope_ref[:, 128:256]
    ck, sk = rope_ref[:, 256:384], rope_ref[:, 384:512]
    q = proj(0, D_ATT)
    q = [rope(q[:, j * LANES:(j + 1) * LANES], cq, sq) for j in range(D_ATT // LANES)]
    low_half = lax.broadcasted_iota(jnp.int32, (tm, LANES), 1) < HEAD_DIM
    for pp in range(N_KV_HEADS // 2):
        head0, head1 = q[2 * pp], q[2 * pp + 1]
        q_ref[Q_PER_KV * pp] = jnp.where(low_half, head0, pltpu.roll(head1, HEAD_DIM, 1))
        q_ref[Q_PER_KV * pp + 1] = jnp.where(low_half, pltpu.roll(head0, HEAD_DIM, 1), head1)
    k = proj(D_ATT, D_ATT + D_KV)
    v = proj(D_ATT + D_KV, D_ATT + 2 * D_KV)
    v_ref[...] = v
    for j in range(D_KV // LANES):
        lanes = slice(j * LANES, (j + 1) * LANES)
        kj = rope(k[:, lanes], ck, sk)
        k_ref[:, lanes] = kj
        if kv_t_refs:
            kv_t_refs[0][lanes, :] = kj.T
            kv_t_refs[1][lanes, :] = v[:, lanes].T
    z_ref[...] = proj(D_ATT + 2 * D_KV, D_ATT + 2 * D_KV + D_SSM)
    xbc_ref[...] = proj(D_ATT + 2 * D_KV + D_SSM, D_IN_MAIN)
    dt_ref[...] = proj(D_IN_MAIN, D_IN_PAD)


def _in_proj(x2d, norm_mix, w_pad, rope_tab, seq=None):
    rows = x2d.shape[0]
    tm = min(ROW_TILE, rows)
    n_tab = rope_tab.shape[0] // tm
    row_blk = lambda w: pl.BlockSpec((tm, w), lambda i: (i, 0))
    out_specs = [pl.BlockSpec((4, tm, LANES), lambda i: (0, i, 0)),
                 row_blk(D_KV), row_blk(D_KV), row_blk(D_SSM), row_blk(D_XBC), row_blk(LANES)]
    out_shape = [jax.ShapeDtypeStruct((4, rows, LANES), F32),
                 jax.ShapeDtypeStruct((rows, D_KV), F32), jax.ShapeDtypeStruct((rows, D_KV), F32),
                 jax.ShapeDtypeStruct((rows, D_SSM), F32), jax.ShapeDtypeStruct((rows, D_XBC), F32),
                 jax.ShapeDtypeStruct((rows, LANES), F32)]
    if seq is not None:
        per_seq = seq // tm
        t_blk = pl.BlockSpec((None, D_KV, tm), lambda i: (i // per_seq, 0, i % per_seq))
        out_specs += [t_blk, t_blk]
        out_shape += [jax.ShapeDtypeStruct((rows // seq, D_KV, seq), F32)] * 2
    return pl.pallas_call(
        _in_proj_body,
        grid=(rows // tm,),
        in_specs=[row_blk(D_MODEL), _resident((1, D_MODEL)), _resident((D_MODEL, D_IN_PAD)),
                  pl.BlockSpec((tm, 4 * LANES), lambda i: (i % n_tab, 0))],
        out_specs=out_specs,
        out_shape=out_shape,
        compiler_params=_params(1),
        name="in_proj",
    )(x2d, norm_mix, w_pad, rope_tab)


def _rope_table(pos):
    pos = np.asarray(pos, np.float64)
    inv = ROPE_THETA ** (-np.arange(0, 2 * ROT_HALF, 2, dtype=np.float64) / (2 * ROT_HALF))
    ang = pos[:, None] * inv[None, :]
    cos, sin = np.cos(ang), np.sin(ang)
    ones = np.ones((pos.shape[0], HEAD_DIM - 2 * ROT_HALF))
    cos_h = np.concatenate([cos, cos, ones], axis=1)
    sin_h = np.concatenate([-sin, sin, 0.0 * ones], axis=1)
    cos_l, sin_l = np.tile(cos_h, (1, 2)), np.tile(sin_h, (1, 2))
    table = np.concatenate([cos_l * ATT_SCALE, sin_l * ATT_SCALE, cos_l, sin_l], axis=1)
    return jnp.asarray(table.astype(np.float32))


def _attn_prompt_body(q_ref, k_ref, v_ref, bias1_ref, bias4_ref, caus_ref, o_ref,
                      q3, k3, vhead3, m3, l3, acc3, nat_s):
    seq = k_ref.shape[0]
    per_res = seq // RESIDUES
    e0 = lax.broadcasted_iota(jnp.int32, (ATT_BLK, LANES), 1) < HEAD_DIM
    zero = jnp.zeros((ATT_BLK, LANES), F32)

    def regroup(src, tmp, dst):
        quarter = seq // REGROUP_STRIDE
        for r in range(REGROUP_STRIDE):
            tmp[r * quarter:(r + 1) * quarter, :] = src[pl.ds(r, quarter, stride=REGROUP_STRIDE), :]
        for r in range(REGROUP_STRIDE):
            for a in range(RESIDUES // REGROUP_STRIDE):
                dst[r + REGROUP_STRIDE * a] = tmp[pl.ds(r * quarter + a, per_res, stride=REGROUP_STRIDE), :]

    for g in range(Q_PER_KV):
        regroup(q_ref.at[g], nat_s.at[g], q3.at[g])
    regroup(k_ref, nat_s.at[Q_PER_KV], k3)
    regroup(v_ref, nat_s.at[Q_PER_KV + 1], vhead3.at[0])
    for r in range(RESIDUES):
        v = vhead3[0, r]
        vhead3[0, r] = jnp.where(e0, v, 0.0)
        vhead3[1, r] = jnp.where(e0, 0.0, v)

    def scores(q_idx, k_idx, bias):
        parts = []
        for g in range(Q_PER_KV):
            qg = q3.at[g][q_idx].reshape(ATT_BLK, LANES)
            parts += [jnp.where(e0, qg, zero), jnp.where(e0, zero, qg)]
        lhs = jnp.concatenate(parts, axis=0).astype(BF16)
        kb = k3[k_idx]
        nk = kb.shape[0] * kb.shape[1]
        kb = kb.reshape(nk, LANES).astype(BF16)
        s = lax.dot_general(lhs, kb, _NT, preferred_element_type=F32)
        s = (s.reshape(4, ATT_BLK, nk) + bias[None]).reshape(4 * ATT_BLK, nk)
        m = jnp.max(s, axis=1, keepdims=True)
        p = jnp.exp(s - m)
        mb = jnp.broadcast_to(m, (4 * ATT_BLK, LANES))
        key_lane = lax.broadcasted_iota(jnp.int32, (nk, LANES), 1) < HEAD_DIM
        ones_of = (jnp.where(key_lane, 1.0, 0.0), jnp.where(key_lane, 0.0, 1.0))
        rhs = jnp.concatenate(
            [jnp.concatenate([vhead3.at[e][k_idx].reshape(nk, LANES), ones_of[e]], axis=1) for e in range(2)],
            axis=0).astype(BF16)
        p2 = jnp.concatenate(
            [jnp.concatenate([p[(2 * g + e) * ATT_BLK:(2 * g + e + 1) * ATT_BLK] for e in range(2)], axis=1)
             for g in range(Q_PER_KV)], axis=0).astype(BF16)
        pvl = jnp.dot(p2, rhs, preferred_element_type=F32)
        out = []
        for g in range(Q_PER_KV):
            r0, r1, r2 = 2 * g * ATT_BLK, (2 * g + 1) * ATT_BLK, (2 * g + 2) * ATT_BLK
            blk = pvl[g * ATT_BLK:(g + 1) * ATT_BLK]
            out.append((jnp.where(e0, mb[r0:r1], mb[r1:r2]), blk[:, LANES:], blk[:, :LANES]))
        return out

    def process(units, first_group):
        state = [(m3.at[g], l3.at[g], acc3.at[g]) for g in range(Q_PER_KV)]
        new = [scores(*unit) for unit in units]
        for i, (q_idx, _, _) in enumerate(units):
            for g in range(Q_PER_KV):
                mg, lg, acc = new[i][g]
                if not first_group:
                    m_old, l_old, acc_old = (ref[q_idx].reshape(ATT_BLK, LANES) for ref in state[g])
                    m_new = jnp.maximum(m_old, mg)
                    a_old = jnp.exp(m_old - m_new)
                    a_new = jnp.exp(mg - m_new)
                    mg, lg, acc = m_new, a_old * l_old + a_new * lg, a_old * acc_old + a_new * acc
                for ref, val in zip(state[g], (mg, lg, acc)):
                    ref[q_idx] = val.reshape(ref[q_idx].shape)

    for gi, (d, bias_ref) in enumerate(zip(DILATIONS, (bias1_ref, bias4_ref, caus_ref))):
        blocks = seq // d // ATT_BLK
        lanes16 = RESIDUES // d
        depth = ATT_BLK // lanes16

        def unit(u, d=d, blocks=blocks, lanes16=lanes16, depth=depth, bias_ref=bias_ref):
            r, n = u // blocks, u % blocks
            lead = pl.ds(r, lanes16, stride=d)
            q_idx = (lead, pl.ds(pl.multiple_of(n * depth, depth), depth))
            if blocks == 1:
                return q_idx, q_idx, bias_ref[...]
            k_idx = (lead, pl.ds(pl.multiple_of(jnp.maximum(n - 1, 0) * depth, depth), 2 * depth))
            return q_idx, k_idx, bias_ref[jnp.minimum(n, 1)]

        def step(i, carry, unit=unit, first_group=(gi == 0)):
            process([unit(ATT_UNITS * i + j) for j in range(ATT_UNITS)], first_group)
            return carry

        lax.fori_loop(0, d * blocks // ATT_UNITS, step, 0)

    quarter = seq // REGROUP_STRIDE
    for g in range(Q_PER_KV):
        tmp, out = nat_s.at[Q_PER_KV + g], nat_s.at[g]
        for r in range(REGROUP_STRIDE):
            for a in range(RESIDUES // REGROUP_STRIDE):
                res = r + REGROUP_STRIDE * a
                tmp[pl.ds(r * quarter + a, per_res, stride=REGROUP_STRIDE), :] = acc3[g, res] / l3[g, res]
        for r in range(REGROUP_STRIDE):
            out[pl.ds(r, quarter, stride=REGROUP_STRIDE), :] = tmp[r * quarter:(r + 1) * quarter, :]
        o_ref[g] = out[...].astype(o_ref.dtype)


def _window_bias(d):
    lanes16 = RESIDUES // d

    def order(n):
        a, ll = np.divmod(np.arange(n), n // lanes16)
        return lanes16 * ll + a

    i = order(ATT_BLK)[:, None]
    if d == RESIDUES:
        return jnp.asarray(np.where(order(ATT_BLK)[None, :] <= i, 0.0, NEG_BIG).astype(np.float32))
    j = order(2 * ATT_BLK)[None, :]
    dist = i + ATT_BLK - j
    inner = np.where((dist >= 0) & (dist <= N_KEYS), 0.0, NEG_BIG)
    first = np.where(j <= i, 0.0, NEG_BIG)
    return jnp.asarray(np.stack([first, inner]).astype(np.float32))


def _attn_prompt(q4, k, v, n_seq, seq):
    biases = [_window_bias(d) for d in DILATIONS]
    slab = (RESIDUES, seq // RESIDUES, LANES)
    return pl.pallas_call(
        _attn_prompt_body,
        grid=(n_seq, 2),
        in_specs=[pl.BlockSpec((2, seq, LANES), lambda b, pp: (pp, b, 0)),
                  pl.BlockSpec((seq, LANES), lambda b, pp: (b, pp)),
                  pl.BlockSpec((seq, LANES), lambda b, pp: (b, pp))] + [_resident(x.shape) for x in biases],
        out_specs=pl.BlockSpec((2, seq, LANES), lambda b, pp: (pp, b, 0)),
        out_shape=jax.ShapeDtypeStruct(q4.shape, BF16),
        scratch_shapes=[pltpu.VMEM((2,) + slab, F32), pltpu.VMEM(slab, F32), pltpu.VMEM((2,) + slab, F32)]
                       + [pltpu.VMEM((2,) + slab, F32)] * 3 + [pltpu.VMEM((Q_PER_KV + 2, seq, LANES), F32)],
        compiler_params=_params(2),
        name="attn_prompt",
    )(q4, k, v, *biases)


def _attn_sample_pieces(q_ref, kn_ref, vn_ref, ck_ref, cv_ref, mult_ref, att_ref, ko_ref, vo_ref):
    t = kn_ref.shape[0]
    n_past = ck_ref.shape[2]
    tail = slice(n_past - LANES, n_past)

    def shift(c_ref, n_ref, o_ref, rows):
        is_new = lax.broadcasted_iota(jnp.int32, (LANES, LANES), 1) >= LANES - t
        moved = pltpu.roll(c_ref[0, rows, :], n_past - t, 1)
        o_ref[0, rows, :] = moved
        new_rows = jnp.concatenate([n_ref[:, rows], jnp.zeros((LANES - t, LANES), F32)], axis=0)
        o_ref[0, rows, tail] = jnp.where(is_new, pltpu.roll(new_rows.T, LANES - t, 1), moved[:, tail])

    def attend():
        mult = mult_ref[...]
        e0 = lax.broadcasted_iota(jnp.int32, (t, LANES), 1) < HEAD_DIM
        zero = jnp.zeros((t, LANES), F32)
        pairs = N_KV_HEADS // 2
        parts = []
        for pp in range(pairs):
            for g in range(Q_PER_KV):
                qg = q_ref[pp * Q_PER_KV + g]
                for masked in (jnp.where(e0, qg, zero), jnp.where(e0, zero, qg)):
                    parts.append(jnp.concatenate([masked if j == pp else zero for j in range(pairs)], axis=1))
        lhs = jnp.concatenate(parts, axis=0).astype(BF16)
        pad = jnp.zeros((LANES - t, D_KV), F32)
        kn = jnp.concatenate([kn_ref[...], pad], axis=0).astype(BF16)
        vn = jnp.concatenate([vn_ref[...], pad], axis=0).astype(BF16)
        s = jnp.concatenate([jnp.dot(lhs, ck_ref[0].astype(BF16), preferred_element_type=F32),
                             lax.dot_general(lhs, kn, _NT, preferred_element_type=F32)], axis=1)
        s = jnp.where(mult > 0.0, s, NEG_BIG)
        m = jnp.max(s, axis=1, keepdims=True)
        p = jnp.exp(s - m) * mult
        l = jnp.sum(p, axis=1, keepdims=True)
        pb = p.astype(BF16)
        pv = (lax.dot_general(pb[:, :n_past], cv_ref[0].astype(BF16), _NT, preferred_element_type=F32)
              + jnp.dot(pb[:, n_past:], vn, preferred_element_type=F32))
        o = pv / l
        for pp in range(pairs):
            lanes = slice(pp * LANES, (pp + 1) * LANES)
            for g in range(Q_PER_KV):
                r0 = (pp * Q_PER_KV + g) * 2 * t
                att_ref[pp * Q_PER_KV + g] = jnp.where(e0, o[r0:r0 + t, lanes], o[r0 + t:r0 + 2 * t, lanes])

    pieces = [attend]
    for pp in range(N_KV_HEADS // 2):
        rows = slice(pp * LANES, (pp + 1) * LANES)
        pieces += [functools.partial(shift, ck_ref, kn_ref, ko_ref, rows),
                   functools.partial(shift, cv_ref, vn_ref, vo_ref, rows)]
    return pieces


def _key_multiplicity(t, n_past):
    idx = np.concatenate([np.arange(n_past + t), np.full(LANES - t, 10 ** 9)])[None, :]
    dist = n_past + np.arange(t)[:, None] - idx
    mult = np.zeros(dist.shape, np.float32)
    for d in DILATIONS:
        mult += (dist >= 0) & (dist % d == 0) & (dist <= N_KEYS * d)
    return jnp.asarray(np.tile(mult, (2 * N_KV_HEADS, 1)))


def _attn_sample_specs(q4, k_new, v_new, cache_k, cache_v):
    n_seq, _, n_past = cache_k.shape
    t = k_new.shape[0] // n_seq
    mult = _key_multiplicity(t, n_past)
    cache_blk = pl.BlockSpec((1, D_KV, n_past), lambda b: (b, 0, 0))
    new_blk = pl.BlockSpec((t, D_KV), lambda b: (b, 0))
    q_blk = pl.BlockSpec((4, t, LANES), lambda b: (0, b, 0))
    return ([q4, k_new, v_new, cache_k, cache_v, mult],
            [q_blk, new_blk, new_blk, cache_blk, cache_blk, _resident(mult.shape)],
            [q_blk, cache_blk, cache_blk],
            [jax.ShapeDtypeStruct(q4.shape, F32),
             jax.ShapeDtypeStruct(cache_k.shape, F32), jax.ShapeDtypeStruct(cache_v.shape, F32)])


def _conv_silu(delayed, cw_ref, cb_ref, xc_s):
    for j in range(D_XBC // LANES):
        lanes = slice(j * LANES, (j + 1) * LANES)
        y = cb_ref[:, lanes]
        for tap in range(CONV_WIDTH):
            y = y + delayed(CONV_WIDTH - 1 - tap, lanes) * cw_ref[tap:tap + 1, lanes]
        xc_s[:, lanes] = _silu(y)


def _select_sum(x, sel):
    heads = x.shape[0]
    x = jnp.concatenate([x, jnp.zeros((LANES - heads, x.shape[1]), F32)], axis=0)
    total = None
    for _ in range(3):
        piece = x.astype(BF16)
        x = x - piece.astype(F32)
        part = jnp.dot(piece, sel, preferred_element_type=F32)
        total = part if total is None else total + part
    return total[0:heads]


def _to_columns(x_t):
    pad = jnp.zeros((LANES - x_t.shape[0], x_t.shape[1]), F32)
    return jnp.concatenate([x_t, pad], axis=0).T


def _ssd_scalars(dt_raw, dtb_ref, alog_ref, before, seg_end=None):
    dt_t = _softplus(dt_raw.T[0:N_SSM_HEADS] + dtb_ref[...])
    a_t = dt_t * (-jnp.exp(alog_ref[...]))
    a_cum_t = _select_sum(a_t, before)
    if seg_end is None:
        a_end_t = jnp.broadcast_to(a_cum_t[:, CHUNK - 1:CHUNK], a_cum_t.shape)
    else:
        a_end_t = _select_sum(a_cum_t, seg_end)
    to_end_t = jnp.exp(a_end_t - a_cum_t) * dt_t
    return dt_t, _to_columns(a_cum_t), a_cum_t, a_end_t, to_end_t


def _ssd_diag(xc_s, blk, cb, a_cum, a_cum_t, dt_t, seg_mask):
    e0 = lax.broadcasted_iota(jnp.int32, (CHUNK, LANES), 1) < SSM_HEAD_DIM
    x_pair = xc_s[:, blk * LANES:(blk + 1) * LANES].astype(BF16)
    out, grow = [], []
    for e in range(2):
        h = 2 * blk + e
        a_col = jnp.broadcast_to(a_cum[:, h:h + 1], (CHUNK, CHUNK))
        seg = a_col - jnp.broadcast_to(a_cum_t[h:h + 1, :], (CHUNK, CHUNK))
        w = cb * jnp.where(seg_mask, jnp.exp(seg), 0.0) * dt_t[h:h + 1, :]
        out.append(jnp.dot(w.astype(BF16), x_pair, preferred_element_type=F32))
        grow.append(jnp.exp(a_col))
    return jnp.where(e0, out[0], out[1]), jnp.where(e0, grow[0], grow[1])


def _ssd_finish(xc_s, z_ref, rows, y_parts, dsk_ref, nrm_ref, y_ref):
    blocks_per_group = D_SSM // SSM_GROUPS // LANES
    for g in range(SSM_GROUPS):
        gated = []
        for p in range(blocks_per_group):
            blk = g * blocks_per_group + p
            lanes = slice(blk * LANES, (blk + 1) * LANES)
            y = y_parts[blk] + dsk_ref[:, lanes] * xc_s[:, lanes]
            gated.append(y * _silu(z_ref[rows, lanes]))
        ss = sum(jnp.sum(y * y, axis=-1, keepdims=True) for y in gated)
        inv = lax.rsqrt(ss / (D_SSM // SSM_GROUPS) + RMS_EPS)
        for p in range(blocks_per_group):
            blk = g * blocks_per_group + p
            lanes = slice(blk * LANES, (blk + 1) * LANES)
            y_ref[rows, lanes] = (gated[p] * inv * nrm_ref[:, lanes]).astype(y_ref.dtype)


def _ssd_prompt_body(xbc_ref, z_ref, dt_ref, cw_ref, cb_ref, dtb_ref, alog_ref, dsk_ref, nrm_ref,
                     y_ref, st_ref, st_s, xc_bufs):
    seq = xbc_ref.shape[0]
    st_s[...] = jnp.zeros_like(st_s)
    row = lax.broadcasted_iota(jnp.int32, (CHUNK, CHUNK), 0)
    col = lax.broadcasted_iota(jnp.int32, (CHUNK, CHUNK), 1)
    causal = row >= col
    before = jnp.where(row <= col, 1.0, 0.0).astype(BF16)
    row8 = lax.broadcasted_iota(jnp.int32, (SUBLANES, LANES), 0)
    heads_per_group = N_SSM_HEADS // SSM_GROUPS
    group_rows = heads_per_group * SSM_HEAD_DIM

    def chunk(c, xc_s):
        rows = pl.ds(pl.multiple_of(c * CHUNK, CHUNK), CHUNK)
        prev_rows = pl.ds(pl.multiple_of(jnp.maximum(c * CHUNK - SUBLANES, 0), SUBLANES), SUBLANES)
        have_prev = c > 0

        def delayed(k, lanes):
            cur = xbc_ref[rows, lanes]
            if k == 0:
                return cur
            rolled = pltpu.roll(cur, k, 0)
            prev = jnp.where(have_prev, xbc_ref[prev_rows, lanes], 0.0)
            head = jnp.where(row8 < k, pltpu.roll(prev, k, 0), rolled[0:SUBLANES])
            return jnp.concatenate([head, rolled[SUBLANES:]], axis=0)

        _conv_silu(delayed, cw_ref, cb_ref, xc_s)
        dt_t, a_cum, a_cum_t, a_end_t, to_end_t = _ssd_scalars(dt_ref[rows, :], dtb_ref, alog_ref, before)
        x_t = xc_s[:, 0:D_SSM].T
        y_parts = []
        for g in range(SSM_GROUPS):
            bg = xc_s[:, D_SSM + g * D_STATE:D_SSM + (g + 1) * D_STATE].astype(BF16)
            cg = xc_s[:, D_SSM + (SSM_GROUPS + g) * D_STATE:D_SSM + (SSM_GROUPS + g + 1) * D_STATE].astype(BF16)
            cb = lax.dot_general(cg, bg, _NT, preferred_element_type=F32)
            grp = slice(g * group_rows, (g + 1) * group_rows)
            st_prev = st_s[grp, :]
            y_off = lax.dot_general(cg, st_prev.astype(BF16), _NT, preferred_element_type=F32)
            for p in range(group_rows // LANES):
                blk = g * (group_rows // LANES) + p
                y_diag, grow = _ssd_diag(xc_s, blk, cb, a_cum, a_cum_t, dt_t, causal)
                y_parts.append(y_diag + y_off[:, p * LANES:(p + 1) * LANES] * grow)
            xw = []
            for h in range(g * heads_per_group, (g + 1) * heads_per_group):
                hr = slice(h * SSM_HEAD_DIM, (h + 1) * SSM_HEAD_DIM)
                xw.append((x_t[hr, :] * to_end_t[h:h + 1, :]).astype(BF16))
            new = jnp.dot(jnp.concatenate(xw, axis=0), bg, preferred_element_type=F32)
            for i, h in enumerate(range(g * heads_per_group, (g + 1) * heads_per_group)):
                hr = slice(h * SSM_HEAD_DIM, (h + 1) * SSM_HEAD_DIM)
                lr = slice(i * SSM_HEAD_DIM, (i + 1) * SSM_HEAD_DIM)
                decay = jnp.exp(jnp.broadcast_to(a_end_t[h:h + 1, 0:1], (SSM_HEAD_DIM, D_STATE)))
                st_s[hr, :] = decay * st_prev[lr, :] + new[lr, :]
        _ssd_finish(xc_s, z_ref, rows, y_parts, dsk_ref, nrm_ref, y_ref)

    def step(i, carry):
        for j in range(SSD_CHUNKS_PER_STEP):
            chunk(SSD_CHUNKS_PER_STEP * i + j, xc_bufs.at[j])
        return carry

    lax.fori_loop(0, seq // CHUNK // SSD_CHUNKS_PER_STEP, step, 0)
    st_ref[...] = st_s[...]


def _ssd_prompt(xbc, z, dt_raw, n_seq, seq, consts):
    rows = xbc.shape[0]
    blk = lambda w: pl.BlockSpec((seq, w), lambda b: (b, 0))
    state_blk = pl.BlockSpec((None, D_SSM, D_STATE), lambda b: (b, 0, 0))
    return pl.pallas_call(
        _ssd_prompt_body,
        grid=(n_seq,),
        in_specs=[blk(D_XBC), blk(D_SSM), blk(LANES)] + [_resident(c.shape) for c in consts],
        out_specs=[blk(D_SSM), state_blk],
        out_shape=[jax.ShapeDtypeStruct((rows, D_SSM), BF16),
                   jax.ShapeDtypeStruct((n_seq, D_SSM, D_STATE), F32)],
        scratch_shapes=[pltpu.VMEM((D_SSM, D_STATE), F32),
                        pltpu.VMEM((SSD_CHUNKS_PER_STEP, CHUNK, D_XBC), F32)],
        compiler_params=_params(1),
        name="ssd_prompt",
    )(xbc, z, dt_raw, *consts)


def _ssd_sample_body(xbc_ref, pre_ref, z_ref, dt_ref, h0_ref, cw_ref, cb_ref, dtb_ref, alog_ref, dsk_ref,
                     nrm_ref, y_ref, st_ref, xc_s, yoff_s, aend_s):
    t = CHUNK // SAMPLE_SEQS
    row = lax.broadcasted_iota(jnp.int32, (CHUNK, CHUNK), 0)
    col = lax.broadcasted_iota(jnp.int32, (CHUNK, CHUNK), 1)
    same_seq = (row // t) == (col // t)
    seg_mask = same_seq & (row >= col)
    before = jnp.where(same_seq & (row <= col), 1.0, 0.0).astype(BF16)
    seg_end = jnp.where(row == (col // t) * t + (t - 1), 1.0, 0.0).astype(BF16)
    step = lax.broadcasted_iota(jnp.int32, (CHUNK, LANES), 0) % t
    rows = pl.ds(0, CHUNK)
    heads_per_group = N_SSM_HEADS // SSM_GROUPS
    group_rows = heads_per_group * SSM_HEAD_DIM

    def delayed(k, lanes):
        cur = xbc_ref[:, lanes]
        if k == 0:
            return cur
        return jnp.where(step < k, pltpu.roll(pre_ref[:, lanes], (k - t) % CHUNK, 0), pltpu.roll(cur, k, 0))

    _conv_silu(delayed, cw_ref, cb_ref, xc_s)
    dt_t, a_cum, a_cum_t, a_end_t, to_end_t = _ssd_scalars(dt_ref[...], dtb_ref, alog_ref, before, seg_end)
    aend_s[...] = _to_columns(a_end_t)
    x_t = xc_s[:, 0:D_SSM].T
    col_seq = lax.broadcasted_iota(jnp.int32, (SSM_HEAD_DIM, CHUNK), 1) // t
    y_parts = []
    for g in range(SSM_GROUPS):
        bg = xc_s[:, D_SSM + g * D_STATE:D_SSM + (g + 1) * D_STATE].astype(BF16)
        cg_lanes = slice(D_SSM + (SSM_GROUPS + g) * D_STATE, D_SSM + (SSM_GROUPS + g + 1) * D_STATE)
        cg = xc_s[:, cg_lanes].astype(BF16)
        cb = lax.dot_general(cg, bg, _NT, preferred_element_type=F32)
        grp = slice(g * group_rows, (g + 1) * group_rows)
        xw = []
        for h in range(g * heads_per_group, (g + 1) * heads_per_group):
            hr = slice(h * SSM_HEAD_DIM, (h + 1) * SSM_HEAD_DIM)
            xw.append(x_t[hr, :] * to_end_t[h:h + 1, :])

        def per_seq(b, carry, g=g, bg=bg, cg_lanes=cg_lanes, grp=grp, xw=xw):
            seq_rows = pl.ds(pl.multiple_of(b * t, t), t)
            h_prev = h0_ref[b, grp, :]
            cg_b = xc_s[seq_rows, cg_lanes].astype(BF16)
            yoff_s[seq_rows, g * group_rows:(g + 1) * group_rows] = lax.dot_general(
                cg_b, h_prev.astype(BF16), _NT, preferred_element_type=F32)
            mine = col_seq == b
            xw_b = jnp.concatenate([jnp.where(mine, w, 0.0).astype(BF16) for w in xw], axis=0)
            new = jnp.dot(xw_b, bg, preferred_element_type=F32)
            a_last = aend_s[pl.ds(b * t, 1), :]
            for i, h in enumerate(range(g * heads_per_group, (g + 1) * heads_per_group)):
                lr = slice(i * SSM_HEAD_DIM, (i + 1) * SSM_HEAD_DIM)
                decay = jnp.exp(jnp.broadcast_to(a_last[:, h:h + 1], (SSM_HEAD_DIM, D_STATE)))
                st_ref[b, h * SSM_HEAD_DIM:(h + 1) * SSM_HEAD_DIM, :] = decay * h_prev[lr, :] + new[lr, :]
            return carry

        lax.fori_loop(0, SAMPLE_SEQS, per_seq, 0, unroll=4)
        for p in range(group_rows // LANES):
            blk = g * (group_rows // LANES) + p
            lanes = slice(blk * LANES, (blk + 1) * LANES)
            y_diag, grow = _ssd_diag(xc_s, blk, cb, a_cum, a_cum_t, dt_t, seg_mask)
            y_parts.append(y_diag + yoff_s[:, lanes] * grow)
    _ssd_finish(xc_s, z_ref, rows, y_parts, dsk_ref, nrm_ref, y_ref)


def _ssd_sample(xbc, prefix_tiles, z, dt_raw, h0, consts):
    rows = xbc.shape[0]
    n_seq = h0.shape[0]
    blk = lambda w: pl.BlockSpec((CHUNK, w), lambda i: (i, 0))
    state_blk = pl.BlockSpec((SAMPLE_SEQS, D_SSM, D_STATE), lambda i: (i, 0, 0))
    return pl.pallas_call(
        _ssd_sample_body,
        grid=(rows // CHUNK,),
        in_specs=[blk(D_XBC), blk(D_XBC), blk(D_SSM), blk(LANES), state_blk]
                 + [_resident(c.shape) for c in consts],
        out_specs=[blk(D_SSM), state_blk],
        out_shape=[jax.ShapeDtypeStruct((rows, D_SSM), BF16),
                   jax.ShapeDtypeStruct((n_seq, D_SSM, D_STATE), F32)],
        scratch_shapes=[pltpu.VMEM((CHUNK, D_XBC), F32), pltpu.VMEM((CHUNK, D_SSM), F32),
                        pltpu.VMEM((CHUNK, LANES), F32)],
        compiler_params=_params(1),
        name="ssd_sample",
    )(xbc, prefix_tiles, z, dt_raw, h0, *consts)


def _out_mlp_body(x_ref, att_ref, ssm_ref, woa_ref, wos_ref, g_ref, wu_ref, wd_ref, gf_ref, o_ref, side_work=()):
    side_work = list(side_work)
    att = jnp.concatenate([att_ref[j] for j in range(att_ref.shape[0])], axis=1).astype(BF16)
    h = (x_ref[...] + jnp.dot(att, woa_ref[...], preferred_element_type=F32)
         + jnp.dot(ssm_ref[...], wos_ref[...], preferred_element_type=F32))
    hn = _rms(h, g_ref[...]).astype(BF16)
    mlp = None
    for c in range(D_FF // FF_TILE):
        if side_work:
            side_work.pop(0)()
        cols = slice(c * FF_TILE, (c + 1) * FF_TILE)
        u = jnp.maximum(jnp.dot(hn, wu_ref[:, cols], preferred_element_type=F32), 0.0)
        down = jnp.dot((u * u).astype(BF16), wd_ref[cols, :], preferred_element_type=F32)
        mlp = down if mlp is None else mlp + down
    for work in side_work:
        work()
    o_ref[...] = _rms(h + mlp, gf_ref[...])


def _out_mlp_attn_body(*refs):
    n_mlp, n_att = 9, 6
    o_ref = refs[n_mlp + n_att]
    pieces = _attn_sample_pieces(*refs[n_mlp:n_mlp + n_att], *refs[n_mlp + n_att + 1:])
    _out_mlp_body(*refs[:n_mlp], o_ref, side_work=pieces)


def _out_mlp(x2d, att4, ssm, w_out_att, w_out_ssm, norm_mlp, w_up, w_down, norm_final, sample=None):
    rows = x2d.shape[0]
    tm = min(ROW_TILE, rows) if sample is None else rows // sample[3].shape[0]
    row_blk = lambda w: pl.BlockSpec((tm, w), lambda i: (i, 0))
    operands = [x2d, att4, ssm, w_out_att, w_out_ssm, norm_mlp, w_up, w_down, norm_final]
    in_specs = [row_blk(D_MODEL), pl.BlockSpec((4, tm, LANES), lambda i: (0, i, 0)), row_blk(D_SSM),
                _resident((D_ATT, D_MODEL)), _resident((D_SSM, D_MODEL)), _resident((1, D_MODEL)),
                _resident((D_MODEL, D_FF)), _resident((D_FF, D_MODEL)), _resident((1, D_MODEL))]
    out_specs = [row_blk(D_MODEL)]
    out_shape = [jax.ShapeDtypeStruct((rows, D_MODEL), F32)]
    body = _out_mlp_body
    if sample is not None:
        att_operands, att_in, att_out, att_shape = _attn_sample_specs(*sample)
        assert rows == tm * sample[3].shape[0] and tm % SUBLANES == 0
        operands += att_operands
        in_specs += att_in
        out_specs += att_out
        out_shape += att_shape
        body = _out_mlp_attn_body
    out = pl.pallas_call(
        body,
        grid=(rows // tm,),
        in_specs=in_specs,
        out_specs=out_specs,
        out_shape=out_shape,
        compiler_params=_params(1),
        name="out_mlp",
    )(*operands)
    return out[0] if sample is None else out


def _q_permutation():
    perm = []
    for pp in range(N_KV_HEADS // 2):
        for g in range(Q_PER_KV):
            for e in range(2):
                base = (2 * pp + e) * Q_PER_KV * HEAD_DIM + g * HEAD_DIM
                perm += list(range(base, base + HEAD_DIM))
    return np.asarray(perm, np.int32)


def _to_lanes(buf):
    n, pos = buf.shape[:2]
    return jnp.transpose(buf, (0, 2, 3, 1)).reshape(n, D_KV, pos)


def _from_lanes(buf_t):
    n, _, pos = buf_t.shape
    return jnp.transpose(buf_t.reshape(n, N_KV_HEADS, HEAD_DIM, pos), (0, 3, 1, 2))[None]


def _head_rows(v):
    return jnp.broadcast_to(v.astype(F32)[:, None], (v.shape[0], LANES))


def kernel(x_prompt, x_sample, cache_k, cache_v, state_conv, state_ssm, w_in, w_out, conv_w, conv_b, dt_bias, a_log, d_skip, ssm_norm, norm_mix, norm_mlp, w_up, w_down, norm_final):
    depth = w_in.shape[0]
    assert depth == 1, "single-layer step"
    n_p, seq, _ = x_prompt.shape
    n_s, t_new, _ = x_sample.shape
    n_past = cache_k.shape[2]
    assert seq == MAX_WINDOW and n_past == MAX_WINDOW and t_new == SUBLANES and n_s % SAMPLE_SEQS == 0

    perm = _q_permutation()
    w = w_in[0]
    w_pad = jnp.pad(w.astype(BF16), ((0, 0), (0, D_IN_PAD - w.shape[1])))
    w_out_att = w_out[0][perm, :].astype(BF16)
    w_out_ssm = w_out[0][D_ATT:, :].astype(BF16)
    w_up_b, w_down_b = w_up[0].astype(BF16), w_down[0].astype(BF16)
    g_mix, g_mlp, g_fin = (v.reshape(1, D_MODEL) for v in (norm_mix[0], norm_mlp[0], norm_final))
    ssd_consts = (conv_w[0], conv_b[0].reshape(1, D_XBC), _head_rows(dt_bias[0]), _head_rows(a_log[0]),
                  jnp.repeat(d_skip[0], SSM_HEAD_DIM).reshape(1, D_SSM), ssm_norm[0].reshape(1, D_SSM))

    xp = x_prompt.reshape(n_p * seq, D_MODEL)
    q4, k, v, z, xbc, dt_raw, k_t, v_t = _in_proj(xp, g_mix, w_pad, _rope_table(np.arange(seq)), seq=seq)
    att4 = _attn_prompt(q4, k, v, n_p, seq)
    y_ssm, st_p = _ssd_prompt(xbc, z, dt_raw, n_p, seq, ssd_consts)
    k_prompt, v_prompt = _from_lanes(k_t), _from_lanes(v_t)
    conv_prompt = xbc.reshape(1, n_p, seq, D_XBC)[:, :, seq - (CONV_WIDTH - 1):]

    xs = x_sample.reshape(n_s * t_new, D_MODEL)
    rows_tile = min(ROW_TILE, n_s * t_new)
    pos = PAST_LEN + (np.arange(rows_tile) % t_new)
    q4s, ks, vs, zs, xbcs, dts = _in_proj(xs, g_mix, w_pad, _rope_table(pos))
    y_prompt, att4s, k_sample, v_sample = _out_mlp(
        xp, att4, y_ssm, w_out_att, w_out_ssm, g_mlp, w_up_b, w_down_b, g_fin,
        sample=(q4s, ks, vs, _to_lanes(cache_k[0]), _to_lanes(cache_v[0])))
    prefix_tiles = jnp.pad(state_conv[0], ((0, 0), (t_new - (CONV_WIDTH - 1), 0), (0, 0))).reshape(n_s * t_new, D_XBC)
    y_ssm_s, st_s = _ssd_sample(xbcs, prefix_tiles, zs, dts, state_ssm[0].reshape(n_s, D_SSM, D_STATE), ssd_consts)
    y_sample = _out_mlp(xs, att4s, y_ssm_s, w_out_att, w_out_ssm, g_mlp, w_up_b, w_down_b, g_fin)
    conv_sample = xbcs.reshape(1, n_s, t_new, D_XBC)[:, :, t_new - (CONV_WIDTH - 1):]

    return (y_prompt.reshape(n_p, seq, D_MODEL), y_sample.reshape(n_s, t_new, D_MODEL),
            k_prompt, v_prompt, conv_prompt,
            st_p.reshape(1, n_p, N_SSM_HEADS, SSM_HEAD_DIM, D_STATE),
            _from_lanes(k_sample), _from_lanes(v_sample), conv_sample,
            st_s.reshape(1, n_s, N_SSM_HEADS, SSM_HEAD_DIM, D_STATE))
```

```python
import functools

import numpy as np
import jax
import jax.numpy as jnp
from jax import lax
from jax.experimental import pallas as pl
from jax.experimental.pallas import tpu as pltpu

F32 = jnp.float32
BF16 = jnp.bfloat16

LANES = 128
SUBLANES = 8

D_MODEL = 1024
HEAD_DIM = 64
N_KV_HEADS = 4
Q_PER_KV = 2
D_ATT = 512
D_KV = 256
ROT_HALF = 8
ROPE_THETA = 500000.0
DILATIONS = (1, 4, 16)
N_KEYS = 128
MAX_WINDOW = 2048
ATT_BLK = 128
ATT_UNITS = 16
REGROUP_STRIDE = 4
RESIDUES = 16
PAST_LEN = 8192
ATT_SCALE = HEAD_DIM ** -0.5
NEG_BIG = -1e30
N_SSM_HEADS = 8
SSM_HEAD_DIM = 64
D_SSM = 512
SSM_GROUPS = 2
D_STATE = 128
CONV_WIDTH = 4
CHUNK = 128
D_XBC = 1024
D_FF = 4096
RMS_EPS = 1e-5
D_IN_MAIN = D_ATT + 2 * D_KV + D_SSM + D_XBC
D_IN_PAD = D_IN_MAIN + LANES

VMEM_LIMIT = 56 * 1024 * 1024
ROW_TILE = 512
FF_TILE = 1024
SSD_CHUNKS_PER_STEP = 4
SAMPLE_SEQS = CHUNK // 8

_NT = (((1,), (1,)), ((), ()))


def _params(n_axes):
    return pltpu.CompilerParams(dimension_semantics=("arbitrary",) * n_axes,
                                vmem_limit_bytes=VMEM_LIMIT)


def _resident(shape):
    return pl.BlockSpec(shape, lambda *_: (0,) * len(shape), pipeline_mode=pl.Buffered(1))


def _rms(x, g):
    return x * lax.rsqrt(jnp.mean(x * x, axis=-1, keepdims=True) + RMS_EPS) * g


def _silu(x):
    half = 0.5 * x
    return half + half * jnp.tanh(half)


def _softplus(x):
    return jnp.maximum(x, 0.0) + jnp.log1p(jnp.exp(-jnp.abs(x)))


def _in_proj_body(x_ref, g_ref, w_ref, rope_ref, q_ref, k_ref, v_ref, z_ref, xbc_ref, dt_ref, *kv_t_refs):
    xb = _rms(x_ref[...], g_ref[...]).astype(BF16)
    tm = xb.shape[0]
    first_half = (lax.broadcasted_iota(jnp.int32, (tm, LANES), 1) % HEAD_DIM) < ROT_HALF

    def proj(lo, hi):
        return jnp.dot(xb, w_ref[:, lo:hi], preferred_element_type=F32)

    def rope(u, cos, sin):
        partner = jnp.where(first_half, pltpu.roll(u, LANES - ROT_HALF, 1), pltpu.roll(u, ROT_HALF, 1))
        return u * cos + partner * sin

    cq, sq = rope_ref[:, 0:128], rope_ref[:, 128:256]
    ck, sk = rope_ref[:, 256:384], rope_ref[:, 384:512]
    q = proj(0, D_ATT)
    q = [rope(q[:, j * LANES:(j + 1) * LANES], cq, sq) for j in range(D_ATT // LANES)]
    low_half = lax.broadcasted_iota(jnp.int32, (tm, LANES), 1) < HEAD_DIM
    for pp in range(N_KV_HEADS // 2):
        head0, head1 = q[2 * pp], q[2 * pp + 1]
        q_ref[Q_PER_KV * pp] = jnp.where(low_half, head0, pltpu.roll(head1, HEAD_DIM, 1))
        q_ref[Q_PER_KV * pp + 1] = jnp.where(low_half, pltpu.roll(head0, HEAD_DIM, 1), head1)
    k = proj(D_ATT, D_ATT + D_KV)
    v = proj(D_ATT + D_KV, D_ATT + 2 * D_KV)
    v_ref[...] = v
    for j in range(D_KV // LANES):
        lanes = slice(j * LANES, (j + 1) * LANES)
        kj = rope(k[:, lanes], ck, sk)
        k_ref[:, lanes] = kj
        if kv_t_refs:
            kv_t_refs[0][lanes, :] = kj.T
            kv_t_refs[1][lanes, :] = v[:, lanes].T
    z_ref[...] = _silu(proj(D_ATT + 2 * D_KV, D_ATT + 2 * D_KV + D_SSM))
    xbc_ref[...] = proj(D_ATT + 2 * D_KV + D_SSM, D_IN_MAIN)
    dt_ref[...] = proj(D_IN_MAIN, D_IN_PAD)


def _in_proj(x2d, norm_mix, w_pad, rope_tab, seq=None):
    rows = x2d.shape[0]
    tm = min(ROW_TILE, rows)
    n_tab = rope_tab.shape[0] // tm
    row_blk = lambda w: pl.BlockSpec((tm, w), lambda i: (i, 0))
    out_specs = [pl.BlockSpec((4, tm, LANES), lambda i: (0, i, 0)),
                 row_blk(D_KV), row_blk(D_KV), row_blk(D_SSM), row_blk(D_XBC), row_blk(LANES)]
    out_shape = [jax.ShapeDtypeStruct((4, rows, LANES), F32),
                 jax.ShapeDtypeStruct((rows, D_KV), F32), jax.ShapeDtypeStruct((rows, D_KV), F32),
                 jax.ShapeDtypeStruct((rows, D_SSM), F32), jax.ShapeDtypeStruct((rows, D_XBC), F32),
                 jax.ShapeDtypeStruct((rows, LANES), F32)]
    if seq is not None:
        per_seq = seq // tm
        t_blk = pl.BlockSpec((None, D_KV, tm), lambda i: (i // per_seq, 0, i % per_seq))
        out_specs += [t_blk, t_blk]
        out_shape += [jax.ShapeDtypeStruct((rows // seq, D_KV, seq), F32)] * 2
    return pl.pallas_call(
        _in_proj_body,
        grid=(rows // tm,),
        in_specs=[row_blk(D_MODEL), _resident((1, D_MODEL)), _resident((D_MODEL, D_IN_PAD)),
                  pl.BlockSpec((tm, 4 * LANES), lambda i: (i % n_tab, 0))],
        out_specs=out_specs,
        out_shape=out_shape,
        compiler_params=_params(1),
        name="in_proj",
    )(x2d, norm_mix, w_pad, rope_tab)


def _rope_table(pos):
    pos = np.asarray(pos, np.float64)
    inv = ROPE_THETA ** (-np.arange(0, 2 * ROT_HALF, 2, dtype=np.float64) / (2 * ROT_HALF))
    ang = pos[:, None] * inv[None, :]
    cos, sin = np.cos(ang), np.sin(ang)
    ones = np.ones((pos.shape[0], HEAD_DIM - 2 * ROT_HALF))
    cos_h = np.concatenate([cos, cos, ones], axis=1)
    sin_h = np.concatenate([-sin, sin, 0.0 * ones], axis=1)
    cos_l, sin_l = np.tile(cos_h, (1, 2)), np.tile(sin_h, (1, 2))
    table = np.concatenate([cos_l * ATT_SCALE, sin_l * ATT_SCALE, cos_l, sin_l], axis=1)
    return jnp.asarray(table.astype(np.float32))


def _attn_prompt_body(q_ref, k_ref, v_ref, bias1_ref, bias4_ref, caus_ref, o_ref,
                      q3, k3, vhead3, m3, l3, acc3, nat_s):
    seq = k_ref.shape[0]
    per_res = seq // RESIDUES
    e0 = lax.broadcasted_iota(jnp.int32, (ATT_BLK, LANES), 1) < HEAD_DIM
    zero = jnp.zeros((ATT_BLK, LANES), F32)

    def regroup(src, tmp, dst):
        quarter = seq // REGROUP_STRIDE
        for r in range(REGROUP_STRIDE):
            tmp[r * quarter:(r + 1) * quarter, :] = src[pl.ds(r, quarter, stride=REGROUP_STRIDE), :]
        for r in range(REGROUP_STRIDE):
            for a in range(RESIDUES // REGROUP_STRIDE):
                dst[r + REGROUP_STRIDE * a] = tmp[pl.ds(r * quarter + a, per_res, stride=REGROUP_STRIDE), :]

    for g in range(Q_PER_KV):
        regroup(q_ref.at[g], nat_s.at[g], q3.at[g])
    regroup(k_ref, nat_s.at[Q_PER_KV], k3)
    regroup(v_ref, nat_s.at[Q_PER_KV + 1], vhead3.at[0])
    for r in range(RESIDUES):
        v = vhead3[0, r]
        vhead3[0, r] = jnp.where(e0, v, 0.0)
        vhead3[1, r] = jnp.where(e0, 0.0, v)

    def scores(q_idx, k_idx, bias):
        parts = []
        for g in range(Q_PER_KV):
            qg = q3.at[g][q_idx].reshape(ATT_BLK, LANES)
            parts += [jnp.where(e0, qg, zero), jnp.where(e0, zero, qg)]
        lhs = jnp.concatenate(parts, axis=0).astype(BF16)
        kb = k3[k_idx]
        nk = kb.shape[0] * kb.shape[1]
        kb = kb.reshape(nk, LANES).astype(BF16)
        s = lax.dot_general(lhs, kb, _NT, preferred_element_type=F32)
        s = (s.reshape(4, ATT_BLK, nk) + bias[None]).reshape(4 * ATT_BLK, nk)
        m = jnp.max(s, axis=1, keepdims=True)
        p = jnp.exp(s - m)
        mb = jnp.broadcast_to(m, (4 * ATT_BLK, LANES))
        key_lane = lax.broadcasted_iota(jnp.int32, (nk, LANES), 1) < HEAD_DIM
        ones_of = (jnp.where(key_lane, 1.0, 0.0), jnp.where(key_lane, 0.0, 1.0))
        rhs = jnp.concatenate(
            [jnp.concatenate([vhead3.at[e][k_idx].reshape(nk, LANES), ones_of[e]], axis=1) for e in range(2)],
            axis=0).astype(BF16)
        p2 = jnp.concatenate(
            [jnp.concatenate([p[(2 * g + e) * ATT_BLK:(2 * g + e + 1) * ATT_BLK] for e in range(2)], axis=1)
             for g in range(Q_PER_KV)], axis=0).astype(BF16)
        pvl = jnp.dot(p2, rhs, preferred_element_type=F32)
        out = []
        for g in range(Q_PER_KV):
            r0, r1, r2 = 2 * g * ATT_BLK, (2 * g + 1) * ATT_BLK, (2 * g + 2) * ATT_BLK
            blk = pvl[g * ATT_BLK:(g + 1) * ATT_BLK]
            out.append((jnp.where(e0, mb[r0:r1], mb[r1:r2]), blk[:, LANES:], blk[:, :LANES]))
        return out

    def process(units, first_group):
        state = [(m3.at[g], l3.at[g], acc3.at[g]) for g in range(Q_PER_KV)]
        new = [scores(*unit) for unit in units]
        for i, (q_idx, _, _) in enumerate(units):
            for g in range(Q_PER_KV):
                mg, lg, acc = new[i][g]
                if not first_group:
                    m_old, l_old, acc_old = (ref[q_idx].reshape(ATT_BLK, LANES) for ref in state[g])
                    m_new = jnp.maximum(m_old, mg)
                    a_old = jnp.exp(m_old - m_new)
                    a_new = jnp.exp(mg - m_new)
                    mg, lg, acc = m_new, a_old * l_old + a_new * lg, a_old * acc_old + a_new * acc
                for ref, val in zip(state[g], (mg, lg, acc)):
                    ref[q_idx] = val.reshape(ref[q_idx].shape)

    for gi, (d, bias_ref) in enumerate(zip(DILATIONS, (bias1_ref, bias4_ref, caus_ref))):
        blocks = seq // d // ATT_BLK
        lanes16 = RESIDUES // d
        depth = ATT_BLK // lanes16

        def unit(u, d=d, blocks=blocks, lanes16=lanes16, depth=depth, bias_ref=bias_ref):
            r, n = u // blocks, u % blocks
            lead = pl.ds(r, lanes16, stride=d)
            q_idx = (lead, pl.ds(pl.multiple_of(n * depth, depth), depth))
            if blocks == 1:
                return q_idx, q_idx, bias_ref[...]
            k_idx = (lead, pl.ds(pl.multiple_of(jnp.maximum(n - 1, 0) * depth, depth), 2 * depth))
            return q_idx, k_idx, bias_ref[jnp.minimum(n, 1)]

        def step(i, carry, unit=unit, first_group=(gi == 0)):
            process([unit(ATT_UNITS * i + j) for j in range(ATT_UNITS)], first_group)
            return carry

        lax.fori_loop(0, d * blocks // ATT_UNITS, step, 0)

    quarter = seq // REGROUP_STRIDE
    for g in range(Q_PER_KV):
        tmp, out = nat_s.at[Q_PER_KV + g], nat_s.at[g]
        for r in range(REGROUP_STRIDE):
            for a in range(RESIDUES // REGROUP_STRIDE):
                res = r + REGROUP_STRIDE * a
                tmp[pl.ds(r * quarter + a, per_res, stride=REGROUP_STRIDE), :] = acc3[g, res] / l3[g, res]
        for r in range(REGROUP_STRIDE):
            out[pl.ds(r, quarter, stride=REGROUP_STRIDE), :] = tmp[r * quarter:(r + 1) * quarter, :]
        o_ref[g] = out[...].astype(o_ref.dtype)


def _window_bias(d):
    lanes16 = RESIDUES // d

    def order(n):
        a, ll = np.divmod(np.arange(n), n // lanes16)
        return lanes16 * ll + a

    i = order(ATT_BLK)[:, None]
    if d == RESIDUES:
        return jnp.asarray(np.where(order(ATT_BLK)[None, :] <= i, 0.0, NEG_BIG).astype(np.float32))
    j = order(2 * ATT_BLK)[None, :]
    dist = i + ATT_BLK - j
    inner = np.where((dist >= 0) & (dist <= N_KEYS), 0.0, NEG_BIG)
    first = np.where(j <= i, 0.0, NEG_BIG)
    return jnp.asarray(np.stack([first, inner]).astype(np.float32))


def _attn_prompt(q4, k, v, n_seq, seq):
    biases = [_window_bias(d) for d in DILATIONS]
    slab = (RESIDUES, seq // RESIDUES, LANES)
    return pl.pallas_call(
        _attn_prompt_body,
        grid=(n_seq, 2),
        in_specs=[pl.BlockSpec((2, seq, LANES), lambda b, pp: (pp, b, 0)),
                  pl.BlockSpec((seq, LANES), lambda b, pp: (b, pp)),
                  pl.BlockSpec((seq, LANES), lambda b, pp: (b, pp))] + [_resident(x.shape) for x in biases],
        out_specs=pl.BlockSpec((2, seq, LANES), lambda b, pp: (pp, b, 0)),
        out_shape=jax.ShapeDtypeStruct(q4.shape, BF16),
        scratch_shapes=[pltpu.VMEM((2,) + slab, F32), pltpu.VMEM(slab, F32), pltpu.VMEM((2,) + slab, F32)]
                       + [pltpu.VMEM((2,) + slab, F32)] * 3 + [pltpu.VMEM((Q_PER_KV + 2, seq, LANES), F32)],
        compiler_params=_params(2),
        name="attn_prompt",
    )(q4, k, v, *biases)


def _attn_sample_pieces(q_ref, kn_ref, vn_ref, ck_ref, cv_ref, mult_ref, att_ref, ko_ref, vo_ref):
    t = kn_ref.shape[0]
    n_past = ck_ref.shape[2]
    tail = slice(n_past - LANES, n_past)

    def shift(c_ref, n_ref, o_ref, rows):
        is_new = lax.broadcasted_iota(jnp.int32, (LANES, LANES), 1) >= LANES - t
        moved = pltpu.roll(c_ref[0, rows, :], n_past - t, 1)
        o_ref[0, rows, :] = moved
        new_rows = jnp.concatenate([n_ref[:, rows], jnp.zeros((LANES - t, LANES), F32)], axis=0)
        o_ref[0, rows, tail] = jnp.where(is_new, pltpu.roll(new_rows.T, LANES - t, 1), moved[:, tail])

    def attend():
        mult = mult_ref[...]
        e0 = lax.broadcasted_iota(jnp.int32, (t, LANES), 1) < HEAD_DIM
        zero = jnp.zeros((t, LANES), F32)
        pairs = N_KV_HEADS // 2
        parts = []
        for pp in range(pairs):
            for g in range(Q_PER_KV):
                qg = q_ref[pp * Q_PER_KV + g]
                for masked in (jnp.where(e0, qg, zero), jnp.where(e0, zero, qg)):
                    parts.append(jnp.concatenate([masked if j == pp else zero for j in range(pairs)], axis=1))
        lhs = jnp.concatenate(parts, axis=0).astype(BF16)
        pad = jnp.zeros((LANES - t, D_KV), F32)
        kn = jnp.concatenate([kn_ref[...], pad], axis=0).astype(BF16)
        vn = jnp.concatenate([vn_ref[...], pad], axis=0).astype(BF16)
        s = jnp.concatenate([jnp.dot(lhs, ck_ref[0].astype(BF16), preferred_element_type=F32),
                             lax.dot_general(lhs, kn, _NT, preferred_element_type=F32)], axis=1)
        s = jnp.where(mult > 0.0, s, NEG_BIG)
        m = jnp.max(s, axis=1, keepdims=True)
        p = jnp.exp(s - m) * mult
        l = jnp.sum(p, axis=1, keepdims=True)
        pb = p.astype(BF16)
        pv = (lax.dot_general(pb[:, :n_past], cv_ref[0].astype(BF16), _NT, preferred_element_type=F32)
              + jnp.dot(pb[:, n_past:], vn, preferred_element_type=F32))
        o = pv / l
        for pp in range(pairs):
            lanes = slice(pp * LANES, (pp + 1) * LANES)
            for g in range(Q_PER_KV):
                r0 = (pp * Q_PER_KV + g) * 2 * t
                att_ref[pp * Q_PER_KV + g] = jnp.where(e0, o[r0:r0 + t, lanes], o[r0 + t:r0 + 2 * t, lanes])

    pieces = [attend]
    for pp in range(N_KV_HEADS // 2):
        rows = slice(pp * LANES, (pp + 1) * LANES)
        pieces += [functools.partial(shift, ck_ref, kn_ref, ko_ref, rows),
                   functools.partial(shift, cv_ref, vn_ref, vo_ref, rows)]
    return pieces


def _key_multiplicity(t, n_past):
    idx = np.concatenate([np.arange(n_past + t), np.full(LANES - t, 10 ** 9)])[None, :]
    dist = n_past + np.arange(t)[:, None] - idx
    mult = np.zeros(dist.shape, np.float32)
    for d in DILATIONS:
        mult += (dist >= 0) & (dist % d == 0) & (dist <= N_KEYS * d)
    return jnp.asarray(np.tile(mult, (2 * N_KV_HEADS, 1)))


def _attn_sample_specs(q4, k_new, v_new, cache_k, cache_v):
    n_seq, _, n_past = cache_k.shape
    t = k_new.shape[0] // n_seq
    mult = _key_multiplicity(t, n_past)
    cache_blk = pl.BlockSpec((1, D_KV, n_past), lambda b: (b, 0, 0))
    new_blk = pl.BlockSpec((t, D_KV), lambda b: (b, 0))
    q_blk = pl.BlockSpec((4, t, LANES), lambda b: (0, b, 0))
    return ([q4, k_new, v_new, cache_k, cache_v, mult],
            [q_blk, new_blk, new_blk, cache_blk, cache_blk, _resident(mult.shape)],
            [q_blk, cache_blk, cache_blk],
            [jax.ShapeDtypeStruct(q4.shape, F32),
             jax.ShapeDtypeStruct(cache_k.shape, F32), jax.ShapeDtypeStruct(cache_v.shape, F32)])


def _conv_silu(delayed, cw_ref, cb_ref, xc_s):
    for j in range(D_XBC // LANES):
        lanes = slice(j * LANES, (j + 1) * LANES)
        y = cb_ref[:, lanes]
        for tap in range(CONV_WIDTH):
            y = y + delayed(CONV_WIDTH - 1 - tap, lanes) * cw_ref[tap:tap + 1, lanes]
        xc_s[:, lanes] = _silu(y)


def _select_sum(x, sel):
    heads = x.shape[0]
    x = jnp.concatenate([x, jnp.zeros((LANES - heads, x.shape[1]), F32)], axis=0)
    total = None
    for _ in range(3):
        piece = x.astype(BF16)
        x = x - piece.astype(F32)
        part = jnp.dot(piece, sel, preferred_element_type=F32)
        total = part if total is None else total + part
    return total[0:heads]


def _to_columns(x_t):
    pad = jnp.zeros((LANES - x_t.shape[0], x_t.shape[1]), F32)
    return jnp.concatenate([x_t, pad], axis=0).T


def _ssd_scalars(dt_raw, dtb_ref, alog_ref, before, seg_end=None):
    dt_t = _softplus(dt_raw.T[0:N_SSM_HEADS] + dtb_ref[...])
    a_t = dt_t * (-jnp.exp(alog_ref[...]))
    a_cum_t = _select_sum(a_t, before)
    if seg_end is None:
        a_end_t = jnp.broadcast_to(a_cum_t[:, CHUNK - 1:CHUNK], a_cum_t.shape)
    else:
        a_end_t = _select_sum(a_cum_t, seg_end)
    to_end_t = jnp.exp(a_end_t - a_cum_t) * dt_t
    return dt_t, _to_columns(a_cum_t), a_cum_t, a_end_t, to_end_t


def _ssd_diag(xc_s, blk, cb, a_cum, a_cum_t, dt_t, seg_mask):
    e0 = lax.broadcasted_iota(jnp.int32, (CHUNK, LANES), 1) < SSM_HEAD_DIM
    x_pair = xc_s[:, blk * LANES:(blk + 1) * LANES].astype(BF16)
    out, grow = [], []
    for e in range(2):
        h = 2 * blk + e
        a_col = jnp.broadcast_to(a_cum[:, h:h + 1], (CHUNK, CHUNK))
        seg = a_col - jnp.broadcast_to(a_cum_t[h:h + 1, :], (CHUNK, CHUNK))
        w = cb * jnp.where(seg_mask, jnp.exp(seg), 0.0) * dt_t[h:h + 1, :]
        out.append(jnp.dot(w.astype(BF16), x_pair, preferred_element_type=F32))
        grow.append(jnp.exp(a_col))
    return jnp.where(e0, out[0], out[1]), jnp.where(e0, grow[0], grow[1])


def _ssd_finish(xc_s, gate_ref, rows, y_parts, dsk_ref, nrm_ref, y_ref):
    blocks_per_group = D_SSM // SSM_GROUPS // LANES
    for g in range(SSM_GROUPS):
        gated = []
        for p in range(blocks_per_group):
            blk = g * blocks_per_group + p
            lanes = slice(blk * LANES, (blk + 1) * LANES)
            y = y_parts[blk] + dsk_ref[:, lanes] * xc_s[:, lanes]
            gated.append(y * gate_ref[rows, lanes])
        ss = sum(jnp.sum(y * y, axis=-1, keepdims=True) for y in gated)
        inv = lax.rsqrt(ss / (D_SSM // SSM_GROUPS) + RMS_EPS)
        for p in range(blocks_per_group):
            blk = g * blocks_per_group + p
            lanes = slice(blk * LANES, (blk + 1) * LANES)
            y_ref[rows, lanes] = (gated[p] * inv * nrm_ref[:, lanes]).astype(y_ref.dtype)


def _ssd_prompt_body(xbc_ref, gate_ref, dt_ref, cw_ref, cb_ref, dtb_ref, alog_ref, dsk_ref, nrm_ref,
                     y_ref, st_ref, st_s, xc_bufs):
    seq = xbc_ref.shape[0]
    st_s[...] = jnp.zeros_like(st_s)
    row = lax.broadcasted_iota(jnp.int32, (CHUNK, CHUNK), 0)
    col = lax.broadcasted_iota(jnp.int32, (CHUNK, CHUNK), 1)
    causal = row >= col
    before = jnp.where(row <= col, 1.0, 0.0).astype(BF16)
    row8 = lax.broadcasted_iota(jnp.int32, (SUBLANES, LANES), 0)
    heads_per_group = N_SSM_HEADS // SSM_GROUPS
    group_rows = heads_per_group * SSM_HEAD_DIM

    def chunk(c, xc_s):
        rows = pl.ds(pl.multiple_of(c * CHUNK, CHUNK), CHUNK)
        prev_rows = pl.ds(pl.multiple_of(jnp.maximum(c * CHUNK - SUBLANES, 0), SUBLANES), SUBLANES)
        have_prev = c > 0

        def delay(x, before8, k):
            rolled = pltpu.roll(x, k, 0)
            head = jnp.where(row8 < k, pltpu.roll(before8, k, 0), rolled[0:SUBLANES])
            return jnp.concatenate([head, rolled[SUBLANES:]], axis=0)

        for j in range(D_XBC // LANES):
            lanes = slice(j * LANES, (j + 1) * LANES)
            w = [cw_ref[tap:tap + 1, lanes] for tap in range(CONV_WIDTH)]
            cur = xbc_ref[rows, lanes]
            prev = jnp.where(have_prev, xbc_ref[prev_rows, lanes], 0.0)
            cur_1 = delay(cur, prev, 1)
            older = cur * w[1] + cur_1 * w[0]
            older_prev = prev * w[1] + pltpu.roll(prev, 1, 0) * w[0]
            y = cb_ref[:, lanes] + (cur * w[3] + cur_1 * w[2]) + delay(older, older_prev, 2)
            xc_s[:, lanes] = _silu(y)
        dt_t, a_cum, a_cum_t, a_end_t, to_end_t = _ssd_scalars(dt_ref[rows, :], dtb_ref, alog_ref, before)
        x_t = xc_s[:, 0:D_SSM].T
        y_parts = []
        for g in range(SSM_GROUPS):
            bg = xc_s[:, D_SSM + g * D_STATE:D_SSM + (g + 1) * D_STATE].astype(BF16)
            cg = xc_s[:, D_SSM + (SSM_GROUPS + g) * D_STATE:D_SSM + (SSM_GROUPS + g + 1) * D_STATE].astype(BF16)
            cb = lax.dot_general(cg, bg, _NT, preferred_element_type=F32)
            grp = slice(g * group_rows, (g + 1) * group_rows)
            st_prev = st_s[grp, :]
            y_off = lax.dot_general(cg, st_prev.astype(BF16), _NT, preferred_element_type=F32)
            for p in range(group_rows // LANES):
                blk = g * (group_rows // LANES) + p
                y_diag, grow = _ssd_diag(xc_s, blk, cb, a_cum, a_cum_t, dt_t, causal)
                y_parts.append(y_diag + y_off[:, p * LANES:(p + 1) * LANES] * grow)
            xw = []
            for h in range(g * heads_per_group, (g + 1) * heads_per_group):
                hr = slice(h * SSM_HEAD_DIM, (h + 1) * SSM_HEAD_DIM)
                xw.append((x_t[hr, :] * to_end_t[h:h + 1, :]).astype(BF16))
            new = jnp.dot(jnp.concatenate(xw, axis=0), bg, preferred_element_type=F32)
            for i, h in enumerate(range(g * heads_per_group, (g + 1) * heads_per_group)):
                hr = slice(h * SSM_HEAD_DIM, (h + 1) * SSM_HEAD_DIM)
                lr = slice(i * SSM_HEAD_DIM, (i + 1) * SSM_HEAD_DIM)
                decay = jnp.exp(jnp.broadcast_to(a_end_t[h:h + 1, 0:1], (SSM_HEAD_DIM, D_STATE)))
                st_s[hr, :] = decay * st_prev[lr, :] + new[lr, :]
        _ssd_finish(xc_s, gate_ref, rows, y_parts, dsk_ref, nrm_ref, y_ref)

    def step(i, carry):
        for j in range(SSD_CHUNKS_PER_STEP):
            chunk(SSD_CHUNKS_PER_STEP * i + j, xc_bufs.at[j])
        return carry

    lax.fori_loop(0, seq // CHUNK // SSD_CHUNKS_PER_STEP, step, 0)
    st_ref[...] = st_s[...]


def _ssd_prompt(xbc, z, dt_raw, n_seq, seq, consts):
    rows = xbc.shape[0]
    blk = lambda w: pl.BlockSpec((seq, w), lambda b: (b, 0))
    state_blk = pl.BlockSpec((None, D_SSM, D_STATE), lambda b: (b, 0, 0))
    return pl.pallas_call(
        _ssd_prompt_body,
        grid=(n_seq,),
        in_specs=[blk(D_XBC), blk(D_SSM), blk(LANES)] + [_resident(c.shape) for c in consts],
        out_specs=[blk(D_SSM), state_blk],
        out_shape=[jax.ShapeDtypeStruct((rows, D_SSM), BF16),
                   jax.ShapeDtypeStruct((n_seq, D_SSM, D_STATE), F32)],
        scratch_shapes=[pltpu.VMEM((D_SSM, D_STATE), F32),
                        pltpu.VMEM((SSD_CHUNKS_PER_STEP, CHUNK, D_XBC), F32)],
        compiler_params=_params(1),
        name="ssd_prompt",
    )(xbc, z, dt_raw, *consts)


def _ssd_sample_body(xbc_ref, pre_ref, z_ref, dt_ref, h0_ref, cw_ref, cb_ref, dtb_ref, alog_ref, dsk_ref,
                     nrm_ref, y_ref, st_ref, xc_s, yoff_s, aend_s):
    t = CHUNK // SAMPLE_SEQS
    row = lax.broadcasted_iota(jnp.int32, (CHUNK, CHUNK), 0)
    col = lax.broadcasted_iota(jnp.int32, (CHUNK, CHUNK), 1)
    same_seq = (row // t) == (col // t)
    seg_mask = same_seq & (row >= col)
    before = jnp.where(same_seq & (row <= col), 1.0, 0.0).astype(BF16)
    seg_end = jnp.where(row == (col // t) * t + (t - 1), 1.0, 0.0).astype(BF16)
    step = lax.broadcasted_iota(jnp.int32, (CHUNK, LANES), 0) % t
    rows = pl.ds(0, CHUNK)
    heads_per_group = N_SSM_HEADS // SSM_GROUPS
    group_rows = heads_per_group * SSM_HEAD_DIM

    def delayed(k, lanes):
        cur = xbc_ref[:, lanes]
        if k == 0:
            return cur
        return jnp.where(step < k, pltpu.roll(pre_ref[:, lanes], (k - t) % CHUNK, 0), pltpu.roll(cur, k, 0))

    _conv_silu(delayed, cw_ref, cb_ref, xc_s)
    dt_t, a_cum, a_cum_t, a_end_t, to_end_t = _ssd_scalars(dt_ref[...], dtb_ref, alog_ref, before, seg_end)
    aend_s[...] = _to_columns(a_end_t)
    x_t = xc_s[:, 0:D_SSM].T
    col_seq = lax.broadcasted_iota(jnp.int32, (SSM_HEAD_DIM, CHUNK), 1) // t
    y_parts = []
    for g in range(SSM_GROUPS):
        bg = xc_s[:, D_SSM + g * D_STATE:D_SSM + (g + 1) * D_STATE].astype(BF16)
        cg_lanes = slice(D_SSM + (SSM_GROUPS + g) * D_STATE, D_SSM + (SSM_GROUPS + g + 1) * D_STATE)
        cg = xc_s[:, cg_lanes].astype(BF16)
        cb = lax.dot_general(cg, bg, _NT, preferred_element_type=F32)
        grp = slice(g * group_rows, (g + 1) * group_rows)
        xw = []
        for h in range(g * heads_per_group, (g + 1) * heads_per_group):
            hr = slice(h * SSM_HEAD_DIM, (h + 1) * SSM_HEAD_DIM)
            xw.append(x_t[hr, :] * to_end_t[h:h + 1, :])

        def per_seq(b, carry, g=g, bg=bg, cg_lanes=cg_lanes, grp=grp, xw=xw):
            seq_rows = pl.ds(pl.multiple_of(b * t, t), t)
            h_prev = h0_ref[b, grp, :]
            cg_b = xc_s[seq_rows, cg_lanes].astype(BF16)
            yoff_s[seq_rows, g * group_rows:(g + 1) * group_rows] = lax.dot_general(
                cg_b, h_prev.astype(BF16), _NT, preferred_element_type=F32)
            mine = col_seq == b
            xw_b = jnp.concatenate([jnp.where(mine, w, 0.0).astype(BF16) for w in xw], axis=0)
            new = jnp.dot(xw_b, bg, preferred_element_type=F32)
            a_last = aend_s[pl.ds(b * t, 1), :]
            for i, h in enumerate(range(g * heads_per_group, (g + 1) * heads_per_group)):
                lr = slice(i * SSM_HEAD_DIM, (i + 1) * SSM_HEAD_DIM)
                decay = jnp.exp(jnp.broadcast_to(a_last[:, h:h + 1], (SSM_HEAD_DIM, D_STATE)))
                st_ref[b, h * SSM_HEAD_DIM:(h + 1) * SSM_HEAD_DIM, :] = decay * h_prev[lr, :] + new[lr, :]
            return carry

        lax.fori_loop(0, SAMPLE_SEQS, per_seq, 0, unroll=4)
        for p in range(group_rows // LANES):
            blk = g * (group_rows // LANES) + p
            lanes = slice(blk * LANES, (blk + 1) * LANES)
            y_diag, grow = _ssd_diag(xc_s, blk, cb, a_cum, a_cum_t, dt_t, seg_mask)
            y_parts.append(y_diag + yoff_s[:, lanes] * grow)
    _ssd_finish(xc_s, z_ref, rows, y_parts, dsk_ref, nrm_ref, y_ref)


def _ssd_sample(xbc, prefix_tiles, z, dt_raw, h0, consts):
    rows = xbc.shape[0]
    n_seq = h0.shape[0]
    blk = lambda w: pl.BlockSpec((CHUNK, w), lambda i: (i, 0))
    state_blk = pl.BlockSpec((SAMPLE_SEQS, D_SSM, D_STATE), lambda i: (i, 0, 0))
    return pl.pallas_call(
        _ssd_sample_body,
        grid=(rows // CHUNK,),
        in_specs=[blk(D_XBC), blk(D_XBC), blk(D_SSM), blk(LANES), state_blk]
                 + [_resident(c.shape) for c in consts],
        out_specs=[blk(D_SSM), state_blk],
        out_shape=[jax.ShapeDtypeStruct((rows, D_SSM), BF16),
                   jax.ShapeDtypeStruct((n_seq, D_SSM, D_STATE), F32)],
        scratch_shapes=[pltpu.VMEM((CHUNK, D_XBC), F32), pltpu.VMEM((CHUNK, D_SSM), F32),
                        pltpu.VMEM((CHUNK, LANES), F32)],
        compiler_params=_params(1),
        name="ssd_sample",
    )(xbc, prefix_tiles, z, dt_raw, h0, *consts)


def _out_mlp_body(x_ref, att_ref, ssm_ref, woa_ref, wos_ref, g_ref, wu_ref, wd_ref, gf_ref, o_ref, side_work=()):
    side_work = list(side_work)
    att = jnp.concatenate([att_ref[j] for j in range(att_ref.shape[0])], axis=1).astype(BF16)
    h = (x_ref[...] + jnp.dot(att, woa_ref[...], preferred_element_type=F32)
         + jnp.dot(ssm_ref[...], wos_ref[...], preferred_element_type=F32))
    hn = _rms(h, g_ref[...]).astype(BF16)
    mlp = None
    for c in range(D_FF // FF_TILE):
        if side_work:
            side_work.pop(0)()
        cols = slice(c * FF_TILE, (c + 1) * FF_TILE)
        u = jnp.maximum(jnp.dot(hn, wu_ref[:, cols], preferred_element_type=F32), 0.0)
        down = jnp.dot((u * u).astype(BF16), wd_ref[cols, :], preferred_element_type=F32)
        mlp = down if mlp is None else mlp + down
    for work in side_work:
        work()
    o_ref[...] = _rms(h + mlp, gf_ref[...])


def _out_mlp_attn_body(*refs):
    n_mlp, n_att = 9, 6
    o_ref = refs[n_mlp + n_att]
    pieces = _attn_sample_pieces(*refs[n_mlp:n_mlp + n_att], *refs[n_mlp + n_att + 1:])
    _out_mlp_body(*refs[:n_mlp], o_ref, side_work=pieces)


def _out_mlp(x2d, att4, ssm, w_out_att, w_out_ssm, norm_mlp, w_up, w_down, norm_final, sample=None):
    rows = x2d.shape[0]
    tm = min(ROW_TILE, rows) if sample is None else rows // sample[3].shape[0]
    row_blk = lambda w: pl.BlockSpec((tm, w), lambda i: (i, 0))
    operands = [x2d, att4, ssm, w_out_att, w_out_ssm, norm_mlp, w_up, w_down, norm_final]
    in_specs = [row_blk(D_MODEL), pl.BlockSpec((4, tm, LANES), lambda i: (0, i, 0)), row_blk(D_SSM),
                _resident((D_ATT, D_MODEL)), _resident((D_SSM, D_MODEL)), _resident((1, D_MODEL)),
                _resident((D_MODEL, D_FF)), _resident((D_FF, D_MODEL)), _resident((1, D_MODEL))]
    out_specs = [row_blk(D_MODEL)]
    out_shape = [jax.ShapeDtypeStruct((rows, D_MODEL), F32)]
    body = _out_mlp_body
    if sample is not None:
        att_operands, att_in, att_out, att_shape = _attn_sample_specs(*sample)
        assert rows == tm * sample[3].shape[0] and tm % SUBLANES == 0
        operands += att_operands
        in_specs += att_in
        out_specs += att_out
        out_shape += att_shape
        body = _out_mlp_attn_body
    out = pl.pallas_call(
        body,
        grid=(rows // tm,),
        in_specs=in_specs,
        out_specs=out_specs,
        out_shape=out_shape,
        compiler_params=_params(1),
        name="out_mlp",
    )(*operands)
    return out[0] if sample is None else out


def _q_permutation():
    perm = []
    for pp in range(N_KV_HEADS // 2):
        for g in range(Q_PER_KV):
            for e in range(2):
                base = (2 * pp + e) * Q_PER_KV * HEAD_DIM + g * HEAD_DIM
                perm += list(range(base, base + HEAD_DIM))
    return np.asarray(perm, np.int32)


def _to_lanes(buf):
    n, pos = buf.shape[:2]
    return jnp.transpose(buf, (0, 2, 3, 1)).reshape(n, D_KV, pos)


def _from_lanes(buf_t):
    n, _, pos = buf_t.shape
    return jnp.transpose(buf_t.reshape(n, N_KV_HEADS, HEAD_DIM, pos), (0, 3, 1, 2))[None]


def _head_rows(v):
    return jnp.broadcast_to(v.astype(F32)[:, None], (v.shape[0], LANES))


def kernel(x_prompt, x_sample, cache_k, cache_v, state_conv, state_ssm, w_in, w_out, conv_w, conv_b, dt_bias, a_log, d_skip, ssm_norm, norm_mix, norm_mlp, w_up, w_down, norm_final):
    depth = w_in.shape[0]
    assert depth == 1, "single-layer step"
    n_p, seq, _ = x_prompt.shape
    n_s, t_new, _ = x_sample.shape
    n_past = cache_k.shape[2]
    assert seq == MAX_WINDOW and n_past == MAX_WINDOW and t_new == SUBLANES and n_s % SAMPLE_SEQS == 0

    perm = _q_permutation()
    w = w_in[0]
    w_pad = jnp.pad(w.astype(BF16), ((0, 0), (0, D_IN_PAD - w.shape[1])))
    w_out_att = w_out[0][perm, :].astype(BF16)
    w_out_ssm = w_out[0][D_ATT:, :].astype(BF16)
    w_up_b, w_down_b = w_up[0].astype(BF16), w_down[0].astype(BF16)
    g_mix, g_mlp, g_fin = (v.reshape(1, D_MODEL) for v in (norm_mix[0], norm_mlp[0], norm_final))
    ssd_consts = (conv_w[0], conv_b[0].reshape(1, D_XBC), _head_rows(dt_bias[0]), _head_rows(a_log[0]),
                  jnp.repeat(d_skip[0], SSM_HEAD_DIM).reshape(1, D_SSM), ssm_norm[0].reshape(1, D_SSM))

    xp = x_prompt.reshape(n_p * seq, D_MODEL)
    q4, k, v, gate, xbc, dt_raw, k_t, v_t = _in_proj(xp, g_mix, w_pad, _rope_table(np.arange(seq)), seq=seq)
    att4 = _attn_prompt(q4, k, v, n_p, seq)
    y_ssm, st_p = _ssd_prompt(xbc, gate, dt_raw, n_p, seq, ssd_consts)
    k_prompt, v_prompt = _from_lanes(k_t), _from_lanes(v_t)
    conv_prompt = xbc.reshape(1, n_p, seq, D_XBC)[:, :, seq - (CONV_WIDTH - 1):]

    xs = x_sample.reshape(n_s * t_new, D_MODEL)
    rows_tile = min(ROW_TILE, n_s * t_new)
    pos = PAST_LEN + (np.arange(rows_tile) % t_new)
    q4s, ks, vs, zs, xbcs, dts = _in_proj(xs, g_mix, w_pad, _rope_table(pos))
    y_prompt, att4s, k_sample, v_sample = _out_mlp(
        xp, att4, y_ssm, w_out_att, w_out_ssm, g_mlp, w_up_b, w_down_b, g_fin,
        sample=(q4s, ks, vs, _to_lanes(cache_k[0]), _to_lanes(cache_v[0])))
    prefix_tiles = jnp.pad(state_conv[0], ((0, 0), (t_new - (CONV_WIDTH - 1), 0), (0, 0))).reshape(n_s * t_new, D_XBC)
    y_ssm_s, st_s = _ssd_sample(xbcs, prefix_tiles, zs, dts, state_ssm[0].reshape(n_s, D_SSM, D_STATE), ssd_consts)
    y_sample = _out_mlp(xs, att4s, y_ssm_s, w_out_att, w_out_ssm, g_mlp, w_up_b, w_down_b, g_fin)
    conv_sample = xbcs.reshape(1, n_s, t_new, D_XBC)[:, :, t_new - (CONV_WIDTH - 1):]

    return (y_prompt.reshape(n_p, seq, D_MODEL), y_sample.reshape(n_s, t_new, D_MODEL),
            k_prompt, v_prompt, conv_prompt,
            st_p.reshape(1, n_p, N_SSM_HEADS, SSM_HEAD_DIM, D_STATE),
            _from_lanes(k_sample), _from_lanes(v_sample), conv_sample,
            st_s.reshape(1, n_s, N_SSM_HEADS, SSM_HEAD_DIM, D_STATE))
```

```python
import functools

import numpy as np
import jax
import jax.numpy as jnp
from jax import lax
from jax.experimental import pallas as pl
from jax.experimental.pallas import tpu as pltpu

F32 = jnp.float32
BF16 = jnp.bfloat16

LANES = 128
SUBLANES = 8

D_MODEL = 1024
HEAD_DIM = 64
N_KV_HEADS = 4
Q_PER_KV = 2
D_ATT = 512
D_KV = 256
ROT_HALF = 8
ROPE_THETA = 500000.0
DILATIONS = (1, 4, 16)
N_KEYS = 128
MAX_WINDOW = 2048
ATT_BLK = 128
ATT_UNITS = 16
REGROUP_STRIDE = 4
RESIDUES = 16
PAST_LEN = 8192
ATT_SCALE = HEAD_DIM ** -0.5
NEG_BIG = -1e30
N_SSM_HEADS = 8
SSM_HEAD_DIM = 64
D_SSM = 512
SSM_GROUPS = 2
D_STATE = 128
CONV_WIDTH = 4
CHUNK = 128
D_XBC = 1024
D_FF = 4096
RMS_EPS = 1e-5
D_IN_MAIN = D_ATT + 2 * D_KV + D_SSM + D_XBC
D_IN_PAD = D_IN_MAIN + LANES

VMEM_LIMIT = 56 * 1024 * 1024
ROW_TILE = 512
IN_ROW_TILE = 1024
FF_TILE = 1024
SSD_CHUNKS_PER_STEP = 4
SAMPLE_SEQS = CHUNK // 8

_NT = (((1,), (1,)), ((), ()))


def _params(n_axes):
    return pltpu.CompilerParams(dimension_semantics=("arbitrary",) * n_axes,
                                vmem_limit_bytes=VMEM_LIMIT)


def _resident(shape):
    return pl.BlockSpec(shape, lambda *_: (0,) * len(shape), pipeline_mode=pl.Buffered(1))


def _rms(x, g):
    return x * lax.rsqrt(jnp.mean(x * x, axis=-1, keepdims=True) + RMS_EPS) * g


def _silu(x):
    half = 0.5 * x
    return half + half * jnp.tanh(half)


def _softplus(x):
    return jnp.maximum(x, 0.0) + jnp.log1p(jnp.exp(-jnp.abs(x)))


def _in_proj_body(x_ref, g_ref, w_ref, rope_ref, q_ref, k_ref, v_ref, z_ref, xbc_ref, dt_ref, *kv_t_refs):
    xb = _rms(x_ref[...], g_ref[...]).astype(BF16)
    tm = xb.shape[0]
    first_half = (lax.broadcasted_iota(jnp.int32, (tm, LANES), 1) % HEAD_DIM) < ROT_HALF

    def proj(lo, hi):
        return jnp.dot(xb, w_ref[:, lo:hi], preferred_element_type=F32)

    def rope(u, cos, sin):
        partner = jnp.where(first_half, pltpu.roll(u, LANES - ROT_HALF, 1), pltpu.roll(u, ROT_HALF, 1))
        return u * cos + partner * sin

    cq, sq = rope_ref[:, 0:128], rope_ref[:, 128:256]
    ck, sk = rope_ref[:, 256:384], rope_ref[:, 384:512]
    q = proj(0, D_ATT)
    q = [rope(q[:, j * LANES:(j + 1) * LANES], cq, sq) for j in range(D_ATT // LANES)]
    low_half = lax.broadcasted_iota(jnp.int32, (tm, LANES), 1) < HEAD_DIM
    for pp in range(N_KV_HEADS // 2):
        head0, head1 = q[2 * pp], q[2 * pp + 1]
        q_ref[Q_PER_KV * pp] = jnp.where(low_half, head0, pltpu.roll(head1, HEAD_DIM, 1))
        q_ref[Q_PER_KV * pp + 1] = jnp.where(low_half, pltpu.roll(head0, HEAD_DIM, 1), head1)
    k = proj(D_ATT, D_ATT + D_KV)
    v = proj(D_ATT + D_KV, D_ATT + 2 * D_KV)
    v_ref[...] = v
    for j in range(D_KV // LANES):
        lanes = slice(j * LANES, (j + 1) * LANES)
        kj = rope(k[:, lanes], ck, sk)
        k_ref[:, lanes] = kj
        if kv_t_refs:
            kv_t_refs[0][lanes, :] = kj.T
            kv_t_refs[1][lanes, :] = v[:, lanes].T
    z_ref[...] = _silu(proj(D_ATT + 2 * D_KV, D_ATT + 2 * D_KV + D_SSM))
    xbc_ref[...] = proj(D_ATT + 2 * D_KV + D_SSM, D_IN_MAIN)
    dt_ref[...] = proj(D_IN_MAIN, D_IN_PAD)


def _in_proj(x2d, norm_mix, w_pad, rope_tab, seq=None):
    rows = x2d.shape[0]
    tm = min(IN_ROW_TILE, rows)
    n_tab = rope_tab.shape[0] // tm
    row_blk = lambda w: pl.BlockSpec((tm, w), lambda i: (i, 0))
    out_specs = [pl.BlockSpec((4, tm, LANES), lambda i: (0, i, 0)),
                 row_blk(D_KV), row_blk(D_KV), row_blk(D_SSM), row_blk(D_XBC), row_blk(LANES)]
    out_shape = [jax.ShapeDtypeStruct((4, rows, LANES), F32),
                 jax.ShapeDtypeStruct((rows, D_KV), F32), jax.ShapeDtypeStruct((rows, D_KV), F32),
                 jax.ShapeDtypeStruct((rows, D_SSM), F32), jax.ShapeDtypeStruct((rows, D_XBC), F32),
                 jax.ShapeDtypeStruct((rows, LANES), F32)]
    if seq is not None:
        per_seq = seq // tm
        t_blk = pl.BlockSpec((None, D_KV, tm), lambda i: (i // per_seq, 0, i % per_seq))
        out_specs += [t_blk, t_blk]
        out_shape += [jax.ShapeDtypeStruct((rows // seq, D_KV, seq), F32)] * 2
    return pl.pallas_call(
        _in_proj_body,
        grid=(rows // tm,),
        in_specs=[row_blk(D_MODEL), _resident((1, D_MODEL)), _resident((D_MODEL, D_IN_PAD)),
                  pl.BlockSpec((tm, 4 * LANES), lambda i: (i % n_tab, 0))],
        out_specs=out_specs,
        out_shape=out_shape,
        compiler_params=_params(1),
        name="in_proj",
    )(x2d, norm_mix, w_pad, rope_tab)


def _rope_table(pos):
    pos = np.asarray(pos, np.float64)
    inv = ROPE_THETA ** (-np.arange(0, 2 * ROT_HALF, 2, dtype=np.float64) / (2 * ROT_HALF))
    ang = pos[:, None] * inv[None, :]
    cos, sin = np.cos(ang), np.sin(ang)
    ones = np.ones((pos.shape[0], HEAD_DIM - 2 * ROT_HALF))
    cos_h = np.concatenate([cos, cos, ones], axis=1)
    sin_h = np.concatenate([-sin, sin, 0.0 * ones], axis=1)
    cos_l, sin_l = np.tile(cos_h, (1, 2)), np.tile(sin_h, (1, 2))
    table = np.concatenate([cos_l * ATT_SCALE, sin_l * ATT_SCALE, cos_l, sin_l], axis=1)
    return jnp.asarray(table.astype(np.float32))


def _attn_prompt_body(q_ref, k_ref, v_ref, bias1_ref, bias4_ref, caus_ref, o_ref,
                      q3, k3, vhead3, m3, l3, acc3, nat_s):
    seq = k_ref.shape[0]
    per_res = seq // RESIDUES
    e0 = lax.broadcasted_iota(jnp.int32, (ATT_BLK, LANES), 1) < HEAD_DIM
    zero = jnp.zeros((ATT_BLK, LANES), F32)

    def regroup(src, tmp, dst):
        quarter = seq // REGROUP_STRIDE
        for r in range(REGROUP_STRIDE):
            tmp[r * quarter:(r + 1) * quarter, :] = src[pl.ds(r, quarter, stride=REGROUP_STRIDE), :]
        for r in range(REGROUP_STRIDE):
            for a in range(RESIDUES // REGROUP_STRIDE):
                dst[r + REGROUP_STRIDE * a] = tmp[pl.ds(r * quarter + a, per_res, stride=REGROUP_STRIDE), :]

    for g in range(Q_PER_KV):
        regroup(q_ref.at[g], nat_s.at[g], q3.at[g])
    regroup(k_ref, nat_s.at[Q_PER_KV], k3)
    regroup(v_ref, nat_s.at[Q_PER_KV + 1], vhead3.at[0])
    for r in range(RESIDUES):
        v = vhead3[0, r]
        vhead3[0, r] = jnp.where(e0, v, 0.0)
        vhead3[1, r] = jnp.where(e0, 0.0, v)

    def scores(q_idx, k_idx, bias):
        parts = []
        for g in range(Q_PER_KV):
            qg = q3.at[g][q_idx].reshape(ATT_BLK, LANES)
            parts += [jnp.where(e0, qg, zero), jnp.where(e0, zero, qg)]
        lhs = jnp.concatenate(parts, axis=0).astype(BF16)
        kb = k3[k_idx]
        nk = kb.shape[0] * kb.shape[1]
        kb = kb.reshape(nk, LANES).astype(BF16)
        s = lax.dot_general(lhs, kb, _NT, preferred_element_type=F32)
        s = (s.reshape(4, ATT_BLK, nk) + bias[None]).reshape(4 * ATT_BLK, nk)
        m = jnp.max(s, axis=1, keepdims=True)
        p = jnp.exp(s - m)
        mb = jnp.broadcast_to(m, (4 * ATT_BLK, LANES))
        key_lane = lax.broadcasted_iota(jnp.int32, (nk, LANES), 1) < HEAD_DIM
        ones_of = (jnp.where(key_lane, 1.0, 0.0), jnp.where(key_lane, 0.0, 1.0))
        rhs = jnp.concatenate(
            [jnp.concatenate([vhead3.at[e][k_idx].reshape(nk, LANES), ones_of[e]], axis=1) for e in range(2)],
            axis=0).astype(BF16)
        p2 = jnp.concatenate(
            [jnp.concatenate([p[(2 * g + e) * ATT_BLK:(2 * g + e + 1) * ATT_BLK] for e in range(2)], axis=1)
             for g in range(Q_PER_KV)], axis=0).astype(BF16)
        pvl = jnp.dot(p2, rhs, preferred_element_type=F32)
        out = []
        for g in range(Q_PER_KV):
            r0, r1, r2 = 2 * g * ATT_BLK, (2 * g + 1) * ATT_BLK, (2 * g + 2) * ATT_BLK
            blk = pvl[g * ATT_BLK:(g + 1) * ATT_BLK]
            out.append((jnp.where(e0, mb[r0:r1], mb[r1:r2]), blk[:, LANES:], blk[:, :LANES]))
        return out

    def process(units, first_group):
        state = [(m3.at[g], l3.at[g], acc3.at[g]) for g in range(Q_PER_KV)]
        new = [scores(*unit) for unit in units]
        for i, (q_idx, _, _) in enumerate(units):
            for g in range(Q_PER_KV):
                mg, lg, acc = new[i][g]
                if not first_group:
                    m_old, l_old, acc_old = (ref[q_idx].reshape(ATT_BLK, LANES) for ref in state[g])
                    m_new = jnp.maximum(m_old, mg)
                    a_old = jnp.exp(m_old - m_new)
                    a_new = jnp.exp(mg - m_new)
                    mg, lg, acc = m_new, a_old * l_old + a_new * lg, a_old * acc_old + a_new * acc
                for ref, val in zip(state[g], (mg, lg, acc)):
                    ref[q_idx] = val.reshape(ref[q_idx].shape)

    for gi, (d, bias_ref) in enumerate(zip(DILATIONS, (bias1_ref, bias4_ref, caus_ref))):
        blocks = seq // d // ATT_BLK
        lanes16 = RESIDUES // d
        depth = ATT_BLK // lanes16

        def unit(u, d=d, blocks=blocks, lanes16=lanes16, depth=depth, bias_ref=bias_ref):
            r, n = u // blocks, u % blocks
            lead = pl.ds(r, lanes16, stride=d)
            q_idx = (lead, pl.ds(pl.multiple_of(n * depth, depth), depth))
            if blocks == 1:
                return q_idx, q_idx, bias_ref[...]
            k_idx = (lead, pl.ds(pl.multiple_of(jnp.maximum(n - 1, 0) * depth, depth), 2 * depth))
            return q_idx, k_idx, bias_ref[jnp.minimum(n, 1)]

        def step(i, carry, unit=unit, first_group=(gi == 0)):
            process([unit(ATT_UNITS * i + j) for j in range(ATT_UNITS)], first_group)
            return carry

        lax.fori_loop(0, d * blocks // ATT_UNITS, step, 0)

    quarter = seq // REGROUP_STRIDE
    for g in range(Q_PER_KV):
        tmp, out = nat_s.at[Q_PER_KV + g], nat_s.at[g]
        for r in range(REGROUP_STRIDE):
            for a in range(RESIDUES // REGROUP_STRIDE):
                res = r + REGROUP_STRIDE * a
                tmp[pl.ds(r * quarter + a, per_res, stride=REGROUP_STRIDE), :] = acc3[g, res] / l3[g, res]
        for r in range(REGROUP_STRIDE):
            out[pl.ds(r, quarter, stride=REGROUP_STRIDE), :] = tmp[r * quarter:(r + 1) * quarter, :]
        o_ref[g] = out[...].astype(o_ref.dtype)


def _window_bias(d):
    lanes16 = RESIDUES // d

    def order(n):
        a, ll = np.divmod(np.arange(n), n // lanes16)
        return lanes16 * ll + a

    i = order(ATT_BLK)[:, None]
    if d == RESIDUES:
        return jnp.asarray(np.where(order(ATT_BLK)[None, :] <= i, 0.0, NEG_BIG).astype(np.float32))
    j = order(2 * ATT_BLK)[None, :]
    dist = i + ATT_BLK - j
    inner = np.where((dist >= 0) & (dist <= N_KEYS), 0.0, NEG_BIG)
    first = np.where(j <= i, 0.0, NEG_BIG)
    return jnp.asarray(np.stack([first, inner]).astype(np.float32))


def _attn_prompt(q4, k, v, n_seq, seq):
    biases = [_window_bias(d) for d in DILATIONS]
    slab = (RESIDUES, seq // RESIDUES, LANES)
    return pl.pallas_call(
        _attn_prompt_body,
        grid=(n_seq, 2),
        in_specs=[pl.BlockSpec((2, seq, LANES), lambda b, pp: (pp, b, 0)),
                  pl.BlockSpec((seq, LANES), lambda b, pp: (b, pp)),
                  pl.BlockSpec((seq, LANES), lambda b, pp: (b, pp))] + [_resident(x.shape) for x in biases],
        out_specs=pl.BlockSpec((2, seq, LANES), lambda b, pp: (pp, b, 0)),
        out_shape=jax.ShapeDtypeStruct(q4.shape, BF16),
        scratch_shapes=[pltpu.VMEM((2,) + slab, F32), pltpu.VMEM(slab, F32), pltpu.VMEM((2,) + slab, F32)]
                       + [pltpu.VMEM((2,) + slab, F32)] * 3 + [pltpu.VMEM((Q_PER_KV + 2, seq, LANES), F32)],
        compiler_params=_params(2),
        name="attn_prompt",
    )(q4, k, v, *biases)


def _attn_sample_pieces(q_ref, kn_ref, vn_ref, ck_ref, cv_ref, mult_ref, att_ref, ko_ref, vo_ref):
    t = kn_ref.shape[0]
    n_past = ck_ref.shape[2]
    tail = slice(n_past - LANES, n_past)

    def shift(c_ref, n_ref, o_ref, rows):
        is_new = lax.broadcasted_iota(jnp.int32, (LANES, LANES), 1) >= LANES - t
        moved = pltpu.roll(c_ref[0, rows, :], n_past - t, 1)
        o_ref[0, rows, :] = moved
        new_rows = jnp.concatenate([n_ref[:, rows], jnp.zeros((LANES - t, LANES), F32)], axis=0)
        o_ref[0, rows, tail] = jnp.where(is_new, pltpu.roll(new_rows.T, LANES - t, 1), moved[:, tail])

    def attend():
        mult = mult_ref[...]
        e0 = lax.broadcasted_iota(jnp.int32, (t, LANES), 1) < HEAD_DIM
        zero = jnp.zeros((t, LANES), F32)
        pairs = N_KV_HEADS // 2
        parts = []
        for pp in range(pairs):
            for g in range(Q_PER_KV):
                qg = q_ref[pp * Q_PER_KV + g]
                for masked in (jnp.where(e0, qg, zero), jnp.where(e0, zero, qg)):
                    parts.append(jnp.concatenate([masked if j == pp else zero for j in range(pairs)], axis=1))
        lhs = jnp.concatenate(parts, axis=0).astype(BF16)
        pad = jnp.zeros((LANES - t, D_KV), F32)
        kn = jnp.concatenate([kn_ref[...], pad], axis=0).astype(BF16)
        vn = jnp.concatenate([vn_ref[...], pad], axis=0).astype(BF16)
        s = jnp.concatenate([jnp.dot(lhs, ck_ref[0].astype(BF16), preferred_element_type=F32),
                             lax.dot_general(lhs, kn, _NT, preferred_element_type=F32)], axis=1)
        s = jnp.where(mult > 0.0, s, NEG_BIG)
        m = jnp.max(s, axis=1, keepdims=True)
        p = jnp.exp(s - m) * mult
        l = jnp.sum(p, axis=1, keepdims=True)
        pb = p.astype(BF16)
        pv = (lax.dot_general(pb[:, :n_past], cv_ref[0].astype(BF16), _NT, preferred_element_type=F32)
              + jnp.dot(pb[:, n_past:], vn, preferred_element_type=F32))
        o = pv / l
        for pp in range(pairs):
            lanes = slice(pp * LANES, (pp + 1) * LANES)
            for g in range(Q_PER_KV):
                r0 = (pp * Q_PER_KV + g) * 2 * t
                att_ref[pp * Q_PER_KV + g] = jnp.where(e0, o[r0:r0 + t, lanes], o[r0 + t:r0 + 2 * t, lanes])

    pieces = [attend]
    for pp in range(N_KV_HEADS // 2):
        rows = slice(pp * LANES, (pp + 1) * LANES)
        pieces += [functools.partial(shift, ck_ref, kn_ref, ko_ref, rows),
                   functools.partial(shift, cv_ref, vn_ref, vo_ref, rows)]
    return pieces


def _key_multiplicity(t, n_past):
    idx = np.concatenate([np.arange(n_past + t), np.full(LANES - t, 10 ** 9)])[None, :]
    dist = n_past + np.arange(t)[:, None] - idx
    mult = np.zeros(dist.shape, np.float32)
    for d in DILATIONS:
        mult += (dist >= 0) & (dist % d == 0) & (dist <= N_KEYS * d)
    return jnp.asarray(np.tile(mult, (2 * N_KV_HEADS, 1)))


def _attn_sample_specs(q4, k_new, v_new, cache_k, cache_v):
    n_seq, _, n_past = cache_k.shape
    t = k_new.shape[0] // n_seq
    mult = _key_multiplicity(t, n_past)
    cache_blk = pl.BlockSpec((1, D_KV, n_past), lambda b: (b, 0, 0))
    new_blk = pl.BlockSpec((t, D_KV), lambda b: (b, 0))
    q_blk = pl.BlockSpec((4, t, LANES), lambda b: (0, b, 0))
    return ([q4, k_new, v_new, cache_k, cache_v, mult],
            [q_blk, new_blk, new_blk, cache_blk, cache_blk, _resident(mult.shape)],
            [q_blk, cache_blk, cache_blk],
            [jax.ShapeDtypeStruct(q4.shape, F32),
             jax.ShapeDtypeStruct(cache_k.shape, F32), jax.ShapeDtypeStruct(cache_v.shape, F32)])


def _conv_silu(delayed, cw_ref, cb_ref, xc_s):
    for j in range(D_XBC // LANES):
        lanes = slice(j * LANES, (j + 1) * LANES)
        y = cb_ref[:, lanes]
        for tap in range(CONV_WIDTH):
            y = y + delayed(CONV_WIDTH - 1 - tap, lanes) * cw_ref[tap:tap + 1, lanes]
        xc_s[:, lanes] = _silu(y)


def _select_sum(x, sel):
    heads = x.shape[0]
    x = jnp.concatenate([x, jnp.zeros((LANES - heads, x.shape[1]), F32)], axis=0)
    total = None
    for _ in range(3):
        piece = x.astype(BF16)
        x = x - piece.astype(F32)
        part = jnp.dot(piece, sel, preferred_element_type=F32)
        total = part if total is None else total + part
    return total[0:heads]


def _to_columns(x_t):
    pad = jnp.zeros((LANES - x_t.shape[0], x_t.shape[1]), F32)
    return jnp.concatenate([x_t, pad], axis=0).T


def _ssd_scalars(dt_raw, dtb_ref, alog_ref, before, seg_end=None):
    dt_t = _softplus(dt_raw.T[0:N_SSM_HEADS] + dtb_ref[...])
    a_t = dt_t * (-jnp.exp(alog_ref[...]))
    a_cum_t = _select_sum(a_t, before)
    if seg_end is None:
        a_end_t = jnp.broadcast_to(a_cum_t[:, CHUNK - 1:CHUNK], a_cum_t.shape)
    else:
        a_end_t = _select_sum(a_cum_t, seg_end)
    to_end_t = jnp.exp(a_end_t - a_cum_t) * dt_t
    return dt_t, _to_columns(a_cum_t), a_cum_t, a_end_t, to_end_t


def _ssd_diag(xc_s, blk, cb, a_cum, a_cum_t, dt_t, seg_mask):
    e0 = lax.broadcasted_iota(jnp.int32, (CHUNK, LANES), 1) < SSM_HEAD_DIM
    x_pair = xc_s[:, blk * LANES:(blk + 1) * LANES].astype(BF16)
    out, grow = [], []
    for e in range(2):
        h = 2 * blk + e
        a_col = jnp.broadcast_to(a_cum[:, h:h + 1], (CHUNK, CHUNK))
        seg = a_col - jnp.broadcast_to(a_cum_t[h:h + 1, :], (CHUNK, CHUNK))
        w = cb * jnp.where(seg_mask, jnp.exp(seg), 0.0) * dt_t[h:h + 1, :]
        out.append(jnp.dot(w.astype(BF16), x_pair, preferred_element_type=F32))
        grow.append(jnp.exp(a_col))
    return jnp.where(e0, out[0], out[1]), jnp.where(e0, grow[0], grow[1])


def _ssd_finish(xc_s, gate_ref, rows, y_parts, dsk_ref, nrm_ref, y_ref):
    blocks_per_group = D_SSM // SSM_GROUPS // LANES
    for g in range(SSM_GROUPS):
        gated = []
        for p in range(blocks_per_group):
            blk = g * blocks_per_group + p
            lanes = slice(blk * LANES, (blk + 1) * LANES)
            y = y_parts[blk] + dsk_ref[:, lanes] * xc_s[:, lanes]
            gated.append(y * gate_ref[rows, lanes])
        ss = sum(jnp.sum(y * y, axis=-1, keepdims=True) for y in gated)
        inv = lax.rsqrt(ss / (D_SSM // SSM_GROUPS) + RMS_EPS)
        for p in range(blocks_per_group):
            blk = g * blocks_per_group + p
            lanes = slice(blk * LANES, (blk + 1) * LANES)
            y_ref[rows, lanes] = (gated[p] * inv * nrm_ref[:, lanes]).astype(y_ref.dtype)


def _ssd_prompt_body(xbc_ref, gate_ref, dt_ref, cw_ref, cb_ref, dtb_ref, alog_ref, dsk_ref, nrm_ref,
                     y_ref, st_ref, st_s, xc_bufs):
    seq = xbc_ref.shape[0]
    st_s[...] = jnp.zeros_like(st_s)
    row = lax.broadcasted_iota(jnp.int32, (CHUNK, CHUNK), 0)
    col = lax.broadcasted_iota(jnp.int32, (CHUNK, CHUNK), 1)
    causal = row >= col
    before = jnp.where(row <= col, 1.0, 0.0).astype(BF16)
    row8 = lax.broadcasted_iota(jnp.int32, (SUBLANES, LANES), 0)
    heads_per_group = N_SSM_HEADS // SSM_GROUPS
    group_rows = heads_per_group * SSM_HEAD_DIM

    def chunk(c, xc_s):
        rows = pl.ds(pl.multiple_of(c * CHUNK, CHUNK), CHUNK)
        prev_rows = pl.ds(pl.multiple_of(jnp.maximum(c * CHUNK - SUBLANES, 0), SUBLANES), SUBLANES)
        have_prev = c > 0

        def delay(x, before8, k):
            rolled = pltpu.roll(x, k, 0)
            head = jnp.where(row8 < k, pltpu.roll(before8, k, 0), rolled[0:SUBLANES])
            return jnp.concatenate([head, rolled[SUBLANES:]], axis=0)

        for j in range(D_XBC // LANES):
            lanes = slice(j * LANES, (j + 1) * LANES)
            w = [cw_ref[tap:tap + 1, lanes] for tap in range(CONV_WIDTH)]
            cur = xbc_ref[rows, lanes]
            prev = jnp.where(have_prev, xbc_ref[prev_rows, lanes], 0.0)
            cur_1 = delay(cur, prev, 1)
            older = cur * w[1] + cur_1 * w[0]
            older_prev = prev * w[1] + pltpu.roll(prev, 1, 0) * w[0]
            y = cb_ref[:, lanes] + (cur * w[3] + cur_1 * w[2]) + delay(older, older_prev, 2)
            xc_s[:, lanes] = _silu(y)
        dt_t, a_cum, a_cum_t, a_end_t, to_end_t = _ssd_scalars(dt_ref[rows, :], dtb_ref, alog_ref, before)
        x_t = xc_s[:, 0:D_SSM].T
        y_parts = []
        for g in range(SSM_GROUPS):
            bg = xc_s[:, D_SSM + g * D_STATE:D_SSM + (g + 1) * D_STATE].astype(BF16)
            cg = xc_s[:, D_SSM + (SSM_GROUPS + g) * D_STATE:D_SSM + (SSM_GROUPS + g + 1) * D_STATE].astype(BF16)
            cb = lax.dot_general(cg, bg, _NT, preferred_element_type=F32)
            grp = slice(g * group_rows, (g + 1) * group_rows)
            st_prev = st_s[grp, :]
            y_off = lax.dot_general(cg, st_prev.astype(BF16), _NT, preferred_element_type=F32)
            for p in range(group_rows // LANES):
                blk = g * (group_rows // LANES) + p
                y_diag, grow = _ssd_diag(xc_s, blk, cb, a_cum, a_cum_t, dt_t, causal)
                y_parts.append(y_diag + y_off[:, p * LANES:(p + 1) * LANES] * grow)
            xw = []
            for h in range(g * heads_per_group, (g + 1) * heads_per_group):
                hr = slice(h * SSM_HEAD_DIM, (h + 1) * SSM_HEAD_DIM)
                xw.append((x_t[hr, :] * to_end_t[h:h + 1, :]).astype(BF16))
            new = jnp.dot(jnp.concatenate(xw, axis=0), bg, preferred_element_type=F32)
            for i, h in enumerate(range(g * heads_per_group, (g + 1) * heads_per_group)):
                hr = slice(h * SSM_HEAD_DIM, (h + 1) * SSM_HEAD_DIM)
                lr = slice(i * SSM_HEAD_DIM, (i + 1) * SSM_HEAD_DIM)
                decay = jnp.exp(jnp.broadcast_to(a_end_t[h:h + 1, 0:1], (SSM_HEAD_DIM, D_STATE)))
                st_s[hr, :] = decay * st_prev[lr, :] + new[lr, :]
        _ssd_finish(xc_s, gate_ref, rows, y_parts, dsk_ref, nrm_ref, y_ref)

    def step(i, carry):
        for j in range(SSD_CHUNKS_PER_STEP):
            chunk(SSD_CHUNKS_PER_STEP * i + j, xc_bufs.at[j])
        return carry

    lax.fori_loop(0, seq // CHUNK // SSD_CHUNKS_PER_STEP, step, 0)
    st_ref[...] = st_s[...]


def _ssd_prompt(xbc, z, dt_raw, n_seq, seq, consts):
    rows = xbc.shape[0]
    blk = lambda w: pl.BlockSpec((seq, w), lambda b: (b, 0))
    state_blk = pl.BlockSpec((None, D_SSM, D_STATE), lambda b: (b, 0, 0))
    return pl.pallas_call(
        _ssd_prompt_body,
        grid=(n_seq,),
        in_specs=[blk(D_XBC), blk(D_SSM), blk(LANES)] + [_resident(c.shape) for c in consts],
        out_specs=[blk(D_SSM), state_blk],
        out_shape=[jax.ShapeDtypeStruct((rows, D_SSM), BF16),
                   jax.ShapeDtypeStruct((n_seq, D_SSM, D_STATE), F32)],
        scratch_shapes=[pltpu.VMEM((D_SSM, D_STATE), F32),
                        pltpu.VMEM((SSD_CHUNKS_PER_STEP, CHUNK, D_XBC), F32)],
        compiler_params=_params(1),
        name="ssd_prompt",
    )(xbc, z, dt_raw, *consts)


def _ssd_sample_body(xbc_ref, pre_ref, z_ref, dt_ref, h0_ref, cw_ref, cb_ref, dtb_ref, alog_ref, dsk_ref,
                     nrm_ref, y_ref, st_ref, xc_s, yoff_s, aend_s):
    t = CHUNK // SAMPLE_SEQS
    row = lax.broadcasted_iota(jnp.int32, (CHUNK, CHUNK), 0)
    col = lax.broadcasted_iota(jnp.int32, (CHUNK, CHUNK), 1)
    same_seq = (row // t) == (col // t)
    seg_mask = same_seq & (row >= col)
    before = jnp.where(same_seq & (row <= col), 1.0, 0.0).astype(BF16)
    seg_end = jnp.where(row == (col // t) * t + (t - 1), 1.0, 0.0).astype(BF16)
    step = lax.broadcasted_iota(jnp.int32, (CHUNK, LANES), 0) % t
    rows = pl.ds(0, CHUNK)
    heads_per_group = N_SSM_HEADS // SSM_GROUPS
    group_rows = heads_per_group * SSM_HEAD_DIM

    def delayed(k, lanes):
        cur = xbc_ref[:, lanes]
        if k == 0:
            return cur
        return jnp.where(step < k, pltpu.roll(pre_ref[:, lanes], (k - t) % CHUNK, 0), pltpu.roll(cur, k, 0))

    _conv_silu(delayed, cw_ref, cb_ref, xc_s)
    dt_t, a_cum, a_cum_t, a_end_t, to_end_t = _ssd_scalars(dt_ref[...], dtb_ref, alog_ref, before, seg_end)
    aend_s[...] = _to_columns(a_end_t)
    x_t = xc_s[:, 0:D_SSM].T
    col_seq = lax.broadcasted_iota(jnp.int32, (SSM_HEAD_DIM, CHUNK), 1) // t
    y_parts = []
    for g in range(SSM_GROUPS):
        bg = xc_s[:, D_SSM + g * D_STATE:D_SSM + (g + 1) * D_STATE].astype(BF16)
        cg_lanes = slice(D_SSM + (SSM_GROUPS + g) * D_STATE, D_SSM + (SSM_GROUPS + g + 1) * D_STATE)
        cg = xc_s[:, cg_lanes].astype(BF16)
        cb = lax.dot_general(cg, bg, _NT, preferred_element_type=F32)
        grp = slice(g * group_rows, (g + 1) * group_rows)
        xw = []
        for h in range(g * heads_per_group, (g + 1) * heads_per_group):
            hr = slice(h * SSM_HEAD_DIM, (h + 1) * SSM_HEAD_DIM)
            xw.append(x_t[hr, :] * to_end_t[h:h + 1, :])

        def per_seq(b, carry, g=g, bg=bg, cg_lanes=cg_lanes, grp=grp, xw=xw):
            seq_rows = pl.ds(pl.multiple_of(b * t, t), t)
            h_prev = h0_ref[b, grp, :]
            cg_b = xc_s[seq_rows, cg_lanes].astype(BF16)
            yoff_s[seq_rows, g * group_rows:(g + 1) * group_rows] = lax.dot_general(
                cg_b, h_prev.astype(BF16), _NT, preferred_element_type=F32)
            mine = col_seq == b
            xw_b = jnp.concatenate([jnp.where(mine, w, 0.0).astype(BF16) for w in xw], axis=0)
            new = jnp.dot(xw_b, bg, preferred_element_type=F32)
            a_last = aend_s[pl.ds(b * t, 1), :]
            for i, h in enumerate(range(g * heads_per_group, (g + 1) * heads_per_group)):
                lr = slice(i * SSM_HEAD_DIM, (i + 1) * SSM_HEAD_DIM)
                decay = jnp.exp(jnp.broadcast_to(a_last[:, h:h + 1], (SSM_HEAD_DIM, D_STATE)))
                st_ref[b, h * SSM_HEAD_DIM:(h + 1) * SSM_HEAD_DIM, :] = decay * h_prev[lr, :] + new[lr, :]
            return carry

        lax.fori_loop(0, SAMPLE_SEQS, per_seq, 0, unroll=4)
        for p in range(group_rows // LANES):
            blk = g * (group_rows // LANES) + p
            lanes = slice(blk * LANES, (blk + 1) * LANES)
            y_diag, grow = _ssd_diag(xc_s, blk, cb, a_cum, a_cum_t, dt_t, seg_mask)
            y_parts.append(y_diag + yoff_s[:, lanes] * grow)
    _ssd_finish(xc_s, z_ref, rows, y_parts, dsk_ref, nrm_ref, y_ref)


def _ssd_sample(xbc, prefix_tiles, z, dt_raw, h0, consts):
    rows = xbc.shape[0]
    n_seq = h0.shape[0]
    blk = lambda w: pl.BlockSpec((CHUNK, w), lambda i: (i, 0))
    state_blk = pl.BlockSpec((SAMPLE_SEQS, D_SSM, D_STATE), lambda i: (i, 0, 0))
    return pl.pallas_call(
        _ssd_sample_body,
        grid=(rows // CHUNK,),
        in_specs=[blk(D_XBC), blk(D_XBC), blk(D_SSM), blk(LANES), state_blk]
                 + [_resident(c.shape) for c in consts],
        out_specs=[blk(D_SSM), state_blk],
        out_shape=[jax.ShapeDtypeStruct((rows, D_SSM), BF16),
                   jax.ShapeDtypeStruct((n_seq, D_SSM, D_STATE), F32)],
        scratch_shapes=[pltpu.VMEM((CHUNK, D_XBC), F32), pltpu.VMEM((CHUNK, D_SSM), F32),
                        pltpu.VMEM((CHUNK, LANES), F32)],
        compiler_params=_params(1),
        name="ssd_sample",
    )(xbc, prefix_tiles, z, dt_raw, h0, *consts)


def _out_mlp_body(x_ref, att_ref, ssm_ref, woa_ref, wos_ref, g_ref, wu_ref, wd_ref, gf_ref, o_ref, side_work=()):
    side_work = list(side_work)
    att = jnp.concatenate([att_ref[j] for j in range(att_ref.shape[0])], axis=1).astype(BF16)
    h = (x_ref[...] + jnp.dot(att, woa_ref[...], preferred_element_type=F32)
         + jnp.dot(ssm_ref[...], wos_ref[...], preferred_element_type=F32))
    hn = _rms(h, g_ref[...]).astype(BF16)
    mlp = None
    for c in range(D_FF // FF_TILE):
        if side_work:
            side_work.pop(0)()
        cols = slice(c * FF_TILE, (c + 1) * FF_TILE)
        u = jnp.maximum(jnp.dot(hn, wu_ref[:, cols], preferred_element_type=F32), 0.0)
        down = jnp.dot((u * u).astype(BF16), wd_ref[cols, :], preferred_element_type=F32)
        mlp = down if mlp is None else mlp + down
    for work in side_work:
        work()
    o_ref[...] = _rms(h + mlp, gf_ref[...])


def _out_mlp_attn_body(*refs):
    n_mlp, n_att = 9, 6
    o_ref = refs[n_mlp + n_att]
    pieces = _attn_sample_pieces(*refs[n_mlp:n_mlp + n_att], *refs[n_mlp + n_att + 1:])
    _out_mlp_body(*refs[:n_mlp], o_ref, side_work=pieces)


def _out_mlp(x2d, att4, ssm, w_out_att, w_out_ssm, norm_mlp, w_up, w_down, norm_final, sample=None):
    rows = x2d.shape[0]
    tm = min(ROW_TILE, rows) if sample is None else rows // sample[3].shape[0]
    row_blk = lambda w: pl.BlockSpec((tm, w), lambda i: (i, 0))
    operands = [x2d, att4, ssm, w_out_att, w_out_ssm, norm_mlp, w_up, w_down, norm_final]
    in_specs = [row_blk(D_MODEL), pl.BlockSpec((4, tm, LANES), lambda i: (0, i, 0)), row_blk(D_SSM),
                _resident((D_ATT, D_MODEL)), _resident((D_SSM, D_MODEL)), _resident((1, D_MODEL)),
                _resident((D_MODEL, D_FF)), _resident((D_FF, D_MODEL)), _resident((1, D_MODEL))]
    out_specs = [row_blk(D_MODEL)]
    out_shape = [jax.ShapeDtypeStruct((rows, D_MODEL), F32)]
    body = _out_mlp_body
    if sample is not None:
        att_operands, att_in, att_out, att_shape = _attn_sample_specs(*sample)
        assert rows == tm * sample[3].shape[0] and tm % SUBLANES == 0
        operands += att_operands
        in_specs += att_in
        out_specs += att_out
        out_shape += att_shape
        body = _out_mlp_attn_body
    out = pl.pallas_call(
        body,
        grid=(rows // tm,),
        in_specs=in_specs,
        out_specs=out_specs,
        out_shape=out_shape,
        compiler_params=_params(1),
        name="out_mlp",
    )(*operands)
    return out[0] if sample is None else out


def _q_permutation():
    perm = []
    for pp in range(N_KV_HEADS // 2):
        for g in range(Q_PER_KV):
            for e in range(2):
                base = (2 * pp + e) * Q_PER_KV * HEAD_DIM + g * HEAD_DIM
                perm += list(range(base, base + HEAD_DIM))
    return np.asarray(perm, np.int32)


def _to_lanes(buf):
    n, pos = buf.shape[:2]
    return jnp.transpose(buf, (0, 2, 3, 1)).reshape(n, D_KV, pos)


def _from_lanes(buf_t):
    n, _, pos = buf_t.shape
    return jnp.transpose(buf_t.reshape(n, N_KV_HEADS, HEAD_DIM, pos), (0, 3, 1, 2))[None]


def _head_rows(v):
    return jnp.broadcast_to(v.astype(F32)[:, None], (v.shape[0], LANES))


def kernel(x_prompt, x_sample, cache_k, cache_v, state_conv, state_ssm, w_in, w_out, conv_w, conv_b, dt_bias, a_log, d_skip, ssm_norm, norm_mix, norm_mlp, w_up, w_down, norm_final):
    depth = w_in.shape[0]
    assert depth == 1, "single-layer step"
    n_p, seq, _ = x_prompt.shape
    n_s, t_new, _ = x_sample.shape
    n_past = cache_k.shape[2]
    assert seq == MAX_WINDOW and n_past == MAX_WINDOW and t_new == SUBLANES and n_s % SAMPLE_SEQS == 0

    perm = _q_permutation()
    w = w_in[0]
    w_pad = jnp.pad(w.astype(BF16), ((0, 0), (0, D_IN_PAD - w.shape[1])))
    w_out_att = w_out[0][perm, :].astype(BF16)
    w_out_ssm = w_out[0][D_ATT:, :].astype(BF16)
    w_up_b, w_down_b = w_up[0].astype(BF16), w_down[0].astype(BF16)
    g_mix, g_mlp, g_fin = (v.reshape(1, D_MODEL) for v in (norm_mix[0], norm_mlp[0], norm_final))
    ssd_consts = (conv_w[0], conv_b[0].reshape(1, D_XBC), _head_rows(dt_bias[0]), _head_rows(a_log[0]),
                  jnp.repeat(d_skip[0], SSM_HEAD_DIM).reshape(1, D_SSM), ssm_norm[0].reshape(1, D_SSM))

    xp = x_prompt.reshape(n_p * seq, D_MODEL)
    q4, k, v, gate, xbc, dt_raw, k_t, v_t = _in_proj(xp, g_mix, w_pad, _rope_table(np.arange(seq)), seq=seq)
    att4 = _attn_prompt(q4, k, v, n_p, seq)
    y_ssm, st_p = _ssd_prompt(xbc, gate, dt_raw, n_p, seq, ssd_consts)
    k_prompt, v_prompt = _from_lanes(k_t), _from_lanes(v_t)
    conv_prompt = xbc.reshape(1, n_p, seq, D_XBC)[:, :, seq - (CONV_WIDTH - 1):]

    xs = x_sample.reshape(n_s * t_new, D_MODEL)
    rows_tile = min(IN_ROW_TILE, n_s * t_new)
    pos = PAST_LEN + (np.arange(rows_tile) % t_new)
    q4s, ks, vs, zs, xbcs, dts = _in_proj(xs, g_mix, w_pad, _rope_table(pos))
    y_prompt, att4s, k_sample, v_sample = _out_mlp(
        xp, att4, y_ssm, w_out_att, w_out_ssm, g_mlp, w_up_b, w_down_b, g_fin,
        sample=(q4s, ks, vs, _to_lanes(cache_k[0]), _to_lanes(cache_v[0])))
    prefix_tiles = jnp.pad(state_conv[0], ((0, 0), (t_new - (CONV_WIDTH - 1), 0), (0, 0))).reshape(n_s * t_new, D_XBC)
    y_ssm_s, st_s = _ssd_sample(xbcs, prefix_tiles, zs, dts, state_ssm[0].reshape(n_s, D_SSM, D_STATE), ssd_consts)
    y_sample = _out_mlp(xs, att4s, y_ssm_s, w_out_att, w_out_ssm, g_mlp, w_up_b, w_down_b, g_fin)
    conv_sample = xbcs.reshape(1, n_s, t_new, D_XBC)[:, :, t_new - (CONV_WIDTH - 1):]

    return (y_prompt.reshape(n_p, seq, D_MODEL), y_sample.reshape(n_s, t_new, D_MODEL),
            k_prompt, v_prompt, conv_prompt,
            st_p.reshape(1, n_p, N_SSM_HEADS, SSM_HEAD_DIM, D_STATE),
            _from_lanes(k_sample), _from_lanes(v_sample), conv_sample,
            st_s.reshape(1, n_s, N_SSM_HEADS, SSM_HEAD_DIM, D_STATE))
```

```python
import functools

import numpy as np
import jax
import jax.numpy as jnp
from jax import lax
from jax.experimental import pallas as pl
from jax.experimental.pallas import tpu as pltpu

F32 = jnp.float32
BF16 = jnp.bfloat16

LANES = 128
SUBLANES = 8

D_MODEL = 1024
HEAD_DIM = 64
N_KV_HEADS = 4
Q_PER_KV = 2
D_ATT = 512
D_KV = 256
ROT_HALF = 8
ROPE_THETA = 500000.0
DILATIONS = (1, 4, 16)
N_KEYS = 128
MAX_WINDOW = 2048
ATT_BLK = 128
ATT_UNITS = 16
REGROUP_STRIDE = 4
RESIDUES = 16
PAST_LEN = 8192
ATT_SCALE = HEAD_DIM ** -0.5
NEG_BIG = -1e30
N_SSM_HEADS = 8
SSM_HEAD_DIM = 64
D_SSM = 512
SSM_GROUPS = 2
D_STATE = 128
CONV_WIDTH = 4
CHUNK = 128
D_XBC = 1024
D_FF = 4096
RMS_EPS = 1e-5
D_IN_MAIN = D_ATT + 2 * D_KV + D_SSM + D_XBC
D_IN_PAD = D_IN_MAIN + LANES

VMEM_LIMIT = 56 * 1024 * 1024
ROW_TILE = 512
IN_ROW_TILE = 1024
FF_TILE = 1024
SSD_CHUNKS_PER_STEP = 4
SAMPLE_SEQS = CHUNK // 8

_NT = (((1,), (1,)), ((), ()))


def _params(n_axes):
    return pltpu.CompilerParams(dimension_semantics=("arbitrary",) * n_axes,
                                vmem_limit_bytes=VMEM_LIMIT)


def _resident(shape):
    return pl.BlockSpec(shape, lambda *_: (0,) * len(shape), pipeline_mode=pl.Buffered(1))


def _rms(x, g):
    return x * lax.rsqrt(jnp.mean(x * x, axis=-1, keepdims=True) + RMS_EPS) * g


def _silu(x):
    half = 0.5 * x
    return half + half * jnp.tanh(half)


def _softplus(x):
    return jnp.maximum(x, 0.0) + jnp.log1p(jnp.exp(-jnp.abs(x)))


def _in_proj_body(x_ref, g_ref, w_ref, rope_ref, q_ref, k_ref, v_ref, z_ref, xbc_ref, dt_ref, *kv_t_refs):
    xb = _rms(x_ref[...], g_ref[...]).astype(BF16)
    tm = xb.shape[0]
    first_half = (lax.broadcasted_iota(jnp.int32, (tm, LANES), 1) % HEAD_DIM) < ROT_HALF

    def proj(lo, hi):
        return jnp.dot(xb, w_ref[:, lo:hi], preferred_element_type=F32)

    def rope(u, cos, sin):
        partner = jnp.where(first_half, pltpu.roll(u, LANES - ROT_HALF, 1), pltpu.roll(u, ROT_HALF, 1))
        return u * cos + partner * sin

    cq, sq = rope_ref[:, 0:128], rope_ref[:, 128:256]
    ck, sk = rope_ref[:, 256:384], rope_ref[:, 384:512]
    q = proj(0, D_ATT)
    q = [rope(q[:, j * LANES:(j + 1) * LANES], cq, sq) for j in range(D_ATT // LANES)]
    low_half = lax.broadcasted_iota(jnp.int32, (tm, LANES), 1) < HEAD_DIM
    for pp in range(N_KV_HEADS // 2):
        head0, head1 = q[2 * pp], q[2 * pp + 1]
        q_ref[Q_PER_KV * pp] = jnp.where(low_half, head0, pltpu.roll(head1, HEAD_DIM, 1))
        q_ref[Q_PER_KV * pp + 1] = jnp.where(low_half, pltpu.roll(head0, HEAD_DIM, 1), head1)
    k = proj(D_ATT, D_ATT + D_KV)
    v = proj(D_ATT + D_KV, D_ATT + 2 * D_KV)
    v_ref[...] = v
    for j in range(D_KV // LANES):
        lanes = slice(j * LANES, (j + 1) * LANES)
        kj = rope(k[:, lanes], ck, sk)
        k_ref[:, lanes] = kj
        if kv_t_refs:
            kv_t_refs[0][lanes, :] = kj.T
            kv_t_refs[1][lanes, :] = v[:, lanes].T
    z_ref[...] = _silu(proj(D_ATT + 2 * D_KV, D_ATT + 2 * D_KV + D_SSM))
    xbc_ref[...] = proj(D_ATT + 2 * D_KV + D_SSM, D_IN_MAIN)
    dt_ref[...] = proj(D_IN_MAIN, D_IN_PAD)


def _in_proj(x2d, norm_mix, w_pad, rope_tab, seq=None):
    rows = x2d.shape[0]
    tm = min(IN_ROW_TILE, rows)
    n_tab = rope_tab.shape[0] // tm
    row_blk = lambda w: pl.BlockSpec((tm, w), lambda i: (i, 0))
    out_specs = [pl.BlockSpec((4, tm, LANES), lambda i: (0, i, 0)),
                 row_blk(D_KV), row_blk(D_KV), row_blk(D_SSM), row_blk(D_XBC), row_blk(LANES)]
    out_shape = [jax.ShapeDtypeStruct((4, rows, LANES), F32),
                 jax.ShapeDtypeStruct((rows, D_KV), F32), jax.ShapeDtypeStruct((rows, D_KV), F32),
                 jax.ShapeDtypeStruct((rows, D_SSM), F32), jax.ShapeDtypeStruct((rows, D_XBC), F32),
                 jax.ShapeDtypeStruct((rows, LANES), F32)]
    if seq is not None:
        per_seq = seq // tm
        t_blk = pl.BlockSpec((None, D_KV, tm), lambda i: (i // per_seq, 0, i % per_seq))
        out_specs += [t_blk, t_blk]
        out_shape += [jax.ShapeDtypeStruct((rows // seq, D_KV, seq), F32)] * 2
    return pl.pallas_call(
        _in_proj_body,
        grid=(rows // tm,),
        in_specs=[row_blk(D_MODEL), _resident((1, D_MODEL)), _resident((D_MODEL, D_IN_PAD)),
                  pl.BlockSpec((tm, 4 * LANES), lambda i: (i % n_tab, 0))],
        out_specs=out_specs,
        out_shape=out_shape,
        compiler_params=_params(1),
        name="in_proj",
    )(x2d, norm_mix, w_pad, rope_tab)


def _rope_table(pos):
    pos = np.asarray(pos, np.float64)
    inv = ROPE_THETA ** (-np.arange(0, 2 * ROT_HALF, 2, dtype=np.float64) / (2 * ROT_HALF))
    ang = pos[:, None] * inv[None, :]
    cos, sin = np.cos(ang), np.sin(ang)
    ones = np.ones((pos.shape[0], HEAD_DIM - 2 * ROT_HALF))
    cos_h = np.concatenate([cos, cos, ones], axis=1)
    sin_h = np.concatenate([-sin, sin, 0.0 * ones], axis=1)
    cos_l, sin_l = np.tile(cos_h, (1, 2)), np.tile(sin_h, (1, 2))
    table = np.concatenate([cos_l * ATT_SCALE, sin_l * ATT_SCALE, cos_l, sin_l], axis=1)
    return jnp.asarray(table.astype(np.float32))


def _attn_prompt_body(q_ref, k_ref, v_ref, bias1_ref, bias4_ref, caus_ref, o_ref,
                      q3, k3, vhead3, m3, l3, acc3, nat_s):
    seq = k_ref.shape[0]
    per_res = seq // RESIDUES
    e0 = lax.broadcasted_iota(jnp.int32, (ATT_BLK, LANES), 1) < HEAD_DIM
    zero = jnp.zeros((ATT_BLK, LANES), F32)

    def regroup(src, tmp, dst):
        quarter = seq // REGROUP_STRIDE
        for r in range(REGROUP_STRIDE):
            tmp[r * quarter:(r + 1) * quarter, :] = src[pl.ds(r, quarter, stride=REGROUP_STRIDE), :]
        for r in range(REGROUP_STRIDE):
            for a in range(RESIDUES // REGROUP_STRIDE):
                dst[r + REGROUP_STRIDE * a] = tmp[pl.ds(r * quarter + a, per_res, stride=REGROUP_STRIDE), :]

    for g in range(Q_PER_KV):
        regroup(q_ref.at[g], nat_s.at[g], q3.at[g])
    regroup(k_ref, nat_s.at[Q_PER_KV], k3)
    regroup(v_ref, nat_s.at[Q_PER_KV + 1], vhead3.at[0])
    for r in range(RESIDUES):
        v = vhead3[0, r]
        vhead3[0, r] = jnp.where(e0, v, 0.0)
        vhead3[1, r] = jnp.where(e0, 0.0, v)

    def scores(q_idx, k_idx, bias):
        parts = []
        for g in range(Q_PER_KV):
            qg = q3.at[g][q_idx].reshape(ATT_BLK, LANES)
            parts += [jnp.where(e0, qg, zero), jnp.where(e0, zero, qg)]
        lhs = jnp.concatenate(parts, axis=0).astype(BF16)
        kb = k3[k_idx]
        nk = kb.shape[0] * kb.shape[1]
        kb = kb.reshape(nk, LANES).astype(BF16)
        s = lax.dot_general(lhs, kb, _NT, preferred_element_type=F32)
        s = (s.reshape(4, ATT_BLK, nk) + bias[None]).reshape(4 * ATT_BLK, nk)
        m = jnp.max(s, axis=1, keepdims=True)
        p = jnp.exp(s - m)
        mb = jnp.broadcast_to(m, (4 * ATT_BLK, LANES))
        key_lane = lax.broadcasted_iota(jnp.int32, (nk, LANES), 1) < HEAD_DIM
        ones_of = (jnp.where(key_lane, 1.0, 0.0), jnp.where(key_lane, 0.0, 1.0))
        rhs = jnp.concatenate(
            [jnp.concatenate([vhead3.at[e][k_idx].reshape(nk, LANES), ones_of[e]], axis=1) for e in range(2)],
            axis=0).astype(BF16)
        p2 = jnp.concatenate(
            [jnp.concatenate([p[(2 * g + e) * ATT_BLK:(2 * g + e + 1) * ATT_BLK] for e in range(2)], axis=1)
             for g in range(Q_PER_KV)], axis=0).astype(BF16)
        pvl = jnp.dot(p2, rhs, preferred_element_type=F32)
        out = []
        for g in range(Q_PER_KV):
            r0, r1, r2 = 2 * g * ATT_BLK, (2 * g + 1) * ATT_BLK, (2 * g + 2) * ATT_BLK
            blk = pvl[g * ATT_BLK:(g + 1) * ATT_BLK]
            out.append((jnp.where(e0, mb[r0:r1], mb[r1:r2]), blk[:, LANES:], blk[:, :LANES]))
        return out

    def process(units, first_group):
        state = [(m3.at[g], l3.at[g], acc3.at[g]) for g in range(Q_PER_KV)]
        new = [scores(*unit) for unit in units]
        for i, (q_idx, _, _) in enumerate(units):
            for g in range(Q_PER_KV):
                mg, lg, acc = new[i][g]
                if not first_group:
                    m_old, l_old, acc_old = (ref[q_idx].reshape(ATT_BLK, LANES) for ref in state[g])
                    m_new = jnp.maximum(m_old, mg)
                    a_old = jnp.exp(m_old - m_new)
                    a_new = jnp.exp(mg - m_new)
                    mg, lg, acc = m_new, a_old * l_old + a_new * lg, a_old * acc_old + a_new * acc
                for ref, val in zip(state[g], (mg, lg, acc)):
                    ref[q_idx] = val.reshape(ref[q_idx].shape)

    for gi, (d, bias_ref) in enumerate(zip(DILATIONS, (bias1_ref, bias4_ref, caus_ref))):
        blocks = seq // d // ATT_BLK
        lanes16 = RESIDUES // d
        depth = ATT_BLK // lanes16

        def unit(u, d=d, blocks=blocks, lanes16=lanes16, depth=depth, bias_ref=bias_ref):
            r, n = u // blocks, u % blocks
            lead = pl.ds(r, lanes16, stride=d)
            q_idx = (lead, pl.ds(pl.multiple_of(n * depth, depth), depth))
            if blocks == 1:
                return q_idx, q_idx, bias_ref[...]
            k_idx = (lead, pl.ds(pl.multiple_of(jnp.maximum(n - 1, 0) * depth, depth), 2 * depth))
            return q_idx, k_idx, bias_ref[jnp.minimum(n, 1)]

        def step(i, carry, unit=unit, first_group=(gi == 0)):
            process([unit(ATT_UNITS * i + j) for j in range(ATT_UNITS)], first_group)
            return carry

        lax.fori_loop(0, d * blocks // ATT_UNITS, step, 0)

    quarter = seq // REGROUP_STRIDE
    for g in range(Q_PER_KV):
        tmp, out = nat_s.at[Q_PER_KV + g], nat_s.at[g]
        for r in range(REGROUP_STRIDE):
            for a in range(RESIDUES // REGROUP_STRIDE):
                res = r + REGROUP_STRIDE * a
                tmp[pl.ds(r * quarter + a, per_res, stride=REGROUP_STRIDE), :] = acc3[g, res] / l3[g, res]
        for r in range(REGROUP_STRIDE):
            out[pl.ds(r, quarter, stride=REGROUP_STRIDE), :] = tmp[r * quarter:(r + 1) * quarter, :]
        o_ref[g] = out[...].astype(o_ref.dtype)


def _window_bias(d):
    lanes16 = RESIDUES // d

    def order(n):
        a, ll = np.divmod(np.arange(n), n // lanes16)
        return lanes16 * ll + a

    i = order(ATT_BLK)[:, None]
    if d == RESIDUES:
        return jnp.asarray(np.where(order(ATT_BLK)[None, :] <= i, 0.0, NEG_BIG).astype(np.float32))
    j = order(2 * ATT_BLK)[None, :]
    dist = i + ATT_BLK - j
    inner = np.where((dist >= 0) & (dist <= N_KEYS), 0.0, NEG_BIG)
    first = np.where(j <= i, 0.0, NEG_BIG)
    return jnp.asarray(np.stack([first, inner]).astype(np.float32))


def _attn_prompt(q4, k, v, n_seq, seq):
    biases = [_window_bias(d) for d in DILATIONS]
    slab = (RESIDUES, seq // RESIDUES, LANES)
    return pl.pallas_call(
        _attn_prompt_body,
        grid=(n_seq, 2),
        in_specs=[pl.BlockSpec((2, seq, LANES), lambda b, pp: (pp, b, 0)),
                  pl.BlockSpec((seq, LANES), lambda b, pp: (b, pp)),
                  pl.BlockSpec((seq, LANES), lambda b, pp: (b, pp))] + [_resident(x.shape) for x in biases],
        out_specs=pl.BlockSpec((2, seq, LANES), lambda b, pp: (pp, b, 0)),
        out_shape=jax.ShapeDtypeStruct(q4.shape, BF16),
        scratch_shapes=[pltpu.VMEM((2,) + slab, F32), pltpu.VMEM(slab, F32), pltpu.VMEM((2,) + slab, F32)]
                       + [pltpu.VMEM((2,) + slab, F32)] * 3 + [pltpu.VMEM((Q_PER_KV + 2, seq, LANES), F32)],
        compiler_params=_params(2),
        name="attn_prompt",
    )(q4, k, v, *biases)


def _attn_sample_pieces(q_ref, kn_ref, vn_ref, ck_ref, cv_ref, mult_ref, att_ref, ko_ref, vo_ref):
    t = kn_ref.shape[0]
    n_past = ck_ref.shape[2]
    tail = slice(n_past - LANES, n_past)

    def shift(c_ref, n_ref, o_ref, rows):
        is_new = lax.broadcasted_iota(jnp.int32, (LANES, LANES), 1) >= LANES - t
        moved = pltpu.roll(c_ref[0, rows, :], n_past - t, 1)
        o_ref[0, rows, :] = moved
        new_rows = jnp.concatenate([n_ref[:, rows], jnp.zeros((LANES - t, LANES), F32)], axis=0)
        o_ref[0, rows, tail] = jnp.where(is_new, pltpu.roll(new_rows.T, LANES - t, 1), moved[:, tail])

    def attend():
        mult = mult_ref[...]
        e0 = lax.broadcasted_iota(jnp.int32, (t, LANES), 1) < HEAD_DIM
        zero = jnp.zeros((t, LANES), F32)
        pairs = N_KV_HEADS // 2
        parts = []
        for pp in range(pairs):
            for g in range(Q_PER_KV):
                qg = q_ref[pp * Q_PER_KV + g]
                for masked in (jnp.where(e0, qg, zero), jnp.where(e0, zero, qg)):
                    parts.append(jnp.concatenate([masked if j == pp else zero for j in range(pairs)], axis=1))
        lhs = jnp.concatenate(parts, axis=0).astype(BF16)
        pad = jnp.zeros((LANES - t, D_KV), F32)
        kn = jnp.concatenate([kn_ref[...], pad], axis=0).astype(BF16)
        vn = jnp.concatenate([vn_ref[...], pad], axis=0).astype(BF16)
        s = jnp.concatenate([jnp.dot(lhs, ck_ref[0].astype(BF16), preferred_element_type=F32),
                             lax.dot_general(lhs, kn, _NT, preferred_element_type=F32)], axis=1)
        s = jnp.where(mult > 0.0, s, NEG_BIG)
        m = jnp.max(s, axis=1, keepdims=True)
        p = jnp.exp(s - m) * mult
        l = jnp.sum(p, axis=1, keepdims=True)
        pb = p.astype(BF16)
        pv = (lax.dot_general(pb[:, :n_past], cv_ref[0].astype(BF16), _NT, preferred_element_type=F32)
              + jnp.dot(pb[:, n_past:], vn, preferred_element_type=F32))
        o = pv / l
        for pp in range(pairs):
            lanes = slice(pp * LANES, (pp + 1) * LANES)
            for g in range(Q_PER_KV):
                r0 = (pp * Q_PER_KV + g) * 2 * t
                att_ref[pp * Q_PER_KV + g] = jnp.where(e0, o[r0:r0 + t, lanes], o[r0 + t:r0 + 2 * t, lanes])

    pieces = [attend]
    for pp in range(N_KV_HEADS // 2):
        rows = slice(pp * LANES, (pp + 1) * LANES)
        pieces += [functools.partial(shift, ck_ref, kn_ref, ko_ref, rows),
                   functools.partial(shift, cv_ref, vn_ref, vo_ref, rows)]
    return pieces


def _key_multiplicity(t, n_past):
    idx = np.concatenate([np.arange(n_past + t), np.full(LANES - t, 10 ** 9)])[None, :]
    dist = n_past + np.arange(t)[:, None] - idx
    mult = np.zeros(dist.shape, np.float32)
    for d in DILATIONS:
        mult += (dist >= 0) & (dist % d == 0) & (dist <= N_KEYS * d)
    return jnp.asarray(np.tile(mult, (2 * N_KV_HEADS, 1)))


def _attn_sample_specs(q4, k_new, v_new, cache_k, cache_v):
    n_seq, _, n_past = cache_k.shape
    t = k_new.shape[0] // n_seq
    mult = _key_multiplicity(t, n_past)
    cache_blk = pl.BlockSpec((1, D_KV, n_past), lambda b: (b, 0, 0))
    new_blk = pl.BlockSpec((t, D_KV), lambda b: (b, 0))
    q_blk = pl.BlockSpec((4, t, LANES), lambda b: (0, b, 0))
    return ([q4, k_new, v_new, cache_k, cache_v, mult],
            [q_blk, new_blk, new_blk, cache_blk, cache_blk, _resident(mult.shape)],
            [q_blk, cache_blk, cache_blk],
            [jax.ShapeDtypeStruct(q4.shape, F32),
             jax.ShapeDtypeStruct(cache_k.shape, F32), jax.ShapeDtypeStruct(cache_v.shape, F32)])


def _conv_silu(delayed, cw_ref, cb_ref, xc_s):
    for j in range(D_XBC // LANES):
        lanes = slice(j * LANES, (j + 1) * LANES)
        y = cb_ref[:, lanes]
        for tap in range(CONV_WIDTH):
            y = y + delayed(CONV_WIDTH - 1 - tap, lanes) * cw_ref[tap:tap + 1, lanes]
        xc_s[:, lanes] = _silu(y)


def _select_sum(x, sel):
    heads = x.shape[0]
    x = jnp.concatenate([x, jnp.zeros((LANES - heads, x.shape[1]), F32)], axis=0)
    total = None
    for _ in range(3):
        piece = x.astype(BF16)
        x = x - piece.astype(F32)
        part = jnp.dot(piece, sel, preferred_element_type=F32)
        total = part if total is None else total + part
    return total[0:heads]


def _to_columns(x_t):
    pad = jnp.zeros((LANES - x_t.shape[0], x_t.shape[1]), F32)
    return jnp.concatenate([x_t, pad], axis=0).T


def _ssd_scalars(dt_raw, dtb_ref, alog_ref, before, seg_end=None):
    dt_t = _softplus(dt_raw.T[0:N_SSM_HEADS] + dtb_ref[...])
    a_t = dt_t * (-jnp.exp(alog_ref[...]))
    a_cum_t = _select_sum(a_t, before)
    if seg_end is None:
        a_end_t = jnp.broadcast_to(a_cum_t[:, CHUNK - 1:CHUNK], a_cum_t.shape)
    else:
        a_end_t = _select_sum(a_cum_t, seg_end)
    to_end_t = jnp.exp(a_end_t - a_cum_t) * dt_t
    return dt_t, _to_columns(a_cum_t), a_cum_t, a_end_t, to_end_t


def _ssd_diag(xc_s, blk, cb, a_cum, a_cum_t, dt_t, seg_mask):
    e0 = lax.broadcasted_iota(jnp.int32, (CHUNK, LANES), 1) < SSM_HEAD_DIM
    x_pair = xc_s[:, blk * LANES:(blk + 1) * LANES].astype(BF16)
    out, grow = [], []
    for e in range(2):
        h = 2 * blk + e
        a_col = jnp.broadcast_to(a_cum[:, h:h + 1], (CHUNK, CHUNK))
        seg = a_col - jnp.broadcast_to(a_cum_t[h:h + 1, :], (CHUNK, CHUNK))
        w = cb * jnp.where(seg_mask, jnp.exp(seg), 0.0) * dt_t[h:h + 1, :]
        out.append(jnp.dot(w.astype(BF16), x_pair, preferred_element_type=F32))
        grow.append(jnp.exp(a_col))
    return jnp.where(e0, out[0], out[1]), jnp.where(e0, grow[0], grow[1])


def _ssd_finish(xc_s, gate_ref, rows, y_parts, dsk_ref, nrm_ref, y_ref):
    blocks_per_group = D_SSM // SSM_GROUPS // LANES
    for g in range(SSM_GROUPS):
        gated = []
        for p in range(blocks_per_group):
            blk = g * blocks_per_group + p
            lanes = slice(blk * LANES, (blk + 1) * LANES)
            y = y_parts[blk] + dsk_ref[:, lanes] * xc_s[:, lanes]
            gated.append(y * gate_ref[rows, lanes])
        ss = sum(jnp.sum(y * y, axis=-1, keepdims=True) for y in gated)
        inv = lax.rsqrt(ss / (D_SSM // SSM_GROUPS) + RMS_EPS)
        for p in range(blocks_per_group):
            blk = g * blocks_per_group + p
            lanes = slice(blk * LANES, (blk + 1) * LANES)
            y_ref[rows, lanes] = (gated[p] * inv * nrm_ref[:, lanes]).astype(y_ref.dtype)


def _ssd_prompt_body(xbc_ref, before_ref, gate_ref, dt_ref, cw_ref, cb_ref, dtb_ref, alog_ref, dsk_ref, nrm_ref,
                     y_ref, st_ref, st_s, xc_bufs):
    seq = xbc_ref.shape[0]
    first_block = pl.program_id(1) == 0

    @pl.when(first_block)
    def _():
        st_s[...] = jnp.zeros_like(st_s)

    row = lax.broadcasted_iota(jnp.int32, (CHUNK, CHUNK), 0)
    col = lax.broadcasted_iota(jnp.int32, (CHUNK, CHUNK), 1)
    causal = row >= col
    before = jnp.where(row <= col, 1.0, 0.0).astype(BF16)
    row8 = lax.broadcasted_iota(jnp.int32, (SUBLANES, LANES), 0)
    heads_per_group = N_SSM_HEADS // SSM_GROUPS
    group_rows = heads_per_group * SSM_HEAD_DIM

    def chunk(c, xc_s):
        rows = pl.ds(pl.multiple_of(c * CHUNK, CHUNK), CHUNK)
        prev_rows = pl.ds(pl.multiple_of(jnp.maximum(c * CHUNK - SUBLANES, 0), SUBLANES), SUBLANES)
        have_prev = c > 0

        def delay(x, before8, k):
            rolled = pltpu.roll(x, k, 0)
            head = jnp.where(row8 < k, pltpu.roll(before8, k, 0), rolled[0:SUBLANES])
            return jnp.concatenate([head, rolled[SUBLANES:]], axis=0)

        for j in range(D_XBC // LANES):
            lanes = slice(j * LANES, (j + 1) * LANES)
            w = [cw_ref[tap:tap + 1, lanes] for tap in range(CONV_WIDTH)]
            cur = xbc_ref[rows, lanes]
            ahead = jnp.where(first_block, 0.0, before_ref[:, lanes])
            prev = jnp.where(have_prev, xbc_ref[prev_rows, lanes], ahead)
            cur_1 = delay(cur, prev, 1)
            older = cur * w[1] + cur_1 * w[0]
            older_prev = prev * w[1] + pltpu.roll(prev, 1, 0) * w[0]
            y = cb_ref[:, lanes] + (cur * w[3] + cur_1 * w[2]) + delay(older, older_prev, 2)
            xc_s[:, lanes] = _silu(y)
        dt_t, a_cum, a_cum_t, a_end_t, to_end_t = _ssd_scalars(dt_ref[rows, :], dtb_ref, alog_ref, before)
        x_t = xc_s[:, 0:D_SSM].T
        y_parts = []
        for g in range(SSM_GROUPS):
            bg = xc_s[:, D_SSM + g * D_STATE:D_SSM + (g + 1) * D_STATE].astype(BF16)
            cg = xc_s[:, D_SSM + (SSM_GROUPS + g) * D_STATE:D_SSM + (SSM_GROUPS + g + 1) * D_STATE].astype(BF16)
            cb = lax.dot_general(cg, bg, _NT, preferred_element_type=F32)
            grp = slice(g * group_rows, (g + 1) * group_rows)
            st_prev = st_s[grp, :]
            y_off = lax.dot_general(cg, st_prev.astype(BF16), _NT, preferred_element_type=F32)
            for p in range(group_rows // LANES):
                blk = g * (group_rows // LANES) + p
                y_diag, grow = _ssd_diag(xc_s, blk, cb, a_cum, a_cum_t, dt_t, causal)
                y_parts.append(y_diag + y_off[:, p * LANES:(p + 1) * LANES] * grow)
            xw = []
            for h in range(g * heads_per_group, (g + 1) * heads_per_group):
                hr = slice(h * SSM_HEAD_DIM, (h + 1) * SSM_HEAD_DIM)
                xw.append((x_t[hr, :] * to_end_t[h:h + 1, :]).astype(BF16))
            new = jnp.dot(jnp.concatenate(xw, axis=0), bg, preferred_element_type=F32)
            for i, h in enumerate(range(g * heads_per_group, (g + 1) * heads_per_group)):
                hr = slice(h * SSM_HEAD_DIM, (h + 1) * SSM_HEAD_DIM)
                lr = slice(i * SSM_HEAD_DIM, (i + 1) * SSM_HEAD_DIM)
                decay = jnp.exp(jnp.broadcast_to(a_end_t[h:h + 1, 0:1], (SSM_HEAD_DIM, D_STATE)))
                st_s[hr, :] = decay * st_prev[lr, :] + new[lr, :]
        _ssd_finish(xc_s, gate_ref, rows, y_parts, dsk_ref, nrm_ref, y_ref)

    def step(i, carry):
        for j in range(SSD_CHUNKS_PER_STEP):
            chunk(SSD_CHUNKS_PER_STEP * i + j, xc_bufs.at[j])
        return carry

    lax.fori_loop(0, seq // CHUNK // SSD_CHUNKS_PER_STEP, step, 0)
    st_ref[...] = st_s[...]


def _ssd_prompt(xbc, z, dt_raw, n_seq, seq, consts):
    rows = xbc.shape[0]
    block = CHUNK * SSD_CHUNKS_PER_STEP
    per_seq = seq // block
    blk = lambda w: pl.BlockSpec((block, w), lambda b, j: (b * per_seq + j, 0))
    tiles = block // SUBLANES
    before_blk = pl.BlockSpec((SUBLANES, D_XBC), lambda b, j: (jnp.maximum((b * per_seq + j) * tiles - 1, 0), 0))
    state_blk = pl.BlockSpec((None, D_SSM, D_STATE), lambda b, j: (b, 0, 0))
    return pl.pallas_call(
        _ssd_prompt_body,
        grid=(n_seq, per_seq),
        in_specs=[blk(D_XBC), before_blk, blk(D_SSM), blk(LANES)] + [_resident(c.shape) for c in consts],
        out_specs=[blk(D_SSM), state_blk],
        out_shape=[jax.ShapeDtypeStruct((rows, D_SSM), BF16),
                   jax.ShapeDtypeStruct((n_seq, D_SSM, D_STATE), F32)],
        scratch_shapes=[pltpu.VMEM((D_SSM, D_STATE), F32),
                        pltpu.VMEM((SSD_CHUNKS_PER_STEP, CHUNK, D_XBC), F32)],
        compiler_params=_params(2),
        name="ssd_prompt",
    )(xbc, xbc, z, dt_raw, *consts)


def _ssd_sample_body(xbc_ref, pre_ref, z_ref, dt_ref, h0_ref, cw_ref, cb_ref, dtb_ref, alog_ref, dsk_ref,
                     nrm_ref, y_ref, st_ref, xc_s, yoff_s, aend_s):
    t = CHUNK // SAMPLE_SEQS
    row = lax.broadcasted_iota(jnp.int32, (CHUNK, CHUNK), 0)
    col = lax.broadcasted_iota(jnp.int32, (CHUNK, CHUNK), 1)
    same_seq = (row // t) == (col // t)
    seg_mask = same_seq & (row >= col)
    before = jnp.where(same_seq & (row <= col), 1.0, 0.0).astype(BF16)
    seg_end = jnp.where(row == (col // t) * t + (t - 1), 1.0, 0.0).astype(BF16)
    step = lax.broadcasted_iota(jnp.int32, (CHUNK, LANES), 0) % t
    rows = pl.ds(0, CHUNK)
    heads_per_group = N_SSM_HEADS // SSM_GROUPS
    group_rows = heads_per_group * SSM_HEAD_DIM

    def delayed(k, lanes):
        cur = xbc_ref[:, lanes]
        if k == 0:
            return cur
        return jnp.where(step < k, pltpu.roll(pre_ref[:, lanes], (k - t) % CHUNK, 0), pltpu.roll(cur, k, 0))

    _conv_silu(delayed, cw_ref, cb_ref, xc_s)
    dt_t, a_cum, a_cum_t, a_end_t, to_end_t = _ssd_scalars(dt_ref[...], dtb_ref, alog_ref, before, seg_end)
    aend_s[...] = _to_columns(a_end_t)
    x_t = xc_s[:, 0:D_SSM].T
    col_seq = lax.broadcasted_iota(jnp.int32, (SSM_HEAD_DIM, CHUNK), 1) // t
    y_parts = []
    for g in range(SSM_GROUPS):
        bg = xc_s[:, D_SSM + g * D_STATE:D_SSM + (g + 1) * D_STATE].astype(BF16)
        cg_lanes = slice(D_SSM + (SSM_GROUPS + g) * D_STATE, D_SSM + (SSM_GROUPS + g + 1) * D_STATE)
        cg = xc_s[:, cg_lanes].astype(BF16)
        cb = lax.dot_general(cg, bg, _NT, preferred_element_type=F32)
        grp = slice(g * group_rows, (g + 1) * group_rows)
        xw = []
        for h in range(g * heads_per_group, (g + 1) * heads_per_group):
            hr = slice(h * SSM_HEAD_DIM, (h + 1) * SSM_HEAD_DIM)
            xw.append(x_t[hr, :] * to_end_t[h:h + 1, :])

        def per_seq(b, carry, g=g, bg=bg, cg_lanes=cg_lanes, grp=grp, xw=xw):
            seq_rows = pl.ds(pl.multiple_of(b * t, t), t)
            h_prev = h0_ref[b, grp, :]
            cg_b = xc_s[seq_rows, cg_lanes].astype(BF16)
            yoff_s[seq_rows, g * group_rows:(g + 1) * group_rows] = lax.dot_general(
                cg_b, h_prev.astype(BF16), _NT, preferred_element_type=F32)
            mine = col_seq == b
            xw_b = jnp.concatenate([jnp.where(mine, w, 0.0).astype(BF16) for w in xw], axis=0)
            new = jnp.dot(xw_b, bg, preferred_element_type=F32)
            a_last = aend_s[pl.ds(b * t, 1), :]
            for i, h in enumerate(range(g * heads_per_group, (g + 1) * heads_per_group)):
                lr = slice(i * SSM_HEAD_DIM, (i + 1) * SSM_HEAD_DIM)
                decay = jnp.exp(jnp.broadcast_to(a_last[:, h:h + 1], (SSM_HEAD_DIM, D_STATE)))
                st_ref[b, h * SSM_HEAD_DIM:(h + 1) * SSM_HEAD_DIM, :] = decay * h_prev[lr, :] + new[lr, :]
            return carry

        lax.fori_loop(0, SAMPLE_SEQS, per_seq, 0, unroll=4)
        for p in range(group_rows // LANES):
            blk = g * (group_rows // LANES) + p
            lanes = slice(blk * LANES, (blk + 1) * LANES)
            y_diag, grow = _ssd_diag(xc_s, blk, cb, a_cum, a_cum_t, dt_t, seg_mask)
            y_parts.append(y_diag + yoff_s[:, lanes] * grow)
    _ssd_finish(xc_s, z_ref, rows, y_parts, dsk_ref, nrm_ref, y_ref)


def _ssd_sample(xbc, prefix_tiles, z, dt_raw, h0, consts):
    rows = xbc.shape[0]
    n_seq = h0.shape[0]
    blk = lambda w: pl.BlockSpec((CHUNK, w), lambda i: (i, 0))
    state_blk = pl.BlockSpec((SAMPLE_SEQS, D_SSM, D_STATE), lambda i: (i, 0, 0))
    return pl.pallas_call(
        _ssd_sample_body,
        grid=(rows // CHUNK,),
        in_specs=[blk(D_XBC), blk(D_XBC), blk(D_SSM), blk(LANES), state_blk]
                 + [_resident(c.shape) for c in consts],
        out_specs=[blk(D_SSM), state_blk],
        out_shape=[jax.ShapeDtypeStruct((rows, D_SSM), BF16),
                   jax.ShapeDtypeStruct((n_seq, D_SSM, D_STATE), F32)],
        scratch_shapes=[pltpu.VMEM((CHUNK, D_XBC), F32), pltpu.VMEM((CHUNK, D_SSM), F32),
                        pltpu.VMEM((CHUNK, LANES), F32)],
        compiler_params=_params(1),
        name="ssd_sample",
    )(xbc, prefix_tiles, z, dt_raw, h0, *consts)


def _out_mlp_body(x_ref, att_ref, ssm_ref, woa_ref, wos_ref, g_ref, wu_ref, wd_ref, gf_ref, o_ref, side_work=()):
    side_work = list(side_work)
    att = jnp.concatenate([att_ref[j] for j in range(att_ref.shape[0])], axis=1).astype(BF16)
    h = (x_ref[...] + jnp.dot(att, woa_ref[...], preferred_element_type=F32)
         + jnp.dot(ssm_ref[...], wos_ref[...], preferred_element_type=F32))
    hn = _rms(h, g_ref[...]).astype(BF16)
    mlp = None
    for c in range(D_FF // FF_TILE):
        if side_work:
            side_work.pop(0)()
        cols = slice(c * FF_TILE, (c + 1) * FF_TILE)
        u = jnp.maximum(jnp.dot(hn, wu_ref[:, cols], preferred_element_type=F32), 0.0)
        down = jnp.dot((u * u).astype(BF16), wd_ref[cols, :], preferred_element_type=F32)
        mlp = down if mlp is None else mlp + down
    for work in side_work:
        work()
    o_ref[...] = _rms(h + mlp, gf_ref[...])


def _out_mlp_attn_body(*refs):
    n_mlp, n_att = 9, 6
    o_ref = refs[n_mlp + n_att]
    pieces = _attn_sample_pieces(*refs[n_mlp:n_mlp + n_att], *refs[n_mlp + n_att + 1:])
    _out_mlp_body(*refs[:n_mlp], o_ref, side_work=pieces)


def _out_mlp(x2d, att4, ssm, w_out_att, w_out_ssm, norm_mlp, w_up, w_down, norm_final, sample=None):
    rows = x2d.shape[0]
    tm = min(ROW_TILE, rows) if sample is None else rows // sample[3].shape[0]
    row_blk = lambda w: pl.BlockSpec((tm, w), lambda i: (i, 0))
    operands = [x2d, att4, ssm, w_out_att, w_out_ssm, norm_mlp, w_up, w_down, norm_final]
    in_specs = [row_blk(D_MODEL), pl.BlockSpec((4, tm, LANES), lambda i: (0, i, 0)), row_blk(D_SSM),
                _resident((D_ATT, D_MODEL)), _resident((D_SSM, D_MODEL)), _resident((1, D_MODEL)),
                _resident((D_MODEL, D_FF)), _resident((D_FF, D_MODEL)), _resident((1, D_MODEL))]
    out_specs = [row_blk(D_MODEL)]
    out_shape = [jax.ShapeDtypeStruct((rows, D_MODEL), F32)]
    body = _out_mlp_body
    if sample is not None:
        att_operands, att_in, att_out, att_shape = _attn_sample_specs(*sample)
        assert rows == tm * sample[3].shape[0] and tm % SUBLANES == 0
        operands += att_operands
        in_specs += att_in
        out_specs += att_out
        out_shape += att_shape
        body = _out_mlp_attn_body
    out = pl.pallas_call(
        body,
        grid=(rows // tm,),
        in_specs=in_specs,
        out_specs=out_specs,
        out_shape=out_shape,
        compiler_params=_params(1),
        name="out_mlp",
    )(*operands)
    return out[0] if sample is None else out


def _q_permutation():
    perm = []
    for pp in range(N_KV_HEADS // 2):
        for g in range(Q_PER_KV):
            for e in range(2):
                base = (2 * pp + e) * Q_PER_KV * HEAD_DIM + g * HEAD_DIM
                perm += list(range(base, base + HEAD_DIM))
    return np.asarray(perm, np.int32)


def _to_lanes(buf):
    n, pos = buf.shape[:2]
    return jnp.transpose(buf, (0, 2, 3, 1)).reshape(n, D_KV, pos)


def _from_lanes(buf_t):
    n, _, pos = buf_t.shape
    return jnp.transpose(buf_t.reshape(n, N_KV_HEADS, HEAD_DIM, pos), (0, 3, 1, 2))[None]


def _head_rows(v):
    return jnp.broadcast_to(v.astype(F32)[:, None], (v.shape[0], LANES))


def kernel(x_prompt, x_sample, cache_k, cache_v, state_conv, state_ssm, w_in, w_out, conv_w, conv_b, dt_bias, a_log, d_skip, ssm_norm, norm_mix, norm_mlp, w_up, w_down, norm_final):
    depth = w_in.shape[0]
    assert depth == 1, "single-layer step"
    n_p, seq, _ = x_prompt.shape
    n_s, t_new, _ = x_sample.shape
    n_past = cache_k.shape[2]
    assert seq == MAX_WINDOW and n_past == MAX_WINDOW and t_new == SUBLANES and n_s % SAMPLE_SEQS == 0

    perm = _q_permutation()
    w = w_in[0]
    w_pad = jnp.pad(w.astype(BF16), ((0, 0), (0, D_IN_PAD - w.shape[1])))
    w_out_att = w_out[0][perm, :].astype(BF16)
    w_out_ssm = w_out[0][D_ATT:, :].astype(BF16)
    w_up_b, w_down_b = w_up[0].astype(BF16), w_down[0].astype(BF16)
    g_mix, g_mlp, g_fin = (v.reshape(1, D_MODEL) for v in (norm_mix[0], norm_mlp[0], norm_final))
    ssd_consts = (conv_w[0], conv_b[0].reshape(1, D_XBC), _head_rows(dt_bias[0]), _head_rows(a_log[0]),
                  jnp.repeat(d_skip[0], SSM_HEAD_DIM).reshape(1, D_SSM), ssm_norm[0].reshape(1, D_SSM))

    xp = x_prompt.reshape(n_p * seq, D_MODEL)
    q4, k, v, gate, xbc, dt_raw, k_t, v_t = _in_proj(xp, g_mix, w_pad, _rope_table(np.arange(seq)), seq=seq)
    att4 = _attn_prompt(q4, k, v, n_p, seq)
    y_ssm, st_p = _ssd_prompt(xbc, gate, dt_raw, n_p, seq, ssd_consts)
    k_prompt, v_prompt = _from_lanes(k_t), _from_lanes(v_t)
    conv_prompt = xbc.reshape(1, n_p, seq, D_XBC)[:, :, seq - (CONV_WIDTH - 1):]

    xs = x_sample.reshape(n_s * t_new, D_MODEL)
    rows_tile = min(IN_ROW_TILE, n_s * t_new)
    pos = PAST_LEN + (np.arange(rows_tile) % t_new)
    q4s, ks, vs, zs, xbcs, dts = _in_proj(xs, g_mix, w_pad, _rope_table(pos))
    y_prompt, att4s, k_sample, v_sample = _out_mlp(
        xp, att4, y_ssm, w_out_att, w_out_ssm, g_mlp, w_up_b, w_down_b, g_fin,
        sample=(q4s, ks, vs, _to_lanes(cache_k[0]), _to_lanes(cache_v[0])))
    prefix_tiles = jnp.pad(state_conv[0], ((0, 0), (t_new - (CONV_WIDTH - 1), 0), (0, 0))).reshape(n_s * t_new, D_XBC)
    y_ssm_s, st_s = _ssd_sample(xbcs, prefix_tiles, zs, dts, state_ssm[0].reshape(n_s, D_SSM, D_STATE), ssd_consts)
    y_sample = _out_mlp(xs, att4s, y_ssm_s, w_out_att, w_out_ssm, g_mlp, w_up_b, w_down_b, g_fin)
    conv_sample = xbcs.reshape(1, n_s, t_new, D_XBC)[:, :, t_new - (CONV_WIDTH - 1):]

    return (y_prompt.reshape(n_p, seq, D_MODEL), y_sample.reshape(n_s, t_new, D_MODEL),
            k_prompt, v_prompt, conv_prompt,
            st_p.reshape(1, n_p, N_SSM_HEADS, SSM_HEAD_DIM, D_STATE),
            _from_lanes(k_sample), _from_lanes(v_sample), conv_sample,
            st_s.reshape(1, n_s, N_SSM_HEADS, SSM_HEAD_DIM, D_STATE))
```
